```python
import math
import jax, jax.numpy as jnp
from jax import lax
import numpy as np

D_MODEL = 1024
BATCH = 32
SEQ = 2048
DEPTH = 4
DEC_BATCH = 16
DEC_SEQ = 4096
PAST_LEN = 128

DIFF_HEADS = 4
DIFF_HEAD_DIM = 64
DIFF_WIDTH = DIFF_HEADS * 2 * DIFF_HEAD_DIM
NA_HEADS = 8
NA_HEAD_DIM = 64
NA_WIDTH = NA_HEADS * NA_HEAD_DIM
GRID_W = 64
NA_WIN_ROWS = 8
NA_WIN_COLS = 16
NA_QCB = NA_WIN_COLS
NA_KCB = 2 * NA_WIN_COLS
NA_NCB = GRID_W // NA_QCB
GATE_WIDTH = 2 * D_MODEL
IN_SPLITS = [DIFF_WIDTH, 2 * DIFF_WIDTH, 3 * DIFF_WIDTH, 3 * DIFF_WIDTH + NA_WIDTH, 3 * DIFF_WIDTH + 2 * NA_WIDTH, 3 * DIFF_WIDTH + 3 * NA_WIDTH]
IN_WIDTH = 3 * DIFF_WIDTH + 3 * NA_WIDTH + GATE_WIDTH
T5_BUCKETS = 32
T5_MAX_DIST = 128
Q_BLOCK = 128
N_EXPERTS = 16
EXPERT_FF = 2816
CAPACITY_FACTOR = 2
EPS = 1e-6

kernel_name = "hybrid_diffattn_natten_ec_encoder"


def rms_norm(x, g):
    x32 = x.astype(jnp.float32)
    y = x32 * lax.rsqrt(jnp.mean(x32 * x32, axis=-1, keepdims=True) + EPS)
    return (y * g.astype(jnp.float32)).astype(x.dtype)


def t5_bucket(rel):
    nb = T5_BUCKETS // 2
    max_exact = nb // 2
    side = jnp.where(rel > 0, nb, 0)
    n = jnp.abs(rel)
    nf = jnp.maximum(n, 1).astype(jnp.float32)
    large = max_exact + (jnp.log(nf / max_exact) / math.log(T5_MAX_DIST / max_exact) * (nb - max_exact)).astype(jnp.int32)
    large = jnp.minimum(large, nb - 1)
    return side + jnp.where(n < max_exact, n, large)


def diff_attention(q1, q2, k1, k2, v, lam, rel_bias):
    B, H, S, hd = q1.shape
    nb = S // Q_BLOCK
    kpos = jnp.arange(S, dtype=jnp.int32)
    table = rel_bias.astype(jnp.float32)

    def to_blocks(q):
        return jnp.moveaxis(q.reshape(B, H, nb, Q_BLOCK, hd), 2, 0)

    def block(args):
        qa, qb, i = args
        qpos = i * Q_BLOCK + jnp.arange(Q_BLOCK, dtype=jnp.int32)
        bias = jnp.transpose(table[t5_bucket(kpos[None, :] - qpos[:, None])], (2, 0, 1))
        p1 = jax.nn.softmax(jnp.einsum('bhqd,bhkd->bhqk', qa, k1).astype(jnp.float32) + bias, axis=-1)
        p2 = jax.nn.softmax(jnp.einsum('bhqd,bhkd->bhqk', qb, k2).astype(jnp.float32) + bias, axis=-1)
        a = (p1 - lam * p2).astype(v.dtype)
        return jnp.einsum('bhqk,bhke->bhqe', a, v)

    out = lax.map(block, (to_blocks(q1), to_blocks(q2), jnp.arange(nb, dtype=jnp.int32)))
    return jnp.moveaxis(out, 0, 2).reshape(B, H, S, v.shape[-1])


def na_column_structure():
    c = np.arange(GRID_W)
    col_start = np.clip(c - NA_WIN_COLS // 2, 0, GRID_W - NA_WIN_COLS).reshape(NA_NCB, NA_QCB)
    kblk_start = np.clip(np.arange(NA_NCB) * NA_QCB - NA_WIN_COLS // 2, 0, GRID_W - NA_KCB)
    kcol_idx = kblk_start[:, None] + np.arange(NA_KCB)[None, :]
    kc = kcol_idx[:, None, :]
    cs = col_start[:, :, None]
    col_valid = (kc >= cs) & (kc < cs + NA_WIN_COLS)
    dc = kc - c.reshape(NA_NCB, NA_QCB)[:, :, None]
    dc_idx = np.clip(dc + NA_WIN_COLS - 1, 0, 2 * NA_WIN_COLS - 2)
    return kcol_idx, col_valid, dc_idx


def neighborhood_attention(q, k, v, rpb):
    B, S, H, hd = q.shape
    rows = S // GRID_W
    wr = min(NA_WIN_ROWS, rows)
    kcol_idx, col_valid, dc_idx = na_column_structure()
    kg = k.reshape(B, rows, GRID_W, H, hd)
    vg = v.reshape(B, rows, GRID_W, H, hd)
    qg = jnp.moveaxis(q.reshape(B, rows, GRID_W, H, hd), 1, 0)
    table = rpb.astype(jnp.float32)
    mask = col_valid[:, :, None, :]

    def row_fn(args):
        q_row, r = args
        rs = jnp.clip(r - wr // 2, 0, rows - wr)
        k_blk = lax.dynamic_slice_in_dim(kg, rs, wr, axis=1)[:, :, kcol_idx]
        v_blk = lax.dynamic_slice_in_dim(vg, rs, wr, axis=1)[:, :, kcol_idx]
        qb = q_row.reshape(B, NA_NCB, NA_QCB, H, hd)
        logits = jnp.einsum('bjqhd,bwjkhd->bhjqwk', qb, k_blk).astype(jnp.float32)
        dr_idx = rs + jnp.arange(wr, dtype=jnp.int32) - r + (NA_WIN_ROWS - 1)
        bias = jnp.transpose(table[:, dr_idx][:, :, dc_idx], (0, 2, 3, 1, 4))
        logits = jnp.where(mask, logits + bias, -jnp.inf)
        p = jax.nn.softmax(logits, axis=(-2, -1)).astype(v.dtype)
        out = jnp.einsum('bhjqwk,bwjkhd->bjqhd', p, v_blk)
        return out.reshape(B, GRID_W, H, hd)

    out = lax.map(row_fn, (qg, jnp.arange(rows, dtype=jnp.int32)))
    return jnp.moveaxis(out, 0, 1).reshape(B, S, H, hd)


def token_mixer(x, lambda_init, norm_g, w_in, diff_qn, diff_kn, lq1, lk1, lq2, lk2, diff_subln, rel_bias,
                na_qn, na_kn, na_rpb, w_br_diff, w_br_na, w_out):
    B, S, _ = x.shape
    h = rms_norm(x, norm_g)
    proj = h @ w_in
    dq, dk, dv, nq, nk, nv, gates = jnp.split(proj, IN_SPLITS, axis=-1)
    dq = rms_norm(dq.reshape(B, S, DIFF_HEADS, 2, DIFF_HEAD_DIM), diff_qn) * (DIFF_HEAD_DIM ** -0.5)
    dk = rms_norm(dk.reshape(B, S, DIFF_HEADS, 2, DIFF_HEAD_DIM), diff_kn)
    q1 = jnp.transpose(dq[..., 0, :], (0, 2, 1, 3))
    q2 = jnp.transpose(dq[..., 1, :], (0, 2, 1, 3))
    k1 = jnp.transpose(dk[..., 0, :], (0, 2, 1, 3))
    k2 = jnp.transpose(dk[..., 1, :], (0, 2, 1, 3))
    dv = jnp.transpose(dv.reshape(B, S, DIFF_HEADS, 2 * DIFF_HEAD_DIM), (0, 2, 1, 3))
    f32 = jnp.float32
    lam = (jnp.exp(jnp.sum(lq1.astype(f32) * lk1.astype(f32))) - jnp.exp(jnp.sum(lq2.astype(f32) * lk2.astype(f32)))
           + lambda_init)
    o = diff_attention(q1, q2, k1, k2, dv, lam, rel_bias)
    o = rms_norm(o, diff_subln) * (1.0 - lambda_init)
    y_diff = jnp.transpose(o, (0, 2, 1, 3)).reshape(B, S, DIFF_WIDTH)
    nq = rms_norm(nq.reshape(B, S, NA_HEADS, NA_HEAD_DIM), na_qn) * (NA_HEAD_DIM ** -0.5)
    nk = rms_norm(nk.reshape(B, S, NA_HEADS, NA_HEAD_DIM), na_kn)
    nv = nv.reshape(B, S, NA_HEADS, NA_HEAD_DIM)
    y_na = neighborhood_attention(nq, nk, nv, na_rpb).reshape(B, S, NA_WIDTH)
    g_diff, g_na = jnp.split(jax.nn.sigmoid(gates), 2, axis=-1)
    merged = g_diff * (y_diff @ w_br_diff) + g_na * (y_na @ w_br_na)
    return x + merged @ w_out


def expert_choice_ffn(h, w_router, w_gate, w_up, w_down):
    B, S, D = h.shape
    n = B * S
    cap = CAPACITY_FACTOR * n // N_EXPERTS
    t = h.reshape(n, D)
    aff = jax.nn.softmax((t @ w_router).astype(jnp.float32), axis=-1)
    gate, idx = lax.top_k(aff.T, cap)

    def expert(args):
        wg, wu, wd, ix, g = args
        xe = t[ix]
        he = jax.nn.silu(xe @ wg) * (xe @ wu)
        return (he @ wd) * g[:, None].astype(he.dtype)

    ye = lax.map(expert, (w_gate, w_up, w_down, idx, gate))
    y = jnp.zeros((n, D), h.dtype).at[idx.reshape(-1)].add(ye.reshape(-1, D))
    return y.reshape(B, S, D)


def setup_inputs(seed: int = 0) -> dict:
    key = jax.random.key(seed)
    ks = jax.random.split(key, 24)
    f32 = jnp.float32

    def nrm(k, shape, scale):
        return scale * jax.random.normal(k, shape, f32)

    def gain(k, shape):
        return 1.0 + 0.02 * jax.random.normal(k, shape, f32)

    return {
        "x_prompt": nrm(ks[0], (BATCH, SEQ, D_MODEL), 1.0),
        "x_sample": nrm(ks[1], (DEC_BATCH, DEC_SEQ, D_MODEL), 1.0),
        "norm1_g": gain(ks[2], (DEPTH, D_MODEL)),
        "w_in": nrm(ks[3], (DEPTH, D_MODEL, IN_WIDTH), D_MODEL ** -0.5),
        "diff_q_norm": gain(ks[4], (DEPTH, DIFF_HEAD_DIM)),
        "diff_k_norm": gain(ks[5], (DEPTH, DIFF_HEAD_DIM)),
        "lambda_q1": nrm(ks[6], (DEPTH, DIFF_HEAD_DIM), 0.1),
        "lambda_k1": nrm(ks[7], (DEPTH, DIFF_HEAD_DIM), 0.1),
        "lambda_q2": nrm(ks[8], (DEPTH, DIFF_HEAD_DIM), 0.1),
        "lambda_k2": nrm(ks[9], (DEPTH, DIFF_HEAD_DIM), 0.1),
        "diff_subln_g": gain(ks[10], (DEPTH, 2 * DIFF_HEAD_DIM)),
        "rel_bias": nrm(ks[11], (T5_BUCKETS, DIFF_HEADS), 0.2),
        "na_q_norm": gain(ks[12], (DEPTH, NA_HEAD_DIM)),
        "na_k_norm": gain(ks[13], (DEPTH, NA_HEAD_DIM)),
        "na_rpb": nrm(ks[14], (DEPTH, NA_HEADS, 2 * NA_WIN_ROWS - 1, 2 * NA_WIN_COLS - 1), 0.2),
        "w_branch_diff": nrm(ks[15], (DEPTH, DIFF_WIDTH, D_MODEL), DIFF_WIDTH ** -0.5),
        "w_branch_na": nrm(ks[16], (DEPTH, NA_WIDTH, D_MODEL), NA_WIDTH ** -0.5),
        "w_out": nrm(ks[17], (DEPTH, D_MODEL, D_MODEL), D_MODEL ** -0.5),
        "norm2_g": gain(ks[18], (DEPTH, D_MODEL)),
        "w_router": nrm(ks[19], (DEPTH, D_MODEL, N_EXPERTS), D_MODEL ** -0.5),
        "w_expert_gate": nrm(ks[20], (DEPTH, N_EXPERTS, D_MODEL, EXPERT_FF), D_MODEL ** -0.5),
        "w_expert_up": nrm(ks[21], (DEPTH, N_EXPERTS, D_MODEL, EXPERT_FF), D_MODEL ** -0.5),
        "w_expert_down": nrm(ks[22], (DEPTH, N_EXPERTS, EXPERT_FF, D_MODEL), EXPERT_FF ** -0.5),
    }


def reference(x_prompt, x_sample, norm1_g, w_in, diff_q_norm, diff_k_norm, lambda_q1, lambda_k1, lambda_q2,
              lambda_k2, diff_subln_g, rel_bias, na_q_norm, na_k_norm, na_rpb, w_branch_diff, w_branch_na,
              w_out, norm2_g, w_router, w_expert_gate, w_expert_up, w_expert_down):
    def run(x):
        for l in range(DEPTH):
            lambda_init = 0.8 - 0.6 * math.exp(-0.3 * l)
            x = token_mixer(x, lambda_init, norm1_g[l], w_in[l], diff_q_norm[l], diff_k_norm[l],
                            lambda_q1[l], lambda_k1[l], lambda_q2[l], lambda_k2[l], diff_subln_g[l], rel_bias,
                            na_q_norm[l], na_k_norm[l], na_rpb[l], w_branch_diff[l], w_branch_na[l], w_out[l])
            x = x + expert_choice_ffn(rms_norm(x, norm2_g[l]), w_router[l], w_expert_gate[l],
                                      w_expert_up[l], w_expert_down[l])
        return x

    y_prompt = run(x_prompt)
    y_sample = run(x_sample)
    return (y_prompt, y_sample)
```

```python
import functools
import math

import jax
import jax.numpy as jnp
from jax import lax
from jax.experimental import pallas as pl
from jax.experimental.pallas import tpu as pltpu

F32 = jnp.float32
BF16 = jnp.bfloat16

DIFF_HEADS = 4
DIFF_HEAD_DIM = 64
DIFF_WIDTH = DIFF_HEADS * 2 * DIFF_HEAD_DIM
NA_HEADS = 8
NA_HEAD_DIM = 64
NA_WIDTH = NA_HEADS * NA_HEAD_DIM
GRID_W = 64
NA_WIN_ROWS = 8
NA_WIN_COLS = 16
T5_BUCKETS = 32
T5_MAX_DIST = 128
CAPACITY_FACTOR = 2
EPS = 1e-6

LANES = 128
MXU_DIM = 256
VMEM_LIMIT_BYTES = 56 * 1024 * 1024
NEG_BIG = -1e30


def _cparams(*sem):
    return pltpu.CompilerParams(dimension_semantics=sem, vmem_limit_bytes=VMEM_LIMIT_BYTES)


def _layer_block(a, layer):
    return pl.BlockSpec((1,) + a.shape[1:], lambda *_: (layer,) + (0,) * (a.ndim - 1))


def _nt_dot(a, b):
    return lax.dot_general(a, b, (((1,), (1,)), ((), ())), preferred_element_type=F32)


def _inproj_kernel(x_ref, g_ref, w_ref, seg_ref, dqg_ref, dkg_ref, nqg_ref, nkg_ref,
                   dq_ref, dk_ref, dv_ref, nq_ref, nk_ref, nv_ref, gate_ref, *, layer, gate_width):
    x = x_ref[...]
    ms = jnp.mean(x * x, axis=-1, keepdims=True)
    h = (x * lax.rsqrt(ms + EPS) * g_ref[layer]).astype(BF16)
    width = DIFF_WIDTH

    def proj(c0):
        return jnp.dot(h, w_ref[0, :, c0:c0 + width], preferred_element_type=F32)

    def head_norm(a, gain):
        sq = (a * a).astype(BF16)
        halves = [jnp.dot(sq[:, c:c + MXU_DIM], seg_ref[...], preferred_element_type=F32)
                  for c in range(0, width, MXU_DIM)]
        ms_seg = jnp.concatenate(halves, axis=1)
        return a * lax.rsqrt(ms_seg + EPS) * gain

    dq_ref[...] = head_norm(proj(0), dqg_ref[layer]).astype(BF16)
    dk_ref[...] = head_norm(proj(width), dkg_ref[layer]).astype(BF16)
    dv_ref[...] = proj(2 * width).astype(BF16)
    nq_ref[...] = head_norm(proj(3 * width), nqg_ref[layer]).astype(BF16)
    nk_ref[...] = head_norm(proj(4 * width), nkg_ref[layer]).astype(BF16)
    nv_ref[...] = proj(5 * width).astype(BF16)
    for c in range(0, gate_width, width):
        a = proj(6 * width + c)
        gate_ref[:, c:c + width] = (1.0 / (1.0 + jnp.exp(-a))).astype(BF16)


def _in_proj(x2d, layer, norm_g, w_in, seg, dqg, dkg, nqg, nkg, tm):
    n, d = x2d.shape
    in_width = w_in.shape[-1]
    gate_width = in_width - 6 * DIFF_WIDTH
    whole = lambda a: pl.BlockSpec(a.shape, lambda i: (0,) * a.ndim)
    row = lambda w: pl.BlockSpec((tm, w), lambda i: (i, 0))
    outs = [jax.ShapeDtypeStruct((n, DIFF_WIDTH), BF16)] * 6 + [jax.ShapeDtypeStruct((n, gate_width), BF16)]
    return pl.pallas_call(
        functools.partial(_inproj_kernel, layer=layer, gate_width=gate_width),
        grid=(n // tm,),
        in_specs=[row(d), whole(norm_g), _layer_block(w_in, layer), whole(seg), whole(dqg), whole(dkg),
                  whole(nqg), whole(nkg)],
        out_specs=[row(DIFF_WIDTH)] * 6 + [row(gate_width)],
        out_shape=outs,
        compiler_params=_cparams("parallel"),
    )(x2d, norm_g, w_in, seg, dqg, dkg, nqg, nkg)


def _diff_attn_kernel(q_ref, k_ref, v_ref, bias_ref, lam_ref, g_ref, o_ref,
                      m1_ref, l1_ref, a1_ref, m2_ref, l2_ref, a2_ref, *, layer, lambda_init):
    ki = pl.program_id(3)

    @pl.when(ki == 0)
    def _():
        for m_ref, l_ref, a_ref in ((m1_ref, l1_ref, a1_ref), (m2_ref, l2_ref, a2_ref)):
            m_ref[...] = jnp.full(m_ref.shape, NEG_BIG, F32)
            l_ref[...] = jnp.zeros(l_ref.shape, F32)
            a_ref[...] = jnp.zeros(a_ref.shape, F32)

    q = q_ref[0]
    k = k_ref[0]
    v = v_ref[0]
    bias = bias_ref[0, 0]
    lane = lax.broadcasted_iota(jnp.int32, q.shape, 1)
    zero = jnp.zeros_like(q)

    def update(qm, m_ref, l_ref, a_ref):
        s = _nt_dot(qm, k) + bias
        m_prev = m_ref[...]
        m_new = jnp.maximum(m_prev, jnp.max(s, axis=-1, keepdims=True))
        p = jnp.exp(s - m_new)
        alpha = jnp.exp(m_prev - m_new)
        l_ref[...] = alpha * l_ref[...] + jnp.sum(p, axis=-1, keepdims=True)
        a_ref[...] = alpha * a_ref[...] + jnp.dot(p.astype(BF16), v, preferred_element_type=F32)
        m_ref[...] = m_new

    update(jnp.where(lane < DIFF_HEAD_DIM, q, zero), m1_ref, l1_ref, a1_ref)
    update(jnp.where(lane >= DIFF_HEAD_DIM, q, zero), m2_ref, l2_ref, a2_ref)

    @pl.when(ki == pl.num_programs(3) - 1)
    def _():
        lp = lam_ref[layer]
        lam = (jnp.exp(jnp.sum(lp[0:1] * lp[1:2], axis=-1, keepdims=True))
               - jnp.exp(jnp.sum(lp[2:3] * lp[3:4], axis=-1, keepdims=True)) + lambda_init)
        o = a1_ref[...] / l1_ref[...] - lam * (a2_ref[...] / l2_ref[...])
        ms = jnp.mean(o * o, axis=-1, keepdims=True)
        o = o * lax.rsqrt(ms + EPS) * g_ref[layer] * (1.0 - lambda_init)
        o_ref[0] = o.astype(BF16)


def _diff_attention(dq, dk, dv, bias_tiles, lam_params, subln_g, layer, lambda_init, t):
    b, s, _ = dq.shape
    nb = s // t
    hw = 2 * DIFF_HEAD_DIM
    qspec = pl.BlockSpec((1, t, hw), lambda bi, h, qi, ki: (bi, qi, h))
    kspec = pl.BlockSpec((1, t, hw), lambda bi, h, qi, ki: (bi, ki, h))
    bspec = pl.BlockSpec((1, 1, t, t), lambda bi, h, qi, ki: (h, jnp.clip(ki - qi, -2, 2) + 2, 0, 0))
    whole = lambda a: pl.BlockSpec(a.shape, lambda bi, h, qi, ki: (0,) * a.ndim)
    return pl.pallas_call(
        functools.partial(_diff_attn_kernel, layer=layer, lambda_init=lambda_init),
        grid=(b, DIFF_HEADS, nb, nb),
        in_specs=[qspec, kspec, kspec, bspec, whole(lam_params), whole(subln_g)],
        out_specs=qspec,
        out_shape=jax.ShapeDtypeStruct((b, s, DIFF_WIDTH), BF16),
        scratch_shapes=[pltpu.VMEM((t, 1), F32), pltpu.VMEM((t, 1), F32), pltpu.VMEM((t, hw), F32),
                        pltpu.VMEM((t, 1), F32), pltpu.VMEM((t, 1), F32), pltpu.VMEM((t, hw), F32)],
        compiler_params=_cparams("parallel", "parallel", "parallel", "arbitrary"),
    )(dq, dk, dv, bias_tiles, lam_params, subln_g)


def _t5_bucket(rel):
    nb = T5_BUCKETS // 2
    max_exact = nb // 2
    side = jnp.where(rel > 0, nb, 0)
    n = jnp.abs(rel)
    nf = jnp.maximum(n, 1).astype(F32)
    large = max_exact + (jnp.log(nf / max_exact) / math.log(T5_MAX_DIST / max_exact) * (nb - max_exact)).astype(jnp.int32)
    large = jnp.minimum(large, nb - 1)
    return side + jnp.where(n < max_exact, n, large)


def _t5_bias_tiles(rel_bias, t):
    assert t + 1 >= T5_MAX_DIST
    i = jnp.arange(t, dtype=jnp.int32)[:, None]
    j = jnp.arange(t, dtype=jnp.int32)[None, :]
    d = jnp.arange(-2, 3, dtype=jnp.int32)[:, None, None]
    rel = d * t + (j - i)[None]
    tiles = rel_bias.astype(F32)[_t5_bucket(rel)]
    return jnp.transpose(tiles, (3, 0, 1, 2))


def _na_kernel(q_ref, k_ref, v_ref, bias_ref, o_ref, *, rows):
    r = pl.program_id(1)
    rs = jnp.clip(r - NA_WIN_ROWS // 2, 0, rows - NA_WIN_ROWS)
    nkeys = NA_WIN_ROWS * GRID_W
    q = q_ref[0, 0]
    kwin = k_ref[0, pl.ds(rs, NA_WIN_ROWS)].reshape(nkeys, NA_WIDTH)
    vwin = v_ref[0, pl.ds(rs, NA_WIN_ROWS)].reshape(nkeys, NA_WIDTH)
    qlane = lax.broadcasted_iota(jnp.int32, (GRID_W, LANES), 1)
    vlane = lax.broadcasted_iota(jnp.int32, (nkeys, LANES), 1)
    for hp in range(NA_HEADS // 2):
        c0 = hp * LANES
        qp = q[:, c0:c0 + LANES]
        kp = kwin[:, c0:c0 + LANES]
        vp = vwin[:, c0:c0 + LANES]
        o_pair = jnp.zeros((GRID_W, LANES), F32)
        for hh in range(2):
            qsel = (qlane < NA_HEAD_DIM) if hh == 0 else (qlane >= NA_HEAD_DIM)
            vsel = (vlane < NA_HEAD_DIM) if hh == 0 else (vlane >= NA_HEAD_DIM)
            s = _nt_dot(jnp.where(qsel, qp, jnp.zeros_like(qp)), kp) + bias_ref[0, 2 * hp + hh]
            m = jnp.max(s, axis=-1, keepdims=True)
            p = jnp.exp(s - m)
            l = jnp.sum(p, axis=-1, keepdims=True)
            pv = jnp.dot(p.astype(BF16), jnp.where(vsel, vp, jnp.zeros_like(vp)), preferred_element_type=F32)
            o_pair = o_pair + pv / l
        o_ref[0, 0, :, c0:c0 + LANES] = o_pair.astype(BF16)


def _na_attention(nq, nk, nv, bias_full):
    b, s, _ = nq.shape
    rows = s // GRID_W
    assert rows >= NA_WIN_ROWS
    shp = (b, rows, GRID_W, NA_WIDTH)
    qspec = pl.BlockSpec((1, 1, GRID_W, NA_WIDTH), lambda bi, r: (bi, r, 0, 0))
    kspec = pl.BlockSpec((1, rows, GRID_W, NA_WIDTH), lambda bi, r: (bi, 0, 0, 0))

    def bias_map(bi, r):
        rs = jnp.clip(r - NA_WIN_ROWS // 2, 0, rows - NA_WIN_ROWS)
        return (r - rs, 0, 0, 0)

    bspec = pl.BlockSpec((1, NA_HEADS, GRID_W, NA_WIN_ROWS * GRID_W), bias_map)
    out = pl.pallas_call(
        functools.partial(_na_kernel, rows=rows),
        grid=(b, rows),
        in_specs=[qspec, kspec, kspec, bspec],
        out_specs=qspec,
        out_shape=jax.ShapeDtypeStruct(shp, BF16),
        compiler_params=_cparams("parallel", "arbitrary"),
    )(nq.reshape(shp), nk.reshape(shp), nv.reshape(shp), bias_full)
    return out.reshape(b, s, NA_WIDTH)


def _na_bias(rpb):
    case = jnp.arange(NA_WIN_ROWS)[:, None]
    w = jnp.arange(NA_WIN_ROWS)[None, :]
    dr_idx = w - case + (NA_WIN_ROWS - 1)
    c = jnp.arange(GRID_W)[:, None]
    kc = jnp.arange(GRID_W)[None, :]
    cs = jnp.clip(c - NA_WIN_COLS // 2, 0, GRID_W - NA_WIN_COLS)
    valid = (kc >= cs) & (kc < cs + NA_WIN_COLS)
    dc_idx = jnp.clip(kc - c + NA_WIN_COLS - 1, 0, 2 * NA_WIN_COLS - 2)
    tab = rpb.astype(F32)[:, dr_idx]
    tab = tab[:, :, :, dc_idx]
    tab = jnp.where(valid[None, None, None], tab, NEG_BIG)
    tab = jnp.transpose(tab, (1, 0, 3, 2, 4))
    return tab.reshape(NA_WIN_ROWS, NA_HEADS, GRID_W, NA_WIN_ROWS * GRID_W)


def _merge_kernel(x_ref, yd_ref, yn_ref, gate_ref, wbd_ref, wbn_ref, wo_ref, g2_ref, wrh_ref, wrl_ref,
                  x1_ref, h2_ref, aff_ref, *, layer):
    d = x_ref.shape[-1]
    bd = jnp.dot(yd_ref[...], wbd_ref[0], preferred_element_type=F32)
    bn = jnp.dot(yn_ref[...], wbn_ref[0], preferred_element_type=F32)
    merged = gate_ref[:, :d].astype(F32) * bd + gate_ref[:, d:].astype(F32) * bn
    x1 = x_ref[...] + jnp.dot(merged.astype(BF16), wo_ref[0], preferred_element_type=F32)
    x1_ref[...] = x1
    ms = jnp.mean(x1 * x1, axis=-1, keepdims=True)
    t = x1 * lax.rsqrt(ms + EPS) * g2_ref[layer]
    t_hi = t.astype(BF16)
    h2_ref[...] = t_hi
    t_lo = (t - t_hi.astype(F32)).astype(BF16)
    wrh = wrh_ref[layer]
    logits = _nt_dot(wrh, t_hi) + _nt_dot(wrh, t_lo) + _nt_dot(wrl_ref[layer], t_hi)
    mx = jnp.max(logits, axis=0, keepdims=True)
    e = jnp.exp(logits - mx)
    aff_ref[...] = e / jnp.sum(e, axis=0, keepdims=True)


def _merge_out(x2d, yd, yn, gates, wbd, wbn, wo, g2, wrh, wrl, layer, tm):
    n, d = x2d.shape
    ne = wrh.shape[1]
    whole = lambda a: pl.BlockSpec(a.shape, lambda i: (0,) * a.ndim)
    row = lambda w: pl.BlockSpec((tm, w), lambda i: (i, 0))
    return pl.pallas_call(
        functools.partial(_merge_kernel, layer=layer),
        grid=(n // tm,),
        in_specs=[row(d), row(DIFF_WIDTH), row(NA_WIDTH), row(2 * d), _layer_block(wbd, layer),
                  _layer_block(wbn, layer), _layer_block(wo, layer), whole(g2), whole(wrh), whole(wrl)],
        out_specs=[row(d), row(d), pl.BlockSpec((ne, tm), lambda i: (0, i))],
        out_shape=[jax.ShapeDtypeStruct((n, d), F32), jax.ShapeDtypeStruct((n, d), BF16),
                   jax.ShapeDtypeStruct((ne, n), F32)],
        compiler_params=_cparams("parallel"),
    )(x2d, yd, yn, gates, wbd, wbn, wo, g2, wrh, wrl)


def _expert_kernel(xe_ref, gate_ref, wg_ref, wu_ref, wd_ref, o_ref, acc_ref):
    f = pl.program_id(2)
    x = xe_ref[0]
    g = jnp.dot(x, wg_ref[0, 0], preferred_element_type=F32)
    u = jnp.dot(x, wu_ref[0, 0], preferred_element_type=F32)
    hmid = (g / (1.0 + jnp.exp(-g)) * u).astype(BF16)
    part = jnp.dot(hmid, wd_ref[0, 0], preferred_element_type=F32)

    @pl.when(f == 0)
    def _():
        acc_ref[...] = part

    @pl.when(f > 0)
    def _():
        acc_ref[...] += part

    @pl.when(f == pl.num_programs(2) - 1)
    def _():
        o_ref[0] = acc_ref[...] * gate_ref[0]


def _expert_ffn(xe, gate, wg, wu, wd, layer, tm, tf):
    ne, cap, d = xe.shape
    ff = wg.shape[-1]
    return pl.pallas_call(
        _expert_kernel,
        grid=(ne, cap // tm, ff // tf),
        in_specs=[pl.BlockSpec((1, tm, d), lambda e, m, f: (e, m, 0)),
                  pl.BlockSpec((1, tm, 1), lambda e, m, f: (e, m, 0)),
                  pl.BlockSpec((1, 1, d, tf), lambda e, m, f: (layer, e, 0, f)),
                  pl.BlockSpec((1, 1, d, tf), lambda e, m, f: (layer, e, 0, f)),
                  pl.BlockSpec((1, 1, tf, d), lambda e, m, f: (layer, e, f, 0))],
        out_specs=pl.BlockSpec((1, tm, d), lambda e, m, f: (e, m, 0)),
        out_shape=jax.ShapeDtypeStruct((ne, cap, d), F32),
        scratch_shapes=[pltpu.VMEM((tm, d), F32)],
        compiler_params=_cparams("parallel", "parallel", "arbitrary"),
    )(xe, gate, wg, wu, wd)


def _pick_tile(n, pref):
    t = min(n, pref)
    assert n % t == 0
    return t


def _ff_tile(ff):
    best = ff
    for t in range(LANES, ff // 2 + 1, LANES):
        if ff % t == 0:
            best = t
    return best


def kernel(x_prompt, x_sample, norm1_g, w_in, diff_q_norm, diff_k_norm, lambda_q1, lambda_k1, lambda_q2,
           lambda_k2, diff_subln_g, rel_bias, na_q_norm, na_k_norm, na_rpb, w_branch_diff, w_branch_na,
           w_out, norm2_g, w_router, w_expert_gate, w_expert_up, w_expert_down):
    depth, d_model, _ = w_in.shape
    n_experts = w_router.shape[-1]
    ff = w_expert_gate.shape[-1]

    w_in_b = w_in.astype(BF16)
    wbd_b = w_branch_diff.astype(BF16)
    wbn_b = w_branch_na.astype(BF16)
    wo_b = w_out.astype(BF16)
    wg_b = w_expert_gate.astype(BF16)
    wu_b = w_expert_up.astype(BF16)
    wd_b = w_expert_down.astype(BF16)
    wr_t = jnp.swapaxes(w_router.astype(F32), 1, 2)
    wr_hi = wr_t.astype(BF16)
    wr_lo = (wr_t - wr_hi.astype(F32)).astype(BF16)
    g1 = norm1_g.astype(F32)[:, None, :]
    g2 = norm2_g.astype(F32)[:, None, :]
    tile_gain = lambda g, reps, scale: (jnp.tile(g.astype(F32), (1, reps)) * scale)[:, None, :]
    dqg = tile_gain(diff_q_norm, 2 * DIFF_HEADS, DIFF_HEAD_DIM ** -0.5)
    dkg = tile_gain(diff_k_norm, 2 * DIFF_HEADS, 1.0)
    nqg = tile_gain(na_q_norm, NA_HEADS, NA_HEAD_DIM ** -0.5)
    nkg = tile_gain(na_k_norm, NA_HEADS, 1.0)
    subln = diff_subln_g.astype(F32)[:, None, :]
    lam_params = jnp.stack([lambda_q1, lambda_k1, lambda_q2, lambda_k2], axis=1).astype(F32)
    seg_id = jnp.arange(MXU_DIM) // DIFF_HEAD_DIM
    seg = jnp.where(seg_id[:, None] == seg_id[None, :], 1.0 / DIFF_HEAD_DIM, 0.0).astype(BF16)
    na_bias = [_na_bias(na_rpb[l]) for l in range(depth)]

    def run(x):
        b, s, _ = x.shape
        n = b * s
        cap = CAPACITY_FACTOR * n // n_experts
        tm = _pick_tile(n, 512)
        t_attn = _pick_tile(s, 512)
        t5_tiles = _t5_bias_tiles(rel_bias, t_attn)
        x2d = x.reshape(n, d_model)
        for l in range(depth):
            lambda_init = 0.8 - 0.6 * math.exp(-0.3 * l)
            dq, dk, dv, nq, nk, nv, gates = _in_proj(x2d, l, g1, w_in_b, seg, dqg, dkg, nqg, nkg, tm)
            r3 = lambda a: a.reshape(b, s, a.shape[-1])
            yd = _diff_attention(r3(dq), r3(dk), r3(dv), t5_tiles, lam_params, subln, l, lambda_init, t_attn)
            yn = _na_attention(r3(nq), r3(nk), r3(nv), na_bias[l])
            x1, h2, aff_t = _merge_out(x2d, yd.reshape(n, DIFF_WIDTH), yn.reshape(n, NA_WIDTH), gates,
                                       wbd_b, wbn_b, wo_b, g2, wr_hi, wr_lo, l, tm)
            gate, idx = lax.top_k(aff_t, cap)
            xe = jnp.take(h2, idx.reshape(-1), axis=0).reshape(n_experts, cap, d_model)
            ye = _expert_ffn(xe, gate[..., None], wg_b, wu_b, wd_b, l, _pick_tile(cap, 512), _ff_tile(ff))
            x2d = x1.at[idx.reshape(-1)].add(ye.reshape(-1, d_model))
        return x2d.reshape(b, s, d_model)

    return (run(x_prompt), run(x_sample))
```

```python
import functools
import math

import jax
import jax.numpy as jnp
from jax import lax
from jax.experimental import pallas as pl
from jax.experimental.pallas import tpu as pltpu

F32 = jnp.float32
BF16 = jnp.bfloat16

DIFF_HEADS = 4
DIFF_HEAD_DIM = 64
DIFF_WIDTH = DIFF_HEADS * 2 * DIFF_HEAD_DIM
NA_HEADS = 8
NA_HEAD_DIM = 64
NA_WIDTH = NA_HEADS * NA_HEAD_DIM
GRID_W = 64
NA_WIN_ROWS = 8
NA_WIN_COLS = 16
T5_BUCKETS = 32
T5_MAX_DIST = 128
CAPACITY_FACTOR = 2
EPS = 1e-6

LANES = 128
MXU_DIM = 256
VMEM_LIMIT_BYTES = 56 * 1024 * 1024
NEG_BIG = -1e30
LOG2E = math.log2(math.e)
NA_ROWS_PER_STEP = 4


def _cparams(*sem):
    return pltpu.CompilerParams(dimension_semantics=sem, vmem_limit_bytes=VMEM_LIMIT_BYTES)


def _layer_block(a, layer):
    return pl.BlockSpec((1,) + a.shape[1:], lambda *_: (layer,) + (0,) * (a.ndim - 1))


def _nt_dot(a, b):
    return lax.dot_general(a, b, (((1,), (1,)), ((), ())), preferred_element_type=F32)


def _inproj_kernel(x_ref, g_ref, w_ref, seg_ref, dqg_ref, dkg_ref, nqg_ref, nkg_ref,
                   dq_ref, dk_ref, dv_ref, nq_ref, nk_ref, nv_ref, gate_ref, *, layer, gate_width):
    x = x_ref[...]
    ms = jnp.mean(x * x, axis=-1, keepdims=True)
    h = (x * lax.rsqrt(ms + EPS) * g_ref[layer]).astype(BF16)
    width = DIFF_WIDTH

    def proj(c0):
        return jnp.dot(h, w_ref[0, :, c0:c0 + width], preferred_element_type=F32)

    def head_norm(a, gain):
        sq = (a * a).astype(BF16)
        halves = [jnp.dot(sq[:, c:c + MXU_DIM], seg_ref[...], preferred_element_type=F32)
                  for c in range(0, width, MXU_DIM)]
        ms_seg = jnp.concatenate(halves, axis=1)
        return a * lax.rsqrt(ms_seg + EPS) * gain

    dq_ref[...] = head_norm(proj(0), dqg_ref[layer]).astype(BF16)
    dk_ref[...] = head_norm(proj(width), dkg_ref[layer]).astype(BF16)
    dv_ref[...] = proj(2 * width).astype(BF16)
    nq_ref[...] = head_norm(proj(3 * width), nqg_ref[layer]).astype(BF16)
    nk_ref[...] = head_norm(proj(4 * width), nkg_ref[layer]).astype(BF16)
    nv_ref[...] = proj(5 * width).astype(BF16)
    for c in range(0, gate_width, width):
        a = proj(6 * width + c)
        gate_ref[:, c:c + width] = (1.0 / (1.0 + jnp.exp(-a))).astype(BF16)


def _in_proj(x2d, layer, norm_g, w_in, seg, dqg, dkg, nqg, nkg, tm):
    n, d = x2d.shape
    in_width = w_in.shape[-1]
    gate_width = in_width - 6 * DIFF_WIDTH
    whole = lambda a: pl.BlockSpec(a.shape, lambda i: (0,) * a.ndim)
    row = lambda w: pl.BlockSpec((tm, w), lambda i: (i, 0))
    outs = [jax.ShapeDtypeStruct((n, DIFF_WIDTH), BF16)] * 6 + [jax.ShapeDtypeStruct((n, gate_width), BF16)]
    return pl.pallas_call(
        functools.partial(_inproj_kernel, layer=layer, gate_width=gate_width),
        grid=(n // tm,),
        in_specs=[row(d), whole(norm_g), _layer_block(w_in, layer), whole(seg), whole(dqg), whole(dkg),
                  whole(nqg), whole(nkg)],
        out_specs=[row(DIFF_WIDTH)] * 6 + [row(gate_width)],
        out_shape=outs,
        compiler_params=_cparams("parallel"),
    )(x2d, norm_g, w_in, seg, dqg, dkg, nqg, nkg)


def _diff_attn_kernel(q_ref, k_ref, v_ref, bias_ref, lam_ref, g_ref, o_ref,
                      m1_ref, a1_ref, m2_ref, a2_ref, *, layer, lambda_init):
    qi = pl.program_id(2)
    hw = 2 * DIFF_HEAD_DIM
    tq = q_ref.shape[1]
    seq = k_ref.shape[1]
    sub = bias_ref.shape[-1]

    q = q_ref[0]
    lane = lax.broadcasted_iota(jnp.int32, q.shape, 1)
    zero = jnp.zeros_like(q)
    q1 = jnp.where(lane < DIFF_HEAD_DIM, q, zero)
    q2 = jnp.where(lane >= DIFF_HEAD_DIM, q, zero)

    def update(qm, k, v_ones, bias, m_ref, a_ref, first):
        s = _nt_dot(qm, k) + bias
        m_cur = jnp.max(s, axis=-1, keepdims=True)
        if first:
            m_new = jnp.broadcast_to(m_cur, (tq, LANES))
        else:
            m_prev = m_ref[...]
            m_new = jnp.maximum(m_prev, m_cur)
        p = jnp.exp2(s - pltpu.repeat(m_new, sub // LANES, 1))
        pv = jnp.dot(p.astype(BF16), v_ones, preferred_element_type=F32)
        if first:
            a_ref[...] = pv
        else:
            alpha = jnp.exp2(m_prev - m_new)
            a_ref[...] = pltpu.repeat(alpha, 2, 1) * a_ref[...] + pv
        m_ref[...] = m_new

    d_lo, _ = _t5_tile_range(tq, sub)
    n_tiles = bias_ref.shape[1]
    for c in range(seq // sub):
        k = k_ref[0, c * sub:(c + 1) * sub]
        v = v_ref[0, c * sub:(c + 1) * sub]
        v_ones = jnp.concatenate([v, jnp.ones_like(v)], axis=1)
        bias = bias_ref[0, jnp.clip(c - qi * (tq // sub) - d_lo, 0, n_tiles - 1)]
        update(q1, k, v_ones, bias, m1_ref, a1_ref, c == 0)
        update(q2, k, v_ones, bias, m2_ref, a2_ref, c == 0)

    lp = lam_ref[layer]
    lam = (jnp.exp(jnp.sum(lp[0:1] * lp[1:2], axis=-1, keepdims=True))
           - jnp.exp(jnp.sum(lp[2:3] * lp[3:4], axis=-1, keepdims=True)) + lambda_init)
    o = a1_ref[:, :hw] / a1_ref[:, hw:] - lam * (a2_ref[:, :hw] / a2_ref[:, hw:])
    ms = jnp.mean(o * o, axis=-1, keepdims=True)
    o = o * lax.rsqrt(ms + EPS) * g_ref[layer] * (1.0 - lambda_init)
    o_ref[0] = o.astype(BF16)


def _diff_attention(dq, dk, dv, bias_tiles, lam_params, subln_g, layer, lambda_init, t):
    b, s, _ = dq.shape
    hw = 2 * DIFF_HEAD_DIM
    qspec = pl.BlockSpec((1, t, hw), lambda bi, h, qi: (bi, qi, h))
    kspec = pl.BlockSpec((1, s, hw), lambda bi, h, qi: (bi, 0, h))
    bspec = pl.BlockSpec((1,) + bias_tiles.shape[1:], lambda bi, h, qi: (h, 0, 0, 0))
    whole = lambda a: pl.BlockSpec(a.shape, lambda bi, h, qi: (0,) * a.ndim)
    return pl.pallas_call(
        functools.partial(_diff_attn_kernel, layer=layer, lambda_init=lambda_init),
        grid=(b, DIFF_HEADS, s // t),
        in_specs=[qspec, kspec, kspec, bspec, whole(lam_params), whole(subln_g)],
        out_specs=qspec,
        out_shape=jax.ShapeDtypeStruct((b, s, DIFF_WIDTH), BF16),
        scratch_shapes=[pltpu.VMEM((t, LANES), F32), pltpu.VMEM((t, 2 * hw), F32),
                        pltpu.VMEM((t, LANES), F32), pltpu.VMEM((t, 2 * hw), F32)],
        compiler_params=_cparams("parallel", "parallel", "arbitrary"),
    )(dq, dk, dv, bias_tiles, lam_params, subln_g)


def _t5_tile_range(tq, sub):
    d_lo = (-(T5_MAX_DIST - 1) - sub) // sub
    d_hi = -(-(tq + T5_MAX_DIST - 1) // sub)
    return d_lo, d_hi


def _t5_bucket(rel):
    nb = T5_BUCKETS // 2
    max_exact = nb // 2
    side = jnp.where(rel > 0, nb, 0)
    n = jnp.abs(rel)
    nf = jnp.maximum(n, 1).astype(F32)
    large = max_exact + (jnp.log(nf / max_exact) / math.log(T5_MAX_DIST / max_exact) * (nb - max_exact)).astype(jnp.int32)
    large = jnp.minimum(large, nb - 1)
    return side + jnp.where(n < max_exact, n, large)


def _t5_bias_tiles(rel_bias, tq, sub):
    d_lo, d_hi = _t5_tile_range(tq, sub)
    i = jnp.arange(tq, dtype=jnp.int32)[:, None]
    j = jnp.arange(sub, dtype=jnp.int32)[None, :]
    d = jnp.arange(d_lo, d_hi + 1, dtype=jnp.int32)[:, None, None]
    rel = d * sub + (j - i)[None]
    tiles = rel_bias.astype(F32)[_t5_bucket(rel)]
    return jnp.transpose(tiles, (3, 0, 1, 2)) * LOG2E


def _na_kernel(q_ref, k_ref, v_ref, bias_ref, o_ref, *, rows, rows_per_step):
    nkeys = NA_WIN_ROWS * GRID_W
    lane = lax.broadcasted_iota(jnp.int32, (GRID_W, LANES), 1)
    first_half = lane < NA_HEAD_DIM
    for j in range(rows_per_step):
        r = pl.program_id(1) * rows_per_step + j
        rs = jnp.clip(r - NA_WIN_ROWS // 2, 0, rows - NA_WIN_ROWS)
        case = r - rs
        q = q_ref[0, j]
        kwin = k_ref[0, pl.ds(rs, NA_WIN_ROWS)].reshape(nkeys, NA_WIDTH)
        vwin = v_ref[0, pl.ds(rs, NA_WIN_ROWS)].reshape(nkeys, NA_WIDTH)
        for hp in range(NA_HEADS // 2):
            c0 = hp * LANES
            qp = q[:, c0:c0 + LANES]
            zero = jnp.zeros_like(qp)
            q_stack = jnp.concatenate([jnp.where(first_half, qp, zero), jnp.where(first_half, zero, qp)], axis=0)
            vp = vwin[:, c0:c0 + LANES]
            s = _nt_dot(q_stack, kwin[:, c0:c0 + LANES]) + bias_ref[case, hp]
            m = jnp.max(s, axis=-1, keepdims=True)
            p = jnp.exp2(s - m)
            res = jnp.dot(p.astype(BF16), jnp.concatenate([vp, jnp.ones_like(vp)], axis=1),
                          preferred_element_type=F32)
            o = res[:, :LANES] / res[:, LANES:]
            o_ref[0, j, :, c0:c0 + LANES] = jnp.where(first_half, o[:GRID_W], o[GRID_W:]).astype(BF16)


def _na_attention(nq, nk, nv, bias_full, rows_per_step):
    b, s, _ = nq.shape
    rows = s // GRID_W
    assert rows >= NA_WIN_ROWS and rows % rows_per_step == 0
    shp = (b, rows, GRID_W, NA_WIDTH)
    qspec = pl.BlockSpec((1, rows_per_step, GRID_W, NA_WIDTH), lambda bi, r: (bi, r, 0, 0))
    kspec = pl.BlockSpec((1, rows, GRID_W, NA_WIDTH), lambda bi, r: (bi, 0, 0, 0))
    bspec = pl.BlockSpec(bias_full.shape, lambda bi, r: (0, 0, 0, 0))
    out = pl.pallas_call(
        functools.partial(_na_kernel, rows=rows, rows_per_step=rows_per_step),
        grid=(b, rows // rows_per_step),
        in_specs=[qspec, kspec, kspec, bspec],
        out_specs=qspec,
        out_shape=jax.ShapeDtypeStruct(shp, BF16),
        compiler_params=_cparams("parallel", "arbitrary"),
    )(nq.reshape(shp), nk.reshape(shp), nv.reshape(shp), bias_full)
    return out.reshape(b, s, NA_WIDTH)


def _na_bias(rpb):
    case = jnp.arange(NA_WIN_ROWS)[:, None]
    w = jnp.arange(NA_WIN_ROWS)[None, :]
    dr_idx = w - case + (NA_WIN_ROWS - 1)
    c = jnp.arange(GRID_W)[:, None]
    kc = jnp.arange(GRID_W)[None, :]
    cs = jnp.clip(c - NA_WIN_COLS // 2, 0, GRID_W - NA_WIN_COLS)
    valid = (kc >= cs) & (kc < cs + NA_WIN_COLS)
    dc_idx = jnp.clip(kc - c + NA_WIN_COLS - 1, 0, 2 * NA_WIN_COLS - 2)
    tab = rpb.astype(F32)[:, dr_idx]
    tab = tab[:, :, :, dc_idx]
    tab = jnp.where(valid[None, None, None], tab * LOG2E, NEG_BIG)
    tab = jnp.transpose(tab, (1, 0, 3, 2, 4))
    return tab.reshape(NA_WIN_ROWS, NA_HEADS // 2, 2 * GRID_W, NA_WIN_ROWS * GRID_W)


def _merge_kernel(x_ref, yd_ref, yn_ref, gate_ref, wbd_ref, wbn_ref, wo_ref, g2_ref, wrh_ref, wrl_ref,
                  x1_ref, h2_ref, aff_ref, *, layer):
    d = x_ref.shape[-1]
    bd = jnp.dot(yd_ref[...], wbd_ref[0], preferred_element_type=F32)
    bn = jnp.dot(yn_ref[...], wbn_ref[0], preferred_element_type=F32)
    merged = gate_ref[:, :d].astype(F32) * bd + gate_ref[:, d:].astype(F32) * bn
    x1 = x_ref[...] + jnp.dot(merged.astype(BF16), wo_ref[0], preferred_element_type=F32)
    x1_ref[...] = x1
    ms = jnp.mean(x1 * x1, axis=-1, keepdims=True)
    t = x1 * lax.rsqrt(ms + EPS) * g2_ref[layer]
    t_hi = t.astype(BF16)
    h2_ref[...] = t_hi
    t_lo = (t - t_hi.astype(F32)).astype(BF16)
    wrh = wrh_ref[layer]
    logits = _nt_dot(wrh, t_hi) + _nt_dot(wrh, t_lo) + _nt_dot(wrl_ref[layer], t_hi)
    mx = jnp.max(logits, axis=0, keepdims=True)
    e = jnp.exp(logits - mx)
    aff_ref[...] = e / jnp.sum(e, axis=0, keepdims=True)


def _merge_out(x2d, yd, yn, gates, wbd, wbn, wo, g2, wrh, wrl, layer, tm):
    n, d = x2d.shape
    ne = wrh.shape[1]
    whole = lambda a: pl.BlockSpec(a.shape, lambda i: (0,) * a.ndim)
    row = lambda w: pl.BlockSpec((tm, w), lambda i: (i, 0))
    return pl.pallas_call(
        functools.partial(_merge_kernel, layer=layer),
        grid=(n // tm,),
        in_specs=[row(d), row(DIFF_WIDTH), row(NA_WIDTH), row(2 * d), _layer_block(wbd, layer),
                  _layer_block(wbn, layer), _layer_block(wo, layer), whole(g2), whole(wrh), whole(wrl)],
        out_specs=[row(d), row(d), pl.BlockSpec((ne, tm), lambda i: (0, i))],
        out_shape=[jax.ShapeDtypeStruct((n, d), F32), jax.ShapeDtypeStruct((n, d), BF16),
                   jax.ShapeDtypeStruct((ne, n), F32)],
        compiler_params=_cparams("parallel"),
    )(x2d, yd, yn, gates, wbd, wbn, wo, g2, wrh, wrl)


def _expert_kernel(xe_ref, gate_ref, wg_ref, wu_ref, wd_ref, o_ref, acc_ref):
    f = pl.program_id(2)
    x = xe_ref[0]
    g = jnp.dot(x, wg_ref[0, 0], preferred_element_type=F32)
    u = jnp.dot(x, wu_ref[0, 0], preferred_element_type=F32)
    hmid = (g / (1.0 + jnp.exp(-g)) * u).astype(BF16)
    part = jnp.dot(hmid, wd_ref[0, 0], preferred_element_type=F32)

    @pl.when(f == 0)
    def _():
        acc_ref[...] = part

    @pl.when(f > 0)
    def _():
        acc_ref[...] += part

    @pl.when(f == pl.num_programs(2) - 1)
    def _():
        o_ref[0] = acc_ref[...] * gate_ref[0]


def _expert_ffn(xe, gate, wg, wu, wd, layer, tm, tf):
    ne, cap, d = xe.shape
    ff = wg.shape[-1]
    return pl.pallas_call(
        _expert_kernel,
        grid=(ne, cap // tm, ff // tf),
        in_specs=[pl.BlockSpec((1, tm, d), lambda e, m, f: (e, m, 0)),
                  pl.BlockSpec((1, tm, 1), lambda e, m, f: (e, m, 0)),
                  pl.BlockSpec((1, 1, d, tf), lambda e, m, f: (layer, e, 0, f)),
                  pl.BlockSpec((1, 1, d, tf), lambda e, m, f: (layer, e, 0, f)),
                  pl.BlockSpec((1, 1, tf, d), lambda e, m, f: (layer, e, f, 0))],
        out_specs=pl.BlockSpec((1, tm, d), lambda e, m, f: (e, m, 0)),
        out_shape=jax.ShapeDtypeStruct((ne, cap, d), F32),
        scratch_shapes=[pltpu.VMEM((tm, d), F32)],
        compiler_params=_cparams("parallel", "parallel", "arbitrary"),
    )(xe, gate, wg, wu, wd)


def _pick_tile(n, pref):
    t = min(n, pref)
    assert n % t == 0
    return t


def _ff_tile(ff):
    best = ff
    for t in range(LANES, ff // 2 + 1, LANES):
        if ff % t == 0:
            best = t
    return best


def kernel(x_prompt, x_sample, norm1_g, w_in, diff_q_norm, diff_k_norm, lambda_q1, lambda_k1, lambda_q2,
           lambda_k2, diff_subln_g, rel_bias, na_q_norm, na_k_norm, na_rpb, w_branch_diff, w_branch_na,
           w_out, norm2_g, w_router, w_expert_gate, w_expert_up, w_expert_down):
    depth, d_model, _ = w_in.shape
    n_experts = w_router.shape[-1]
    ff = w_expert_gate.shape[-1]

    w_in_b = w_in.astype(BF16)
    wbd_b = w_branch_diff.astype(BF16)
    wbn_b = w_branch_na.astype(BF16)
    wo_b = w_out.astype(BF16)
    wg_b = w_expert_gate.astype(BF16)
    wu_b = w_expert_up.astype(BF16)
    wd_b = w_expert_down.astype(BF16)
    wr_t = jnp.swapaxes(w_router.astype(F32), 1, 2)
    wr_hi = wr_t.astype(BF16)
    wr_lo = (wr_t - wr_hi.astype(F32)).astype(BF16)
    g1 = norm1_g.astype(F32)[:, None, :]
    g2 = norm2_g.astype(F32)[:, None, :]
    tile_gain = lambda g, reps, scale: (jnp.tile(g.astype(F32), (1, reps)) * scale)[:, None, :]
    dqg = tile_gain(diff_q_norm, 2 * DIFF_HEADS, DIFF_HEAD_DIM ** -0.5 * LOG2E)
    dkg = tile_gain(diff_k_norm, 2 * DIFF_HEADS, 1.0)
    nqg = tile_gain(na_q_norm, NA_HEADS, NA_HEAD_DIM ** -0.5 * LOG2E)
    nkg = tile_gain(na_k_norm, NA_HEADS, 1.0)
    subln = diff_subln_g.astype(F32)[:, None, :]
    lam_params = jnp.stack([lambda_q1, lambda_k1, lambda_q2, lambda_k2], axis=1).astype(F32)
    seg_id = jnp.arange(MXU_DIM) // DIFF_HEAD_DIM
    seg = jnp.where(seg_id[:, None] == seg_id[None, :], 1.0 / DIFF_HEAD_DIM, 0.0).astype(BF16)
    na_bias = [_na_bias(na_rpb[l]) for l in range(depth)]

    def run(x):
        b, s, _ = x.shape
        n = b * s
        cap = CAPACITY_FACTOR * n // n_experts
        tm = _pick_tile(n, 512)
        t_attn = _pick_tile(s, 512)
        t5_tiles = _t5_bias_tiles(rel_bias, t_attn, min(s, MXU_DIM))
        x2d = x.reshape(n, d_model)
        for l in range(depth):
            lambda_init = 0.8 - 0.6 * math.exp(-0.3 * l)
            dq, dk, dv, nq, nk, nv, gates = _in_proj(x2d, l, g1, w_in_b, seg, dqg, dkg, nqg, nkg, tm)
            r3 = lambda a: a.reshape(b, s, a.shape[-1])
            yd = _diff_attention(r3(dq), r3(dk), r3(dv), t5_tiles, lam_params, subln, l, lambda_init, t_attn)
            yn = _na_attention(r3(nq), r3(nk), r3(nv), na_bias[l], NA_ROWS_PER_STEP)
            x1, h2, aff_t = _merge_out(x2d, yd.reshape(n, DIFF_WIDTH), yn.reshape(n, NA_WIDTH), gates,
                                       wbd_b, wbn_b, wo_b, g2, wr_hi, wr_lo, l, tm)
            gate, idx = lax.top_k(aff_t, cap)
            xe = jnp.take(h2, idx.reshape(-1), axis=0).reshape(n_experts, cap, d_model)
            ye = _expert_ffn(xe, gate[..., None], wg_b, wu_b, wd_b, l, _pick_tile(cap, 512), _ff_tile(ff))
            x2d = x1.at[idx.reshape(-1)].add(ye.reshape(-1, d_model))
        return x2d.reshape(b, s, d_model)

    return (run(x_prompt), run(x_sample))
```

```python
import functools
import math

import numpy as np
import jax
import jax.numpy as jnp
from jax import lax
from jax.experimental import pallas as pl
from jax.experimental.pallas import tpu as pltpu

F32 = jnp.float32
BF16 = jnp.bfloat16

DIFF_HEADS = 4
DIFF_HEAD_DIM = 64
DIFF_WIDTH = DIFF_HEADS * 2 * DIFF_HEAD_DIM
NA_HEADS = 8
NA_HEAD_DIM = 64
NA_WIDTH = NA_HEADS * NA_HEAD_DIM
GRID_W = 64
NA_WIN_ROWS = 8
NA_WIN_COLS = 16
T5_BUCKETS = 32
T5_MAX_DIST = 128
CAPACITY_FACTOR = 2
EPS = 1e-6

LANES = 128
MXU_DIM = 256
VMEM_LIMIT_BYTES = 56 * 1024 * 1024
NEG_BIG = -1e30
LOG2E = math.log2(math.e)
NA_ROWS_PER_STEP = 4
BF16_ROWS = 16
ROUTE_TILE = 256
ROUTE_CHUNK = 64
COMBINE_CHUNK = ROUTE_CHUNK + BF16_ROWS
F32_ROWS = 8
DISPATCH_WIN = ROUTE_CHUNK + F32_ROWS


def _cparams(*sem):
    return pltpu.CompilerParams(dimension_semantics=sem, vmem_limit_bytes=VMEM_LIMIT_BYTES)


def _layer_block(a, layer):
    return pl.BlockSpec((1,) + a.shape[1:], lambda *_: (layer,) + (0,) * (a.ndim - 1))


def _nt_dot(a, b):
    return lax.dot_general(a, b, (((1,), (1,)), ((), ())), preferred_element_type=F32)


def _inproj_kernel(x_ref, g_ref, w_ref, seg_ref, dqg_ref, dkg_ref, nqg_ref, nkg_ref,
                   dq_ref, dk_ref, dv_ref, nq_ref, nk_ref, nv_ref, gate_ref, *, layer, gate_width):
    x = x_ref[...]
    ms = jnp.mean(x * x, axis=-1, keepdims=True)
    h = (x * lax.rsqrt(ms + EPS) * g_ref[layer]).astype(BF16)
    width = DIFF_WIDTH

    def proj(c0):
        return jnp.dot(h, w_ref[0, :, c0:c0 + width], preferred_element_type=F32)

    def head_norm(a, gain):
        sq = (a * a).astype(BF16)
        halves = [jnp.dot(sq[:, c:c + MXU_DIM], seg_ref[...], preferred_element_type=F32)
                  for c in range(0, width, MXU_DIM)]
        ms_seg = jnp.concatenate(halves, axis=1)
        return a * lax.rsqrt(ms_seg + EPS) * gain

    dq_ref[...] = head_norm(proj(0), dqg_ref[layer]).astype(BF16)
    dk_ref[...] = head_norm(proj(width), dkg_ref[layer]).astype(BF16)
    dv_ref[...] = proj(2 * width).astype(BF16)
    nq_ref[...] = head_norm(proj(3 * width), nqg_ref[layer]).astype(BF16)
    nk_ref[...] = head_norm(proj(4 * width), nkg_ref[layer]).astype(BF16)
    nv_ref[...] = proj(5 * width).astype(BF16)
    for c in range(0, gate_width, width):
        a = proj(6 * width + c)
        gate_ref[:, c:c + width] = (1.0 / (1.0 + jnp.exp(-a))).astype(BF16)


def _in_proj(x2d, layer, norm_g, w_in, seg, dqg, dkg, nqg, nkg, tm):
    n, d = x2d.shape
    in_width = w_in.shape[-1]
    gate_width = in_width - 6 * DIFF_WIDTH
    whole = lambda a: pl.BlockSpec(a.shape, lambda i: (0,) * a.ndim)
    row = lambda w: pl.BlockSpec((tm, w), lambda i: (i, 0))
    outs = [jax.ShapeDtypeStruct((n, DIFF_WIDTH), BF16)] * 6 + [jax.ShapeDtypeStruct((n, gate_width), BF16)]
    return pl.pallas_call(
        functools.partial(_inproj_kernel, layer=layer, gate_width=gate_width),
        grid=(n // tm,),
        in_specs=[row(d), whole(norm_g), _layer_block(w_in, layer), whole(seg), whole(dqg), whole(dkg),
                  whole(nqg), whole(nkg)],
        out_specs=[row(DIFF_WIDTH)] * 6 + [row(gate_width)],
        out_shape=outs,
        compiler_params=_cparams("parallel"),
    )(x2d, norm_g, w_in, seg, dqg, dkg, nqg, nkg)


def _diff_attn_kernel(q_ref, k_ref, v_ref, bias_ref, lam_ref, g_ref, o_ref,
                      m1_ref, a1_ref, m2_ref, a2_ref, *, layer, lambda_init):
    qi = pl.program_id(2)
    hw = 2 * DIFF_HEAD_DIM
    tq = q_ref.shape[1]
    seq = k_ref.shape[1]
    sub = bias_ref.shape[-1]

    q = q_ref[0]
    lane = lax.broadcasted_iota(jnp.int32, q.shape, 1)
    zero = jnp.zeros_like(q)
    q1 = jnp.where(lane < DIFF_HEAD_DIM, q, zero)
    q2 = jnp.where(lane >= DIFF_HEAD_DIM, q, zero)

    def update(qm, k, v_ones, bias, m_ref, a_ref, first):
        s = _nt_dot(qm, k) + bias
        m_cur = jnp.max(s, axis=-1, keepdims=True)
        if first:
            m_new = jnp.broadcast_to(m_cur, (tq, LANES))
        else:
            m_prev = m_ref[...]
            m_new = jnp.maximum(m_prev, m_cur)
        p = jnp.exp2(s - jnp.concatenate([m_new] * (sub // LANES), axis=1))
        pv = jnp.dot(p.astype(BF16), v_ones, preferred_element_type=F32)
        if first:
            a_ref[...] = pv
        else:
            alpha = jnp.exp2(m_prev - m_new)
            a_ref[...] = jnp.concatenate([alpha, alpha], axis=1) * a_ref[...] + pv
        m_ref[...] = m_new

    d_lo, _ = _t5_tile_range(tq, sub)
    n_tiles = bias_ref.shape[1]
    for c in range(seq // sub):
        k = k_ref[0, c * sub:(c + 1) * sub]
        v = v_ref[0, c * sub:(c + 1) * sub]
        v_ones = jnp.concatenate([v, jnp.ones_like(v)], axis=1)
        bias = bias_ref[0, jnp.clip(c - qi * (tq // sub) - d_lo, 0, n_tiles - 1)]
        update(q1, k, v_ones, bias, m1_ref, a1_ref, c == 0)
        update(q2, k, v_ones, bias, m2_ref, a2_ref, c == 0)

    lp = lam_ref[layer]
    lam = (jnp.exp(jnp.sum(lp[0:1] * lp[1:2], axis=-1, keepdims=True))
           - jnp.exp(jnp.sum(lp[2:3] * lp[3:4], axis=-1, keepdims=True)) + lambda_init)
    o = a1_ref[:, :hw] / a1_ref[:, hw:] - lam * (a2_ref[:, :hw] / a2_ref[:, hw:])
    ms = jnp.mean(o * o, axis=-1, keepdims=True)
    o = o * lax.rsqrt(ms + EPS) * g_ref[layer] * (1.0 - lambda_init)
    o_ref[0] = o.astype(BF16)


def _diff_attention(dq, dk, dv, bias_tiles, lam_params, subln_g, layer, lambda_init, t):
    b, s, _ = dq.shape
    hw = 2 * DIFF_HEAD_DIM
    qspec = pl.BlockSpec((1, t, hw), lambda bi, h, qi: (bi, qi, h))
    kspec = pl.BlockSpec((1, s, hw), lambda bi, h, qi: (bi, 0, h))
    bspec = pl.BlockSpec((1,) + bias_tiles.shape[1:], lambda bi, h, qi: (h, 0, 0, 0))
    whole = lambda a: pl.BlockSpec(a.shape, lambda bi, h, qi: (0,) * a.ndim)
    return pl.pallas_call(
        functools.partial(_diff_attn_kernel, layer=layer, lambda_init=lambda_init),
        grid=(b, DIFF_HEADS, s // t),
        in_specs=[qspec, kspec, kspec, bspec, whole(lam_params), whole(subln_g)],
        out_specs=qspec,
        out_shape=jax.ShapeDtypeStruct((b, s, DIFF_WIDTH), BF16),
        scratch_shapes=[pltpu.VMEM((t, LANES), F32), pltpu.VMEM((t, 2 * hw), F32),
                        pltpu.VMEM((t, LANES), F32), pltpu.VMEM((t, 2 * hw), F32)],
        compiler_params=_cparams("parallel", "parallel", "arbitrary"),
    )(dq, dk, dv, bias_tiles, lam_params, subln_g)


def _t5_tile_range(tq, sub):
    d_lo = (-(T5_MAX_DIST - 1) - sub) // sub
    d_hi = -(-(tq + T5_MAX_DIST - 1) // sub)
    return d_lo, d_hi


def _t5_bucket(rel):
    nb = T5_BUCKETS // 2
    max_exact = nb // 2
    side = jnp.where(rel > 0, nb, 0)
    n = jnp.abs(rel)
    nf = jnp.maximum(n, 1).astype(F32)
    large = max_exact + (jnp.log(nf / max_exact) / math.log(T5_MAX_DIST / max_exact) * (nb - max_exact)).astype(jnp.int32)
    large = jnp.minimum(large, nb - 1)
    return side + jnp.where(n < max_exact, n, large)


def _t5_bias_tiles(rel_bias, tq, sub):
    d_lo, d_hi = _t5_tile_range(tq, sub)
    i = jnp.arange(tq, dtype=jnp.int32)[:, None]
    j = jnp.arange(sub, dtype=jnp.int32)[None, :]
    d = jnp.arange(d_lo, d_hi + 1, dtype=jnp.int32)[:, None, None]
    rel = d * sub + (j - i)[None]
    tiles = rel_bias.astype(F32)[_t5_bucket(rel)]
    return jnp.transpose(tiles, (3, 0, 1, 2)) * LOG2E


def _na_kernel(q_ref, k_ref, v_ref, bias_ref, o_ref, *, rows, rows_per_step):
    nkeys = NA_WIN_ROWS * GRID_W
    lane = lax.broadcasted_iota(jnp.int32, (GRID_W, LANES), 1)
    first_half = lane < NA_HEAD_DIM
    for j in range(rows_per_step):
        r = pl.program_id(1) * rows_per_step + j
        rs = jnp.clip(r - NA_WIN_ROWS // 2, 0, rows - NA_WIN_ROWS)
        case = r - rs
        q = q_ref[0, j]
        kwin = k_ref[0, pl.ds(rs, NA_WIN_ROWS)].reshape(nkeys, NA_WIDTH)
        vwin = v_ref[0, pl.ds(rs, NA_WIN_ROWS)].reshape(nkeys, NA_WIDTH)
        for hp in range(NA_HEADS // 2):
            c0 = hp * LANES
            qp = q[:, c0:c0 + LANES]
            zero = jnp.zeros_like(qp)
            q_stack = jnp.concatenate([jnp.where(first_half, qp, zero), jnp.where(first_half, zero, qp)], axis=0)
            vp = vwin[:, c0:c0 + LANES]
            s = _nt_dot(q_stack, kwin[:, c0:c0 + LANES]) + bias_ref[case, hp]
            m = jnp.max(s, axis=-1, keepdims=True)
            p = jnp.exp2(s - m)
            res = jnp.dot(p.astype(BF16), jnp.concatenate([vp, jnp.ones_like(vp)], axis=1),
                          preferred_element_type=F32)
            o = res[:, :LANES] / res[:, LANES:]
            o_ref[0, j, :, c0:c0 + LANES] = jnp.where(first_half, o[:GRID_W], o[GRID_W:]).astype(BF16)


def _na_attention(nq, nk, nv, bias_full, rows_per_step):
    b, s, _ = nq.shape
    rows = s // GRID_W
    assert rows >= NA_WIN_ROWS and rows % rows_per_step == 0
    shp = (b, rows, GRID_W, NA_WIDTH)
    qspec = pl.BlockSpec((1, rows_per_step, GRID_W, NA_WIDTH), lambda bi, r: (bi, r, 0, 0))
    kspec = pl.BlockSpec((1, rows, GRID_W, NA_WIDTH), lambda bi, r: (bi, 0, 0, 0))
    bspec = pl.BlockSpec(bias_full.shape, lambda bi, r: (0, 0, 0, 0))
    out = pl.pallas_call(
        functools.partial(_na_kernel, rows=rows, rows_per_step=rows_per_step),
        grid=(b, rows // rows_per_step),
        in_specs=[qspec, kspec, kspec, bspec],
        out_specs=qspec,
        out_shape=jax.ShapeDtypeStruct(shp, BF16),
        compiler_params=_cparams("parallel", "arbitrary"),
    )(nq.reshape(shp), nk.reshape(shp), nv.reshape(shp), bias_full)
    return out.reshape(b, s, NA_WIDTH)


def _na_bias(rpb):
    case = jnp.arange(NA_WIN_ROWS)[:, None]
    w = jnp.arange(NA_WIN_ROWS)[None, :]
    dr_idx = w - case + (NA_WIN_ROWS - 1)
    c = jnp.arange(GRID_W)[:, None]
    kc = jnp.arange(GRID_W)[None, :]
    cs = jnp.clip(c - NA_WIN_COLS // 2, 0, GRID_W - NA_WIN_COLS)
    valid = (kc >= cs) & (kc < cs + NA_WIN_COLS)
    dc_idx = jnp.clip(kc - c + NA_WIN_COLS - 1, 0, 2 * NA_WIN_COLS - 2)
    tab = rpb.astype(F32)[:, dr_idx]
    tab = tab[:, :, :, dc_idx]
    tab = jnp.where(valid[None, None, None], tab * LOG2E, NEG_BIG)
    tab = jnp.transpose(tab, (1, 0, 3, 2, 4))
    return tab.reshape(NA_WIN_ROWS, NA_HEADS // 2, 2 * GRID_W, NA_WIN_ROWS * GRID_W)


def _merge_kernel(x_ref, yd_ref, yn_ref, gate_ref, wbd_ref, wbn_ref, wo_ref, g2_ref, wrh_ref, wrl_ref,
                  wch_ref, wcl_ref, x1_ref, h2_ref, aff_ref, affr_ref, *, layer):
    d = x_ref.shape[-1]
    ne = aff_ref.shape[0]
    bd = jnp.dot(yd_ref[...], wbd_ref[0], preferred_element_type=F32)
    bn = jnp.dot(yn_ref[...], wbn_ref[0], preferred_element_type=F32)
    merged = gate_ref[:, :d].astype(F32) * bd + gate_ref[:, d:].astype(F32) * bn
    x1 = x_ref[...] + jnp.dot(merged.astype(BF16), wo_ref[0], preferred_element_type=F32)
    x1_ref[...] = x1
    ms = jnp.mean(x1 * x1, axis=-1, keepdims=True)
    t = x1 * lax.rsqrt(ms + EPS) * g2_ref[layer]
    t_hi = t.astype(BF16)
    h2_ref[...] = t_hi
    t_lo = (t - t_hi.astype(F32)).astype(BF16)
    wrh = wrh_ref[layer]
    logits = _nt_dot(wrh, t_hi) + _nt_dot(wrh, t_lo) + _nt_dot(wrl_ref[layer], t_hi)
    mx = jnp.max(logits, axis=0, keepdims=True)
    e = jnp.exp(logits - mx)
    aff_ref[...] = e / jnp.sum(e, axis=0, keepdims=True)
    wch = wch_ref[0]
    lg = (jnp.dot(t_hi, wch, preferred_element_type=F32) + jnp.dot(t_lo, wch, preferred_element_type=F32)
          + jnp.dot(t_hi, wcl_ref[0], preferred_element_type=F32))
    lg = jnp.where(lax.broadcasted_iota(jnp.int32, lg.shape, 1) < ne, lg, NEG_BIG)
    er = jnp.exp(lg - jnp.max(lg, axis=-1, keepdims=True))
    affr_ref[...] = er / jnp.sum(er, axis=-1, keepdims=True)


def _merge_out(x2d, yd, yn, gates, wbd, wbn, wo, g2, wrh, wrl, wch, wcl, layer, tm):
    n, d = x2d.shape
    ne = wrh.shape[1]
    whole = lambda a: pl.BlockSpec(a.shape, lambda i: (0,) * a.ndim)
    row = lambda w: pl.BlockSpec((tm, w), lambda i: (i, 0))
    return pl.pallas_call(
        functools.partial(_merge_kernel, layer=layer),
        grid=(n // tm,),
        in_specs=[row(d), row(DIFF_WIDTH), row(NA_WIDTH), row(2 * d), _layer_block(wbd, layer),
                  _layer_block(wbn, layer), _layer_block(wo, layer), whole(g2), whole(wrh), whole(wrl),
                  _layer_block(wch, layer), _layer_block(wcl, layer)],
        out_specs=[row(d), row(d), pl.BlockSpec((ne, tm), lambda i: (0, i)), row(LANES)],
        out_shape=[jax.ShapeDtypeStruct((n, d), F32), jax.ShapeDtypeStruct((n, d), BF16),
                   jax.ShapeDtypeStruct((ne, n), F32), jax.ShapeDtypeStruct((n, LANES), F32)],
        compiler_params=_cparams("parallel"),
    )(x2d, yd, yn, gates, wbd, wbn, wo, g2, wrh, wrl, wch, wcl)


def _select_kernel(aff_ref, pos_ref, tb_ref, *, cap):
    ne, nt, _ = aff_ref.shape
    bits = lambda e: pltpu.bitcast(aff_ref[e], jnp.int32)

    def bit_step(i, thr):
        cand = thr | lax.shift_left(jnp.int32(1), 30 - i)
        rows = []
        for e in range(ne):
            ce = cand[e:e + 1]
            ge = (bits(e) >= ce).astype(jnp.int32)
            cnt = jnp.sum(jnp.sum(ge, axis=0, keepdims=True), axis=1, keepdims=True)
            rows.append(jnp.where(cnt >= cap, ce, thr[e:e + 1]))
        return jnp.concatenate(rows, axis=0)

    thr = lax.fori_loop(0, 31, bit_step, jnp.zeros((ne, LANES), jnp.int32))

    li = lax.broadcasted_iota(jnp.int32, (LANES, LANES), 0)
    lj = lax.broadcasted_iota(jnp.int32, (LANES, LANES), 1)
    upper = (li < lj).astype(BF16)
    ones = jnp.ones((LANES, LANES), BF16)
    ti = lax.broadcasted_iota(jnp.int32, (nt, nt), 0)
    tj = lax.broadcasted_iota(jnp.int32, (nt, nt), 1)
    lower = (tj < ti).astype(BF16)

    def prefix(flags):
        fb = flags.astype(BF16)
        within = jnp.dot(fb, upper, preferred_element_type=F32)
        per_tile = jnp.dot(fb, ones, preferred_element_type=F32)
        before = jnp.dot(lower, per_tile.astype(BF16), preferred_element_type=F32)
        return before + within, before

    for e in range(ne):
        be = bits(e)
        te = thr[e:e + 1]
        gt = be > te
        eq = be == te
        n_gt = jnp.sum(jnp.sum(gt.astype(F32), axis=0, keepdims=True), axis=1, keepdims=True)
        eq_rank, _ = prefix(eq)
        sel = gt | (eq & (eq_rank < cap - n_gt))
        pos, before = prefix(sel)
        pos_ref[e] = jnp.where(sel, pos, -1.0).astype(jnp.int32)
        tb_ref[e] = before


def _select(aff_t, cap):
    ne, n = aff_t.shape
    nt = n // LANES
    shp = (ne, nt, LANES)
    spec = pl.BlockSpec(shp, lambda i: (0, 0, 0))
    pos, tb = pl.pallas_call(
        functools.partial(_select_kernel, cap=cap),
        grid=(1,),
        in_specs=[spec],
        out_specs=[spec, spec],
        out_shape=[jax.ShapeDtypeStruct(shp, jnp.int32), jax.ShapeDtypeStruct(shp, F32)],
        compiler_params=_cparams("arbitrary"),
    )(aff_t.reshape(shp))
    return pos.reshape(ne, n), tb[:, :, 0].astype(jnp.int32)


def _dispatch_kernel(tb_ref, nr_ref, h_ref, pos_ref, affr_ref, xe_hbm, gc_hbm, xs_ref, gs_ref, last_ref, sem,
                     *, cap):
    j = pl.program_id(0)
    ne = pos_ref.shape[0]
    tt = h_ref.shape[0]

    @pl.when(j == 0)
    def _():
        xs_ref[...] = jnp.zeros(xs_ref.shape, F32)
        gs_ref[...] = jnp.zeros(gs_ref.shape, F32)
        for e in range(ne):
            last_ref[e] = 0
        pad = [(pltpu.make_async_copy(xs_ref.at[e], xe_hbm.at[e, pl.ds(cap + q * DISPATCH_WIN, DISPATCH_WIN)], sem.at[0]),
                pltpu.make_async_copy(gs_ref.at[e], gc_hbm.at[e, pl.ds(cap + q * DISPATCH_WIN, DISPATCH_WIN)], sem.at[1]))
               for e in range(ne) for q in range((xe_hbm.shape[1] - cap) // DISPATCH_WIN)]
        for cx, cg in pad:
            cx.start()
            cg.start()
        for cx, cg in pad:
            cx.wait()
            cg.wait()

    h = h_ref[...]
    g = affr_ref[...]
    g1 = g.astype(BF16)
    g2 = (g - g1.astype(F32)).astype(BF16)
    g3 = (g - g1.astype(F32) - g2.astype(F32)).astype(BF16)
    slot = lax.broadcasted_iota(jnp.int32, (DISPATCH_WIN, tt), 0)
    row = lax.broadcasted_iota(jnp.int32, (F32_ROWS, 1), 0)

    def copies(e, start):
        dst = pl.ds(pl.multiple_of(start, F32_ROWS), DISPATCH_WIN)
        return (pltpu.make_async_copy(xs_ref.at[e], xe_hbm.at[e, dst], sem.at[0]),
                pltpu.make_async_copy(gs_ref.at[e], gc_hbm.at[e, dst], sem.at[1]))

    def one_round(r, carry):
        first = [tb_ref[e, j] for e in range(ne)]
        starts = [lax.shift_left(lax.shift_right_logical(first[e], 3), 3) + r * ROUTE_CHUNK for e in range(ne)]
        live = [(r == 0) | (tb_ref[e, j + 1] > first[e] + r * ROUTE_CHUNK) for e in range(ne)]
        onehot = jnp.concatenate(
            [(pos_ref[e:e + 1, :] - starts[e] == slot).astype(BF16) for e in range(ne)], axis=0)
        xs = jnp.dot(onehot, h, preferred_element_type=F32)
        gs = (jnp.dot(onehot, g1, preferred_element_type=F32) + jnp.dot(onehot, g2, preferred_element_type=F32)
              + jnp.dot(onehot, g3, preferred_element_type=F32))
        for e in range(ne):
            @pl.when(live[e])
            def _():
                n_carry = jnp.where(r == 0, first[e] - starts[e], 0)
                off = pl.multiple_of(jnp.where(r == 0, starts[e] - last_ref[e], 0), F32_ROWS)
                keep = row < n_carry
                lo = e * DISPATCH_WIN
                x_head = xs[lo:lo + F32_ROWS] + jnp.where(keep, xs_ref[e, pl.ds(off, F32_ROWS)], 0.0)
                g_head = gs[lo:lo + F32_ROWS] + jnp.where(keep, gs_ref[e, pl.ds(off, F32_ROWS)], 0.0)
                xs_ref[e, F32_ROWS:] = xs[lo + F32_ROWS:lo + DISPATCH_WIN]
                gs_ref[e, F32_ROWS:] = gs[lo + F32_ROWS:lo + DISPATCH_WIN]
                xs_ref[e, :F32_ROWS] = x_head
                gs_ref[e, :F32_ROWS] = g_head
                last_ref[e] = starts[e]
                for c in copies(e, starts[e]):
                    c.start()
        for e in range(ne):
            @pl.when(live[e])
            def _():
                for c in copies(e, starts[e]):
                    c.wait()
        return carry

    lax.fori_loop(0, nr_ref[j], one_round, 0)


def _dispatch(h2, pos, affr, tb_tiles, n_rounds, cap):
    n, d = h2.shape
    ne = pos.shape[0]
    tt = ROUTE_TILE
    assert cap % F32_ROWS == 0
    cap_pad = cap + (tt // ROUTE_CHUNK) * DISPATCH_WIN
    grid_spec = pltpu.PrefetchScalarGridSpec(
        num_scalar_prefetch=2,
        grid=(n // tt,),
        in_specs=[pl.BlockSpec((tt, d), lambda j, *_: (j, 0)),
                  pl.BlockSpec((ne, tt), lambda j, *_: (0, j)),
                  pl.BlockSpec((tt, LANES), lambda j, *_: (j, 0))],
        out_specs=[pl.BlockSpec(memory_space=pl.ANY), pl.BlockSpec(memory_space=pl.ANY)],
        scratch_shapes=[pltpu.VMEM((ne, DISPATCH_WIN, d), F32), pltpu.VMEM((ne, DISPATCH_WIN, LANES), F32),
                        pltpu.SMEM((ne,), jnp.int32), pltpu.SemaphoreType.DMA((2,))],
    )
    return pl.pallas_call(
        functools.partial(_dispatch_kernel, cap=cap),
        grid_spec=grid_spec,
        out_shape=[jax.ShapeDtypeStruct((ne, cap_pad, d), F32), jax.ShapeDtypeStruct((ne, cap_pad, LANES), F32)],
        compiler_params=_cparams("arbitrary"),
    )(tb_tiles, n_rounds, h2, pos, affr)


def _combine_kernel(tb_ref, nr_ref, x_ref, post_ref, tbt_ref, spread_ref, kcol_ref, ye_hbm, o_ref, ch_ref, sem,
                    *, cap):
    j = pl.program_id(0)
    ne = ye_hbm.shape[0]
    o_ref[...] = x_ref[...]
    spread = spread_ref[...]
    kcol = kcol_ref[...]
    last = cap - COMBINE_CHUNK

    def one_round(r, carry):
        def start_of(e):
            s = tb_ref[e, j] + r * ROUTE_CHUNK
            return jnp.minimum(lax.shift_left(lax.shift_right_logical(s, 4), 4), last)
        cps = [pltpu.make_async_copy(ye_hbm.at[e, pl.ds(pl.multiple_of(start_of(e), BF16_ROWS), COMBINE_CHUNK)],
                                     ch_ref.at[e], sem.at[0]) for e in range(ne)]
        for c in cps:
            c.start()
        first = tbt_ref[pl.ds(j, 1), :] + (r * ROUTE_CHUNK).astype(F32)
        sv = jnp.minimum(jnp.floor(first * (1.0 / BF16_ROWS)) * BF16_ROWS, float(last))
        post = post_ref[...]
        mine = (post >= first) & (post < first + ROUTE_CHUNK)
        rel = jnp.where(mine, post - sv, -1.0).astype(BF16)
        rel_wide = jnp.dot(rel, spread, preferred_element_type=F32)
        onehot = (rel_wide == kcol).astype(BF16)
        for c in cps:
            c.wait()
        rows = ch_ref[...].reshape(ne * COMBINE_CHUNK, ch_ref.shape[-1])
        o_ref[...] += jnp.dot(onehot, rows, preferred_element_type=F32)
        return carry

    lax.fori_loop(0, nr_ref[j], one_round, 0)


def _combine(x1, pos_t, tb_tiles, tb_t, n_rounds, ye, cap):
    n, d = x1.shape
    ne = ye.shape[0]
    tt = ROUTE_TILE
    assert cap >= COMBINE_CHUNK and cap % BF16_ROWS == 0
    col = np.arange(ne * COMBINE_CHUNK)
    spread = jnp.asarray(np.arange(LANES)[:, None] == (col // COMBINE_CHUNK)[None, :], BF16)
    kcol = jnp.asarray((col % COMBINE_CHUNK)[None, :], F32)
    grid_spec = pltpu.PrefetchScalarGridSpec(
        num_scalar_prefetch=2,
        grid=(n // tt,),
        in_specs=[pl.BlockSpec((tt, d), lambda j, *_: (j, 0)),
                  pl.BlockSpec((tt, LANES), lambda j, *_: (j, 0)),
                  pl.BlockSpec(tb_t.shape, lambda j, *_: (0, 0)),
                  pl.BlockSpec(spread.shape, lambda j, *_: (0, 0)),
                  pl.BlockSpec(kcol.shape, lambda j, *_: (0, 0)),
                  pl.BlockSpec(memory_space=pl.ANY)],
        out_specs=pl.BlockSpec((tt, d), lambda j, *_: (j, 0)),
        scratch_shapes=[pltpu.VMEM((ne, COMBINE_CHUNK, d), BF16), pltpu.SemaphoreType.DMA((1,))],
    )
    return pl.pallas_call(
        functools.partial(_combine_kernel, cap=cap),
        grid_spec=grid_spec,
        out_shape=jax.ShapeDtypeStruct((n, d), F32),
        compiler_params=_cparams("arbitrary"),
    )(tb_tiles, n_rounds, x1, pos_t, tb_t, spread, kcol, ye)


def _expert_kernel(xe_ref, gate_ref, wg_ref, wu_ref, wd_ref, o_ref, acc_ref):
    f = pl.program_id(2)
    x = xe_ref[0].astype(BF16)
    g = jnp.dot(x, wg_ref[0, 0], preferred_element_type=F32)
    u = jnp.dot(x, wu_ref[0, 0], preferred_element_type=F32)
    hmid = (g / (1.0 + jnp.exp(-g)) * u).astype(BF16)
    part = jnp.dot(hmid, wd_ref[0, 0], preferred_element_type=F32)

    @pl.when(f == 0)
    def _():
        acc_ref[...] = part

    @pl.when(f > 0)
    def _():
        acc_ref[...] += part

    @pl.when(f == pl.num_programs(2) - 1)
    def _():
        g_all = gate_ref[0]
        mine = lax.broadcasted_iota(jnp.int32, g_all.shape, 1) == pl.program_id(0)
        gate = jnp.sum(jnp.where(mine, g_all, 0.0), axis=-1, keepdims=True)
        o_ref[0] = (acc_ref[...] * gate).astype(o_ref.dtype)


def _expert_ffn(xe, gate, wg, wu, wd, layer, cap, tm, tf):
    ne, _, d = xe.shape
    ff = wg.shape[-1]
    return pl.pallas_call(
        _expert_kernel,
        grid=(ne, cap // tm, ff // tf),
        in_specs=[pl.BlockSpec((1, tm, d), lambda e, m, f: (e, m, 0)),
                  pl.BlockSpec((1, tm, LANES), lambda e, m, f: (e, m, 0)),
                  pl.BlockSpec((1, 1, d, tf), lambda e, m, f: (layer, e, 0, f)),
                  pl.BlockSpec((1, 1, d, tf), lambda e, m, f: (layer, e, 0, f)),
                  pl.BlockSpec((1, 1, tf, d), lambda e, m, f: (layer, e, f, 0))],
        out_specs=pl.BlockSpec((1, tm, d), lambda e, m, f: (e, m, 0)),
        out_shape=jax.ShapeDtypeStruct((ne, cap, d), BF16),
        scratch_shapes=[pltpu.VMEM((tm, d), F32)],
        compiler_params=_cparams("parallel", "parallel", "arbitrary"),
    )(xe, gate, wg, wu, wd)


def _pick_tile(n, pref):
    t = min(n, pref)
    assert n % t == 0
    return t


def _ff_tile(ff):
    best = ff
    for t in range(LANES, ff // 2 + 1, LANES):
        if ff % t == 0:
            best = t
    return best


def kernel(x_prompt, x_sample, norm1_g, w_in, diff_q_norm, diff_k_norm, lambda_q1, lambda_k1, lambda_q2,
           lambda_k2, diff_subln_g, rel_bias, na_q_norm, na_k_norm, na_rpb, w_branch_diff, w_branch_na,
           w_out, norm2_g, w_router, w_expert_gate, w_expert_up, w_expert_down):
    depth, d_model, _ = w_in.shape
    n_experts = w_router.shape[-1]
    ff = w_expert_gate.shape[-1]

    w_in_b = w_in.astype(BF16)
    wbd_b = w_branch_diff.astype(BF16)
    wbn_b = w_branch_na.astype(BF16)
    wo_b = w_out.astype(BF16)
    wg_b = w_expert_gate.astype(BF16)
    wu_b = w_expert_up.astype(BF16)
    wd_b = w_expert_down.astype(BF16)
    wr_t = jnp.swapaxes(w_router.astype(F32), 1, 2)
    wr_hi = wr_t.astype(BF16)
    wr_lo = (wr_t - wr_hi.astype(F32)).astype(BF16)
    wr_c = jnp.pad(w_router.astype(F32), ((0, 0), (0, 0), (0, LANES - n_experts)))
    wc_hi = wr_c.astype(BF16)
    wc_lo = (wr_c - wc_hi.astype(F32)).astype(BF16)
    g1 = norm1_g.astype(F32)[:, None, :]
    g2 = norm2_g.astype(F32)[:, None, :]
    tile_gain = lambda g, reps, scale: (jnp.tile(g.astype(F32), (1, reps)) * scale)[:, None, :]
    dqg = tile_gain(diff_q_norm, 2 * DIFF_HEADS, DIFF_HEAD_DIM ** -0.5 * LOG2E)
    dkg = tile_gain(diff_k_norm, 2 * DIFF_HEADS, 1.0)
    nqg = tile_gain(na_q_norm, NA_HEADS, NA_HEAD_DIM ** -0.5 * LOG2E)
    nkg = tile_gain(na_k_norm, NA_HEADS, 1.0)
    subln = diff_subln_g.astype(F32)[:, None, :]
    lam_params = jnp.stack([lambda_q1, lambda_k1, lambda_q2, lambda_k2], axis=1).astype(F32)
    seg_id = jnp.arange(MXU_DIM) // DIFF_HEAD_DIM
    seg = jnp.where(seg_id[:, None] == seg_id[None, :], 1.0 / DIFF_HEAD_DIM, 0.0).astype(BF16)
    na_bias = [_na_bias(na_rpb[l]) for l in range(depth)]

    def run(x):
        b, s, _ = x.shape
        n = b * s
        cap = CAPACITY_FACTOR * n // n_experts
        tm = _pick_tile(n, 512)
        t_attn = _pick_tile(s, 512)
        t5_tiles = _t5_bias_tiles(rel_bias, t_attn, min(s, MXU_DIM))
        x2d = x.reshape(n, d_model)
        for l in range(depth):
            lambda_init = 0.8 - 0.6 * math.exp(-0.3 * l)
            dq, dk, dv, nq, nk, nv, gates = _in_proj(x2d, l, g1, w_in_b, seg, dqg, dkg, nqg, nkg, tm)
            r3 = lambda a: a.reshape(b, s, a.shape[-1])
            yd = _diff_attention(r3(dq), r3(dk), r3(dv), t5_tiles, lam_params, subln, l, lambda_init, t_attn)
            yn = _na_attention(r3(nq), r3(nk), r3(nv), na_bias[l], NA_ROWS_PER_STEP)
            x1, h2, aff_t, aff_r = _merge_out(x2d, yd.reshape(n, DIFF_WIDTH), yn.reshape(n, NA_WIDTH), gates,
                                              wbd_b, wbn_b, wo_b, g2, wr_hi, wr_lo, wc_hi, wc_lo, l, tm)
            pos, tb128 = _select(aff_t, cap)
            per = ROUTE_TILE // LANES
            tb_tiles = jnp.concatenate([tb128[:, ::per], jnp.full((n_experts, 1), cap, jnp.int32)], axis=1)
            counts = tb_tiles[:, 1:] - tb_tiles[:, :-1]
            n_rounds = jnp.maximum(jnp.max(-(-counts // ROUTE_CHUNK), axis=0), 1).astype(jnp.int32)
            tb_t = jnp.pad(tb_tiles.T.astype(F32), ((0, 7), (0, LANES - n_experts)))
            pos_t = jnp.pad(pos.T.astype(F32), ((0, 0), (0, LANES - n_experts)), constant_values=-1.0)
            xe, gate_rows = _dispatch(h2, pos, aff_r, tb_tiles, n_rounds, cap)
            ye = _expert_ffn(xe, gate_rows, wg_b, wu_b, wd_b, l, cap, _pick_tile(cap, 512), _ff_tile(ff))
            x2d = _combine(x1, pos_t, tb_tiles, tb_t, n_rounds, ye, cap)
        return x2d.reshape(b, s, d_model)

    return (run(x_prompt), run(x_sample))
```

```python
import functools
import math

import numpy as np
import jax
import jax.numpy as jnp
from jax import lax
from jax.experimental import pallas as pl
from jax.experimental.pallas import tpu as pltpu

F32 = jnp.float32
BF16 = jnp.bfloat16

DIFF_HEADS = 4
DIFF_HEAD_DIM = 64
DIFF_WIDTH = DIFF_HEADS * 2 * DIFF_HEAD_DIM
NA_HEADS = 8
NA_HEAD_DIM = 64
NA_WIDTH = NA_HEADS * NA_HEAD_DIM
GRID_W = 64
NA_WIN_ROWS = 8
NA_WIN_COLS = 16
T5_BUCKETS = 32
T5_MAX_DIST = 128
CAPACITY_FACTOR = 2
EPS = 1e-6

LANES = 128
MXU_DIM = 256
VMEM_LIMIT_BYTES = 56 * 1024 * 1024
NEG_BIG = -1e30
LOG2E = math.log2(math.e)
NA_ROWS_PER_STEP = 4
BF16_ROWS = 16
ROUTE_TILE = 256
ROUTE_CHUNK = 64
COMBINE_CHUNK = ROUTE_CHUNK + BF16_ROWS
F32_ROWS = 8
DISPATCH_WIN = ROUTE_CHUNK + F32_ROWS


def _cparams(*sem):
    return pltpu.CompilerParams(dimension_semantics=sem, vmem_limit_bytes=VMEM_LIMIT_BYTES)


def _layer_block(a, layer):
    return pl.BlockSpec((1,) + a.shape[1:], lambda *_: (layer,) + (0,) * (a.ndim - 1))


def _nt_dot(a, b):
    return lax.dot_general(a, b, (((1,), (1,)), ((), ())), preferred_element_type=F32)


def _inproj_kernel(x_ref, g_ref, w_ref, seg_ref, dqg_ref, dkg_ref, nqg_ref, nkg_ref,
                   dq_ref, dk_ref, dv_ref, nq_ref, nk_ref, nv_ref, gate_ref, *, layer, gate_width):
    x = x_ref[...]
    ms = jnp.mean(x * x, axis=-1, keepdims=True)
    h = (x * lax.rsqrt(ms + EPS) * g_ref[layer]).astype(BF16)
    width = DIFF_WIDTH

    def proj(c0):
        return jnp.dot(h, w_ref[0, :, c0:c0 + width], preferred_element_type=F32)

    def head_norm(a, gain):
        sq = (a * a).astype(BF16)
        halves = [jnp.dot(sq[:, c:c + MXU_DIM], seg_ref[...], preferred_element_type=F32)
                  for c in range(0, width, MXU_DIM)]
        ms_seg = jnp.concatenate(halves, axis=1)
        return a * lax.rsqrt(ms_seg + EPS) * gain

    dq_ref[...] = head_norm(proj(0), dqg_ref[layer]).astype(BF16)
    dk_ref[...] = head_norm(proj(width), dkg_ref[layer]).astype(BF16)
    dv_ref[...] = proj(2 * width).astype(BF16)
    nq_ref[...] = head_norm(proj(3 * width), nqg_ref[layer]).astype(BF16)
    nk_ref[...] = head_norm(proj(4 * width), nkg_ref[layer]).astype(BF16)
    nv_ref[...] = proj(5 * width).astype(BF16)
    for c in range(0, gate_width, width):
        a = proj(6 * width + c)
        gate_ref[:, c:c + width] = (1.0 / (1.0 + jnp.exp(-a))).astype(BF16)


def _in_proj(x2d, layer, norm_g, w_in, seg, dqg, dkg, nqg, nkg, tm):
    n, d = x2d.shape
    in_width = w_in.shape[-1]
    gate_width = in_width - 6 * DIFF_WIDTH
    whole = lambda a: pl.BlockSpec(a.shape, lambda i: (0,) * a.ndim)
    row = lambda w: pl.BlockSpec((tm, w), lambda i: (i, 0))
    outs = [jax.ShapeDtypeStruct((n, DIFF_WIDTH), BF16)] * 6 + [jax.ShapeDtypeStruct((n, gate_width), BF16)]
    return pl.pallas_call(
        functools.partial(_inproj_kernel, layer=layer, gate_width=gate_width),
        grid=(n // tm,),
        in_specs=[row(d), whole(norm_g), _layer_block(w_in, layer), whole(seg), whole(dqg), whole(dkg),
                  whole(nqg), whole(nkg)],
        out_specs=[row(DIFF_WIDTH)] * 6 + [row(gate_width)],
        out_shape=outs,
        compiler_params=_cparams("parallel"),
    )(x2d, norm_g, w_in, seg, dqg, dkg, nqg, nkg)


def _diff_attn_kernel(q_ref, k_ref, v_ref, bias_ref, lam_ref, g_ref, o_ref,
                      m1_ref, a1_ref, m2_ref, a2_ref, *, layer, lambda_init):
    qi = pl.program_id(2)
    hw = 2 * DIFF_HEAD_DIM
    tq = q_ref.shape[1]
    seq = k_ref.shape[1]
    sub = bias_ref.shape[-1]

    q = q_ref[0]
    lane = lax.broadcasted_iota(jnp.int32, q.shape, 1)
    zero = jnp.zeros_like(q)
    q1 = jnp.where(lane < DIFF_HEAD_DIM, q, zero)
    q2 = jnp.where(lane >= DIFF_HEAD_DIM, q, zero)

    def update(qm, k, v_ones, bias, m_ref, a_ref, first):
        s = _nt_dot(qm, k) + bias
        m_cur = jnp.max(s, axis=-1, keepdims=True)
        if first:
            m_new = jnp.broadcast_to(m_cur, (tq, LANES))
        else:
            m_prev = m_ref[...]
            m_new = jnp.maximum(m_prev, m_cur)
        p = jnp.exp2(s - jnp.concatenate([m_new] * (sub // LANES), axis=1))
        pv = jnp.dot(p.astype(BF16), v_ones, preferred_element_type=F32)
        if first:
            a_ref[...] = pv
        else:
            alpha = jnp.exp2(m_prev - m_new)
            a_ref[...] = jnp.concatenate([alpha, alpha], axis=1) * a_ref[...] + pv
        m_ref[...] = m_new

    d_lo, _ = _t5_tile_range(tq, sub)
    n_tiles = bias_ref.shape[1]
    for c in range(seq // sub):
        k = k_ref[0, c * sub:(c + 1) * sub]
        v = v_ref[0, c * sub:(c + 1) * sub]
        v_ones = jnp.concatenate([v, jnp.ones_like(v)], axis=1)
        bias = bias_ref[0, jnp.clip(c - qi * (tq // sub) - d_lo, 0, n_tiles - 1)]
        update(q1, k, v_ones, bias, m1_ref, a1_ref, c == 0)
        update(q2, k, v_ones, bias, m2_ref, a2_ref, c == 0)

    lp = lam_ref[layer]
    lam = (jnp.exp(jnp.sum(lp[0:1] * lp[1:2], axis=-1, keepdims=True))
           - jnp.exp(jnp.sum(lp[2:3] * lp[3:4], axis=-1, keepdims=True)) + lambda_init)
    o = a1_ref[:, :hw] / a1_ref[:, hw:] - lam * (a2_ref[:, :hw] / a2_ref[:, hw:])
    ms = jnp.mean(o * o, axis=-1, keepdims=True)
    o = o * lax.rsqrt(ms + EPS) * g_ref[layer] * (1.0 - lambda_init)
    o_ref[0] = o.astype(BF16)


def _diff_attention(dq, dk, dv, bias_tiles, lam_params, subln_g, layer, lambda_init, t):
    b, s, _ = dq.shape
    hw = 2 * DIFF_HEAD_DIM
    qspec = pl.BlockSpec((1, t, hw), lambda bi, h, qi: (bi, qi, h))
    kspec = pl.BlockSpec((1, s, hw), lambda bi, h, qi: (bi, 0, h))
    bspec = pl.BlockSpec((1,) + bias_tiles.shape[1:], lambda bi, h, qi: (h, 0, 0, 0))
    whole = lambda a: pl.BlockSpec(a.shape, lambda bi, h, qi: (0,) * a.ndim)
    return pl.pallas_call(
        functools.partial(_diff_attn_kernel, layer=layer, lambda_init=lambda_init),
        grid=(b, DIFF_HEADS, s // t),
        in_specs=[qspec, kspec, kspec, bspec, whole(lam_params), whole(subln_g)],
        out_specs=qspec,
        out_shape=jax.ShapeDtypeStruct((b, s, DIFF_WIDTH), BF16),
        scratch_shapes=[pltpu.VMEM((t, LANES), F32), pltpu.VMEM((t, 2 * hw), F32),
                        pltpu.VMEM((t, LANES), F32), pltpu.VMEM((t, 2 * hw), F32)],
        compiler_params=_cparams("parallel", "parallel", "arbitrary"),
    )(dq, dk, dv, bias_tiles, lam_params, subln_g)


def _t5_tile_range(tq, sub):
    d_lo = (-(T5_MAX_DIST - 1) - sub) // sub
    d_hi = -(-(tq + T5_MAX_DIST - 1) // sub)
    return d_lo, d_hi


def _t5_bucket(rel):
    nb = T5_BUCKETS // 2
    max_exact = nb // 2
    side = jnp.where(rel > 0, nb, 0)
    n = jnp.abs(rel)
    nf = jnp.maximum(n, 1).astype(F32)
    large = max_exact + (jnp.log(nf / max_exact) / math.log(T5_MAX_DIST / max_exact) * (nb - max_exact)).astype(jnp.int32)
    large = jnp.minimum(large, nb - 1)
    return side + jnp.where(n < max_exact, n, large)


def _t5_bias_tiles(rel_bias, tq, sub):
    d_lo, d_hi = _t5_tile_range(tq, sub)
    i = jnp.arange(tq, dtype=jnp.int32)[:, None]
    j = jnp.arange(sub, dtype=jnp.int32)[None, :]
    d = jnp.arange(d_lo, d_hi + 1, dtype=jnp.int32)[:, None, None]
    rel = d * sub + (j - i)[None]
    tiles = jnp.einsum("dqkb,bh->dqkh", jax.nn.one_hot(_t5_bucket(rel), T5_BUCKETS, dtype=F32),
                       rel_bias.astype(F32), precision=lax.Precision.HIGHEST)
    return jnp.transpose(tiles, (3, 0, 1, 2)) * LOG2E


def _na_kernel(q_ref, k_ref, v_ref, bias_ref, o_ref, *, rows, rows_per_step):
    nkeys = NA_WIN_ROWS * GRID_W
    lane = lax.broadcasted_iota(jnp.int32, (GRID_W, LANES), 1)
    first_half = lane < NA_HEAD_DIM
    for j in range(rows_per_step):
        r = pl.program_id(1) * rows_per_step + j
        rs = jnp.clip(r - NA_WIN_ROWS // 2, 0, rows - NA_WIN_ROWS)
        case = r - rs
        q = q_ref[0, j]
        kwin = k_ref[0, pl.ds(rs, NA_WIN_ROWS)].reshape(nkeys, NA_WIDTH)
        vwin = v_ref[0, pl.ds(rs, NA_WIN_ROWS)].reshape(nkeys, NA_WIDTH)
        for hp in range(NA_HEADS // 2):
            c0 = hp * LANES
            qp = q[:, c0:c0 + LANES]
            zero = jnp.zeros_like(qp)
            q_stack = jnp.concatenate([jnp.where(first_half, qp, zero), jnp.where(first_half, zero, qp)], axis=0)
            vp = vwin[:, c0:c0 + LANES]
            s = _nt_dot(q_stack, kwin[:, c0:c0 + LANES]) + bias_ref[case, hp]
            m = jnp.max(s, axis=-1, keepdims=True)
            p = jnp.exp2(s - m)
            res = jnp.dot(p.astype(BF16), jnp.concatenate([vp, jnp.ones_like(vp)], axis=1),
                          preferred_element_type=F32)
            o = res[:, :LANES] / res[:, LANES:]
            o_ref[0, j, :, c0:c0 + LANES] = jnp.where(first_half, o[:GRID_W], o[GRID_W:]).astype(BF16)


def _na_attention(nq, nk, nv, bias_full, rows_per_step):
    b, s, _ = nq.shape
    rows = s // GRID_W
    assert rows >= NA_WIN_ROWS and rows % rows_per_step == 0
    shp = (b, rows, GRID_W, NA_WIDTH)
    qspec = pl.BlockSpec((1, rows_per_step, GRID_W, NA_WIDTH), lambda bi, r: (bi, r, 0, 0))
    kspec = pl.BlockSpec((1, rows, GRID_W, NA_WIDTH), lambda bi, r: (bi, 0, 0, 0))
    bspec = pl.BlockSpec(bias_full.shape, lambda bi, r: (0, 0, 0, 0))
    out = pl.pallas_call(
        functools.partial(_na_kernel, rows=rows, rows_per_step=rows_per_step),
        grid=(b, rows // rows_per_step),
        in_specs=[qspec, kspec, kspec, bspec],
        out_specs=qspec,
        out_shape=jax.ShapeDtypeStruct(shp, BF16),
        compiler_params=_cparams("parallel", "arbitrary"),
    )(nq.reshape(shp), nk.reshape(shp), nv.reshape(shp), bias_full)
    return out.reshape(b, s, NA_WIDTH)


def _na_bias(rpb):
    case = jnp.arange(NA_WIN_ROWS)[:, None]
    w = jnp.arange(NA_WIN_ROWS)[None, :]
    dr_idx = w - case + (NA_WIN_ROWS - 1)
    c = jnp.arange(GRID_W)[:, None]
    kc = jnp.arange(GRID_W)[None, :]
    cs = jnp.clip(c - NA_WIN_COLS // 2, 0, GRID_W - NA_WIN_COLS)
    valid = (kc >= cs) & (kc < cs + NA_WIN_COLS)
    dc_idx = jnp.clip(kc - c + NA_WIN_COLS - 1, 0, 2 * NA_WIN_COLS - 2)
    tab = rpb.astype(F32)[:, dr_idx]
    tab = tab[:, :, :, dc_idx]
    tab = jnp.where(valid[None, None, None], tab * LOG2E, NEG_BIG)
    tab = jnp.transpose(tab, (1, 0, 3, 2, 4))
    return tab.reshape(NA_WIN_ROWS, NA_HEADS // 2, 2 * GRID_W, NA_WIN_ROWS * GRID_W)


def _merge_kernel(x_ref, yd_ref, yn_ref, gate_ref, wbd_ref, wbn_ref, wo_ref, g2_ref, wrh_ref, wrl_ref,
                  x1_ref, h2_ref, aff_ref, *, layer):
    d = x_ref.shape[-1]
    bd = jnp.dot(yd_ref[...], wbd_ref[0], preferred_element_type=F32)
    bn = jnp.dot(yn_ref[...], wbn_ref[0], preferred_element_type=F32)
    merged = gate_ref[:, :d].astype(F32) * bd + gate_ref[:, d:].astype(F32) * bn
    x1 = x_ref[...] + jnp.dot(merged.astype(BF16), wo_ref[0], preferred_element_type=F32)
    x1_ref[...] = x1
    ms = jnp.mean(x1 * x1, axis=-1, keepdims=True)
    t = x1 * lax.rsqrt(ms + EPS) * g2_ref[layer]
    t_hi = t.astype(BF16)
    h2_ref[...] = t_hi
    t_lo = (t - t_hi.astype(F32)).astype(BF16)
    wrh = wrh_ref[layer]
    logits = _nt_dot(wrh, t_hi) + _nt_dot(wrh, t_lo) + _nt_dot(wrl_ref[layer], t_hi)
    mx = jnp.max(logits, axis=0, keepdims=True)
    e = jnp.exp(logits - mx)
    aff_ref[...] = e / jnp.sum(e, axis=0, keepdims=True)


def _merge_out(x2d, yd, yn, gates, wbd, wbn, wo, g2, wrh, wrl, layer, tm):
    n, d = x2d.shape
    ne = wrh.shape[1]
    whole = lambda a: pl.BlockSpec(a.shape, lambda i: (0,) * a.ndim)
    row = lambda w: pl.BlockSpec((tm, w), lambda i: (i, 0))
    return pl.pallas_call(
        functools.partial(_merge_kernel, layer=layer),
        grid=(n // tm,),
        in_specs=[row(d), row(DIFF_WIDTH), row(NA_WIDTH), row(2 * d), _layer_block(wbd, layer),
                  _layer_block(wbn, layer), _layer_block(wo, layer), whole(g2), whole(wrh), whole(wrl)],
        out_specs=[row(d), row(d), pl.BlockSpec((ne, tm), lambda i: (0, i))],
        out_shape=[jax.ShapeDtypeStruct((n, d), F32), jax.ShapeDtypeStruct((n, d), BF16),
                   jax.ShapeDtypeStruct((ne, n), F32)],
        compiler_params=_cparams("parallel"),
    )(x2d, yd, yn, gates, wbd, wbn, wo, g2, wrh, wrl)


def _select_kernel(aff_ref, pos_ref, tb_ref, *, cap):
    ne, nt, _ = aff_ref.shape
    bits = lambda e: pltpu.bitcast(aff_ref[e], jnp.int32)

    def bit_step(i, thr):
        cand = thr | lax.shift_left(jnp.int32(1), 30 - i)
        rows = []
        for e in range(ne):
            ce = cand[e:e + 1]
            ge = (bits(e) >= ce).astype(jnp.int32)
            cnt = jnp.sum(jnp.sum(ge, axis=0, keepdims=True), axis=1, keepdims=True)
            rows.append(jnp.where(cnt >= cap, ce, thr[e:e + 1]))
        return jnp.concatenate(rows, axis=0)

    thr = lax.fori_loop(0, 31, bit_step, jnp.zeros((ne, LANES), jnp.int32))

    li = lax.broadcasted_iota(jnp.int32, (LANES, LANES), 0)
    lj = lax.broadcasted_iota(jnp.int32, (LANES, LANES), 1)
    upper = (li < lj).astype(BF16)
    ones = jnp.ones((LANES, LANES), BF16)
    ti = lax.broadcasted_iota(jnp.int32, (nt, nt), 0)
    tj = lax.broadcasted_iota(jnp.int32, (nt, nt), 1)
    lower = (tj < ti).astype(BF16)

    def prefix(flags):
        fb = flags.astype(BF16)
        within = jnp.dot(fb, upper, preferred_element_type=F32)
        per_tile = jnp.dot(fb, ones, preferred_element_type=F32)
        before = jnp.dot(lower, per_tile.astype(BF16), preferred_element_type=F32)
        return before + within, before

    for e in range(ne):
        be = bits(e)
        te = thr[e:e + 1]
        gt = be > te
        eq = be == te
        n_gt = jnp.sum(jnp.sum(gt.astype(F32), axis=0, keepdims=True), axis=1, keepdims=True)
        eq_rank, _ = prefix(eq)
        sel = gt | (eq & (eq_rank < cap - n_gt))
        pos, before = prefix(sel)
        pos_ref[e] = jnp.where(sel, pos, -1.0).astype(jnp.int32)
        tb_ref[e] = before


def _select(aff_t, cap):
    ne, n = aff_t.shape
    nt = n // LANES
    shp = (ne, nt, LANES)
    spec = pl.BlockSpec(shp, lambda i: (0, 0, 0))
    pos, tb = pl.pallas_call(
        functools.partial(_select_kernel, cap=cap),
        grid=(1,),
        in_specs=[spec],
        out_specs=[spec, spec],
        out_shape=[jax.ShapeDtypeStruct(shp, jnp.int32), jax.ShapeDtypeStruct(shp, F32)],
        compiler_params=_cparams("arbitrary"),
    )(aff_t.reshape(shp))
    return pos.reshape(ne, n), tb[:, :, 0].astype(jnp.int32)


def _dispatch_kernel(tb_ref, nr_ref, h_ref, pos_ref, aff_ref, xe_hbm, gc_hbm, xs_ref, gs_ref, last_ref, pend_ref,
                     sem, *, cap):
    j = pl.program_id(0)
    ne = pos_ref.shape[0]
    tt = h_ref.shape[0]

    @pl.when(j == 0)
    def _():
        xs_ref[...] = jnp.zeros(xs_ref.shape, F32)
        gs_ref[...] = jnp.zeros(gs_ref.shape, F32)
        for e in range(ne):
            last_ref[e] = 0
            pend_ref[e] = 0
        pad = [(pltpu.make_async_copy(xs_ref.at[e], xe_hbm.at[e, pl.ds(cap + q * DISPATCH_WIN, DISPATCH_WIN)], sem.at[0]),
                pltpu.make_async_copy(gs_ref.at[e], gc_hbm.at[e, pl.ds(cap + q * DISPATCH_WIN, DISPATCH_WIN)], sem.at[1]))
               for e in range(ne) for q in range((xe_hbm.shape[1] - cap) // DISPATCH_WIN)]
        for cx, cg in pad:
            cx.start()
            cg.start()
        for cx, cg in pad:
            cx.wait()
            cg.wait()

    h = h_ref[...]
    slot = lax.broadcasted_iota(jnp.int32, (DISPATCH_WIN, tt), 0)
    row = lax.broadcasted_iota(jnp.int32, (F32_ROWS, 1), 0)

    def copies(e, start):
        dst = pl.ds(pl.multiple_of(start, F32_ROWS), DISPATCH_WIN)
        return (pltpu.make_async_copy(xs_ref.at[e], xe_hbm.at[e, dst], sem.at[0]),
                pltpu.make_async_copy(gs_ref.at[e], gc_hbm.at[e, dst], sem.at[1]))

    def drain():
        for e in range(ne):
            @pl.when(pend_ref[e] == 1)
            def _():
                for c in copies(e, 0):
                    c.wait()
                pend_ref[e] = 0

    def one_round(r, carry):
        first = [tb_ref[e, j] for e in range(ne)]
        starts = [lax.shift_left(lax.shift_right_logical(first[e], 3), 3) + r * ROUTE_CHUNK for e in range(ne)]
        live = [(r == 0) | (tb_ref[e, j + 1] > first[e] + r * ROUTE_CHUNK) for e in range(ne)]
        hits = [pos_ref[e:e + 1, :] - starts[e] == slot for e in range(ne)]
        onehot = jnp.concatenate([hit.astype(BF16) for hit in hits], axis=0)
        xs = jnp.dot(onehot, h, preferred_element_type=F32)
        gates = [jnp.sum(jnp.where(hits[e], aff_ref[e:e + 1, :], 0.0), axis=1, keepdims=True) for e in range(ne)]
        drain()
        for e in range(ne):
            @pl.when(live[e])
            def _():
                n_carry = jnp.where(r == 0, first[e] - starts[e], 0)
                off = pl.multiple_of(jnp.where(r == 0, starts[e] - last_ref[e], 0), F32_ROWS)
                keep = row < n_carry
                lo = e * DISPATCH_WIN
                gs = jnp.broadcast_to(gates[e], (DISPATCH_WIN, LANES))
                x_head = xs[lo:lo + F32_ROWS] + jnp.where(keep, xs_ref[e, pl.ds(off, F32_ROWS)], 0.0)
                g_head = gs[:F32_ROWS] + jnp.where(keep, gs_ref[e, pl.ds(off, F32_ROWS)], 0.0)
                xs_ref[e, F32_ROWS:] = xs[lo + F32_ROWS:lo + DISPATCH_WIN]
                gs_ref[e, F32_ROWS:] = gs[F32_ROWS:]
                xs_ref[e, :F32_ROWS] = x_head
                gs_ref[e, :F32_ROWS] = g_head
                last_ref[e] = starts[e]
                for c in copies(e, starts[e]):
                    c.start()
                pend_ref[e] = 1
        return carry

    lax.fori_loop(0, nr_ref[j], one_round, 0)

    @pl.when(j == pl.num_programs(0) - 1)
    def _():
        drain()


def _dispatch(h2, pos, aff_t, tb_tiles, n_rounds, cap):
    n, d = h2.shape
    ne = pos.shape[0]
    tt = ROUTE_TILE
    assert cap % F32_ROWS == 0
    cap_pad = cap + (tt // ROUTE_CHUNK) * DISPATCH_WIN
    grid_spec = pltpu.PrefetchScalarGridSpec(
        num_scalar_prefetch=2,
        grid=(n // tt,),
        in_specs=[pl.BlockSpec((tt, d), lambda j, *_: (j, 0)),
                  pl.BlockSpec((ne, tt), lambda j, *_: (0, j)),
                  pl.BlockSpec((ne, tt), lambda j, *_: (0, j))],
        out_specs=[pl.BlockSpec(memory_space=pl.ANY), pl.BlockSpec(memory_space=pl.ANY)],
        scratch_shapes=[pltpu.VMEM((ne, DISPATCH_WIN, d), F32), pltpu.VMEM((ne, DISPATCH_WIN, LANES), F32),
                        pltpu.SMEM((ne,), jnp.int32), pltpu.SMEM((ne,), jnp.int32),
                        pltpu.SemaphoreType.DMA((2,))],
    )
    return pl.pallas_call(
        functools.partial(_dispatch_kernel, cap=cap),
        grid_spec=grid_spec,
        out_shape=[jax.ShapeDtypeStruct((ne, cap_pad, d), F32), jax.ShapeDtypeStruct((ne, cap_pad, LANES), F32)],
        compiler_params=_cparams("arbitrary"),
    )(tb_tiles, n_rounds, h2, pos, aff_t)


def _combine_kernel(tb_ref, nr_ref, x_ref, post_ref, tbt_ref, spread_ref, kcol_ref, ye_hbm, o_ref, ch_ref, sem,
                    *, cap):
    j = pl.program_id(0)
    ne = ye_hbm.shape[0]
    o_ref[...] = x_ref[...]
    spread = spread_ref[...]
    kcol = kcol_ref[...]
    last = cap - COMBINE_CHUNK

    buf = j % 2

    def chunk_copies(tile, r, b):
        def start_of(e):
            s = tb_ref[e, tile] + r * ROUTE_CHUNK
            return jnp.minimum(lax.shift_left(lax.shift_right_logical(s, 4), 4), last)
        return [pltpu.make_async_copy(ye_hbm.at[e, pl.ds(pl.multiple_of(start_of(e), BF16_ROWS), COMBINE_CHUNK)],
                                      ch_ref.at[b, e], sem.at[b]) for e in range(ne)]

    @pl.when(j == 0)
    def _():
        for c in chunk_copies(0, 0, 0):
            c.start()

    @pl.when(j + 1 < pl.num_programs(0))
    def _():
        for c in chunk_copies(j + 1, 0, 1 - buf):
            c.start()

    def one_round(r, carry):
        cps = chunk_copies(j, r, buf)

        @pl.when(r > 0)
        def _():
            for c in cps:
                c.start()
        first = tbt_ref[pl.ds(j, 1), :] + (r * ROUTE_CHUNK).astype(F32)
        sv = jnp.minimum(jnp.floor(first * (1.0 / BF16_ROWS)) * BF16_ROWS, float(last))
        post = post_ref[...]
        mine = (post >= first) & (post < first + ROUTE_CHUNK)
        rel = jnp.where(mine, post - sv, -1.0).astype(BF16)
        rel_wide = jnp.dot(rel, spread, preferred_element_type=F32)
        onehot = (rel_wide == kcol).astype(BF16)
        for c in cps:
            c.wait()
        rows = ch_ref[buf].reshape(ne * COMBINE_CHUNK, ch_ref.shape[-1])
        o_ref[...] += jnp.dot(onehot, rows, preferred_element_type=F32)
        return carry

    lax.fori_loop(0, nr_ref[j], one_round, 0)


def _combine(x1, pos_t, tb_tiles, tb_t, n_rounds, ye, cap):
    n, d = x1.shape
    ne = ye.shape[0]
    tt = ROUTE_TILE
    assert cap >= COMBINE_CHUNK and cap % BF16_ROWS == 0
    col = np.arange(ne * COMBINE_CHUNK)
    spread = jnp.asarray(np.arange(LANES)[:, None] == (col // COMBINE_CHUNK)[None, :], BF16)
    kcol = jnp.asarray((col % COMBINE_CHUNK)[None, :], F32)
    grid_spec = pltpu.PrefetchScalarGridSpec(
        num_scalar_prefetch=2,
        grid=(n // tt,),
        in_specs=[pl.BlockSpec((tt, d), lambda j, *_: (j, 0)),
                  pl.BlockSpec((tt, LANES), lambda j, *_: (j, 0)),
                  pl.BlockSpec(tb_t.shape, lambda j, *_: (0, 0)),
                  pl.BlockSpec(spread.shape, lambda j, *_: (0, 0)),
                  pl.BlockSpec(kcol.shape, lambda j, *_: (0, 0)),
                  pl.BlockSpec(memory_space=pl.ANY)],
        out_specs=pl.BlockSpec((tt, d), lambda j, *_: (j, 0)),
        scratch_shapes=[pltpu.VMEM((2, ne, COMBINE_CHUNK, d), BF16), pltpu.SemaphoreType.DMA((2,))],
    )
    return pl.pallas_call(
        functools.partial(_combine_kernel, cap=cap),
        grid_spec=grid_spec,
        out_shape=jax.ShapeDtypeStruct((n, d), F32),
        compiler_params=_cparams("arbitrary"),
    )(tb_tiles, n_rounds, x1, pos_t, tb_t, spread, kcol, ye)


def _expert_kernel(xe_ref, gate_ref, wg_ref, wu_ref, wd_ref, o_ref, acc_ref):
    f = pl.program_id(2)
    x = xe_ref[0].astype(BF16)
    g = jnp.dot(x, wg_ref[0, 0], preferred_element_type=F32)
    u = jnp.dot(x, wu_ref[0, 0], preferred_element_type=F32)
    hmid = (g / (1.0 + jnp.exp(-g)) * u).astype(BF16)
    part = jnp.dot(hmid, wd_ref[0, 0], preferred_element_type=F32)

    @pl.when(f == 0)
    def _():
        acc_ref[...] = part

    @pl.when(f > 0)
    def _():
        acc_ref[...] += part

    @pl.when(f == pl.num_programs(2) - 1)
    def _():
        gate = gate_ref[0]
        gate = jnp.concatenate([gate] * (acc_ref.shape[1] // LANES), axis=1)
        o_ref[0] = (acc_ref[...] * gate).astype(o_ref.dtype)


def _expert_ffn(xe, gate, wg, wu, wd, layer, cap, tm, tf):
    ne, _, d = xe.shape
    ff = wg.shape[-1]
    return pl.pallas_call(
        _expert_kernel,
        grid=(ne, cap // tm, ff // tf),
        in_specs=[pl.BlockSpec((1, tm, d), lambda e, m, f: (e, m, 0)),
                  pl.BlockSpec((1, tm, LANES), lambda e, m, f: (e, m, 0)),
                  pl.BlockSpec((1, 1, d, tf), lambda e, m, f: (layer, e, 0, f)),
                  pl.BlockSpec((1, 1, d, tf), lambda e, m, f: (layer, e, 0, f)),
                  pl.BlockSpec((1, 1, tf, d), lambda e, m, f: (layer, e, f, 0))],
        out_specs=pl.BlockSpec((1, tm, d), lambda e, m, f: (e, m, 0)),
        out_shape=jax.ShapeDtypeStruct((ne, cap, d), BF16),
        scratch_shapes=[pltpu.VMEM((tm, d), F32)],
        compiler_params=_cparams("parallel", "parallel", "arbitrary"),
    )(xe, gate, wg, wu, wd)


def _pick_tile(n, pref):
    t = min(n, pref)
    assert n % t == 0
    return t


def _ff_tile(ff):
    best = ff
    for t in range(LANES, ff // 2 + 1, LANES):
        if ff % t == 0:
            best = t
    return best


def kernel(x_prompt, x_sample, norm1_g, w_in, diff_q_norm, diff_k_norm, lambda_q1, lambda_k1, lambda_q2,
           lambda_k2, diff_subln_g, rel_bias, na_q_norm, na_k_norm, na_rpb, w_branch_diff, w_branch_na,
           w_out, norm2_g, w_router, w_expert_gate, w_expert_up, w_expert_down):
    depth, d_model, _ = w_in.shape
    n_experts = w_router.shape[-1]
    ff = w_expert_gate.shape[-1]

    w_in_b = w_in.astype(BF16)
    wbd_b = w_branch_diff.astype(BF16)
    wbn_b = w_branch_na.astype(BF16)
    wo_b = w_out.astype(BF16)
    wg_b = w_expert_gate.astype(BF16)
    wu_b = w_expert_up.astype(BF16)
    wd_b = w_expert_down.astype(BF16)
    wr_t = jnp.swapaxes(w_router.astype(F32), 1, 2)
    wr_hi = wr_t.astype(BF16)
    wr_lo = (wr_t - wr_hi.astype(F32)).astype(BF16)
    g1 = norm1_g.astype(F32)[:, None, :]
    g2 = norm2_g.astype(F32)[:, None, :]
    tile_gain = lambda g, reps, scale: (jnp.tile(g.astype(F32), (1, reps)) * scale)[:, None, :]
    dqg = tile_gain(diff_q_norm, 2 * DIFF_HEADS, DIFF_HEAD_DIM ** -0.5 * LOG2E)
    dkg = tile_gain(diff_k_norm, 2 * DIFF_HEADS, 1.0)
    nqg = tile_gain(na_q_norm, NA_HEADS, NA_HEAD_DIM ** -0.5 * LOG2E)
    nkg = tile_gain(na_k_norm, NA_HEADS, 1.0)
    subln = diff_subln_g.astype(F32)[:, None, :]
    lam_params = jnp.stack([lambda_q1, lambda_k1, lambda_q2, lambda_k2], axis=1).astype(F32)
    seg_id = jnp.arange(MXU_DIM) // DIFF_HEAD_DIM
    seg = jnp.where(seg_id[:, None] == seg_id[None, :], 1.0 / DIFF_HEAD_DIM, 0.0).astype(BF16)
    na_bias = [_na_bias(na_rpb[l]) for l in range(depth)]

    def run(x):
        b, s, _ = x.shape
        n = b * s
        cap = CAPACITY_FACTOR * n // n_experts
        tm = _pick_tile(n, 512)
        t_attn = _pick_tile(s, 512)
        t5_tiles = _t5_bias_tiles(rel_bias, t_attn, min(s, MXU_DIM))
        x2d = x.reshape(n, d_model)
        for l in range(depth):
            lambda_init = 0.8 - 0.6 * math.exp(-0.3 * l)
            dq, dk, dv, nq, nk, nv, gates = _in_proj(x2d, l, g1, w_in_b, seg, dqg, dkg, nqg, nkg, tm)
            r3 = lambda a: a.reshape(b, s, a.shape[-1])
            yd = _diff_attention(r3(dq), r3(dk), r3(dv), t5_tiles, lam_params, subln, l, lambda_init, t_attn)
            yn = _na_attention(r3(nq), r3(nk), r3(nv), na_bias[l], NA_ROWS_PER_STEP)
            x1, h2, aff_t = _merge_out(x2d, yd.reshape(n, DIFF_WIDTH), yn.reshape(n, NA_WIDTH), gates,
                                       wbd_b, wbn_b, wo_b, g2, wr_hi, wr_lo, l, tm)
            pos, tb128 = _select(aff_t, cap)
            per = ROUTE_TILE // LANES
            tb_tiles = jnp.concatenate([tb128[:, ::per], jnp.full((n_experts, 1), cap, jnp.int32)], axis=1)
            counts = tb_tiles[:, 1:] - tb_tiles[:, :-1]
            n_rounds = jnp.maximum(jnp.max(-(-counts // ROUTE_CHUNK), axis=0), 1).astype(jnp.int32)
            tb_t = jnp.pad(tb_tiles.T.astype(F32), ((0, 7), (0, LANES - n_experts)))
            pos_t = jnp.pad(pos.T.astype(F32), ((0, 0), (0, LANES - n_experts)), constant_values=-1.0)
            xe, gate_rows = _dispatch(h2, pos, aff_t, tb_tiles, n_rounds, cap)
            ye = _expert_ffn(xe, gate_rows, wg_b, wu_b, wd_b, l, cap, _pick_tile(cap, 512), _ff_tile(ff))
            x2d = _combine(x1, pos_t, tb_tiles, tb_t, n_rounds, ye, cap)
        return x2d.reshape(b, s, d_model)

    return (run(x_prompt), run(x_sample))
```

```python
import functools
import math

import numpy as np
import jax
import jax.numpy as jnp
from jax import lax
from jax.experimental import pallas as pl
from jax.experimental.pallas import tpu as pltpu

F32 = jnp.float32
BF16 = jnp.bfloat16

DIFF_HEADS = 4
DIFF_HEAD_DIM = 64
DIFF_WIDTH = DIFF_HEADS * 2 * DIFF_HEAD_DIM
NA_HEADS = 8
NA_HEAD_DIM = 64
NA_WIDTH = NA_HEADS * NA_HEAD_DIM
GRID_W = 64
NA_WIN_ROWS = 8
NA_WIN_COLS = 16
T5_BUCKETS = 32
T5_MAX_DIST = 128
CAPACITY_FACTOR = 2
EPS = 1e-6

LANES = 128
MXU_DIM = 256
VMEM_LIMIT_BYTES = 56 * 1024 * 1024
NEG_BIG = -1e30
LOG2E = math.log2(math.e)
NA_ROWS_PER_STEP = 4
BF16_ROWS = 16
ROUTE_TILE = 512
ROUTE_CHUNK = 96
COMBINE_CHUNK = ROUTE_CHUNK + BF16_ROWS
DISPATCH_WIN = ROUTE_CHUNK + BF16_ROWS


def _cparams(*sem):
    return pltpu.CompilerParams(dimension_semantics=sem, vmem_limit_bytes=VMEM_LIMIT_BYTES)


def _layer_block(a, layer):
    return pl.BlockSpec((1,) + a.shape[1:], lambda *_: (layer,) + (0,) * (a.ndim - 1))


def _nt_dot(a, b):
    return lax.dot_general(a, b, (((1,), (1,)), ((), ())), preferred_element_type=F32)


def _inproj_kernel(x_ref, g_ref, w_ref, seg_ref, dqg_ref, dkg_ref, nqg_ref, nkg_ref,
                   dq_ref, dk_ref, dv_ref, nq_ref, nk_ref, nv_ref, gate_ref, *, layer, gate_width):
    x = x_ref[...]
    ms = jnp.mean(x * x, axis=-1, keepdims=True)
    h = (x * lax.rsqrt(ms + EPS) * g_ref[layer]).astype(BF16)
    width = DIFF_WIDTH

    def proj(c0):
        return jnp.dot(h, w_ref[0, :, c0:c0 + width], preferred_element_type=F32)

    def head_norm(a, gain):
        sq = (a * a).astype(BF16)
        halves = [jnp.dot(sq[:, c:c + MXU_DIM], seg_ref[...], preferred_element_type=F32)
                  for c in range(0, width, MXU_DIM)]
        ms_seg = jnp.concatenate(halves, axis=1)
        return a * lax.rsqrt(ms_seg + EPS) * gain

    dq_ref[...] = head_norm(proj(0), dqg_ref[layer]).astype(BF16)
    dk_ref[...] = head_norm(proj(width), dkg_ref[layer]).astype(BF16)
    dv_ref[...] = proj(2 * width).astype(BF16)
    nq_ref[...] = head_norm(proj(3 * width), nqg_ref[layer]).astype(BF16)
    nk_ref[...] = head_norm(proj(4 * width), nkg_ref[layer]).astype(BF16)
    nv_ref[...] = proj(5 * width).astype(BF16)
    for c in range(0, gate_width, width):
        a = proj(6 * width + c)
        gate_ref[:, c:c + width] = (1.0 / (1.0 + jnp.exp(-a))).astype(BF16)


def _in_proj(x2d, layer, norm_g, w_in, seg, dqg, dkg, nqg, nkg, tm):
    n, d = x2d.shape
    in_width = w_in.shape[-1]
    gate_width = in_width - 6 * DIFF_WIDTH
    whole = lambda a: pl.BlockSpec(a.shape, lambda i: (0,) * a.ndim)
    row = lambda w: pl.BlockSpec((tm, w), lambda i: (i, 0))
    outs = [jax.ShapeDtypeStruct((n, DIFF_WIDTH), BF16)] * 6 + [jax.ShapeDtypeStruct((n, gate_width), BF16)]
    return pl.pallas_call(
        functools.partial(_inproj_kernel, layer=layer, gate_width=gate_width),
        grid=(n // tm,),
        in_specs=[row(d), whole(norm_g), _layer_block(w_in, layer), whole(seg), whole(dqg), whole(dkg),
                  whole(nqg), whole(nkg)],
        out_specs=[row(DIFF_WIDTH)] * 6 + [row(gate_width)],
        out_shape=outs,
        compiler_params=_cparams("parallel"),
    )(x2d, norm_g, w_in, seg, dqg, dkg, nqg, nkg)


def _diff_attn_kernel(q_ref, k_ref, v_ref, bias_ref, lam_ref, g_ref, o_ref,
                      m1_ref, a1_ref, m2_ref, a2_ref, *, layer, lambda_init):
    qi = pl.program_id(2)
    hw = 2 * DIFF_HEAD_DIM
    tq = q_ref.shape[1]
    seq = k_ref.shape[1]
    sub = bias_ref.shape[-1]

    q = q_ref[0]
    lane = lax.broadcasted_iota(jnp.int32, q.shape, 1)
    zero = jnp.zeros_like(q)
    q1 = jnp.where(lane < DIFF_HEAD_DIM, q, zero)
    q2 = jnp.where(lane >= DIFF_HEAD_DIM, q, zero)

    def update(qm, k, v_ones, bias, m_ref, a_ref, first):
        s = _nt_dot(qm, k) + bias
        m_cur = jnp.max(s, axis=-1, keepdims=True)
        if first:
            m_new = jnp.broadcast_to(m_cur, (tq, LANES))
        else:
            m_prev = m_ref[...]
            m_new = jnp.maximum(m_prev, m_cur)
        p = jnp.exp2(s - jnp.concatenate([m_new] * (sub // LANES), axis=1))
        pv = jnp.dot(p.astype(BF16), v_ones, preferred_element_type=F32)
        if first:
            a_ref[...] = pv
        else:
            alpha = jnp.exp2(m_prev - m_new)
            a_ref[...] = jnp.concatenate([alpha, alpha], axis=1) * a_ref[...] + pv
        m_ref[...] = m_new

    d_lo, _ = _t5_tile_range(tq, sub)
    n_tiles = bias_ref.shape[1]
    for c in range(seq // sub):
        k = k_ref[0, c * sub:(c + 1) * sub]
        v = v_ref[0, c * sub:(c + 1) * sub]
        v_ones = jnp.concatenate([v, jnp.ones_like(v)], axis=1)
        bias = bias_ref[0, jnp.clip(c - qi * (tq // sub) - d_lo, 0, n_tiles - 1)]
        update(q1, k, v_ones, bias, m1_ref, a1_ref, c == 0)
        update(q2, k, v_ones, bias, m2_ref, a2_ref, c == 0)

    lp = lam_ref[layer]
    lam = (jnp.exp(jnp.sum(lp[0:1] * lp[1:2], axis=-1, keepdims=True))
           - jnp.exp(jnp.sum(lp[2:3] * lp[3:4], axis=-1, keepdims=True)) + lambda_init)
    o = a1_ref[:, :hw] / a1_ref[:, hw:] - lam * (a2_ref[:, :hw] / a2_ref[:, hw:])
    ms = jnp.mean(o * o, axis=-1, keepdims=True)
    o = o * lax.rsqrt(ms + EPS) * g_ref[layer] * (1.0 - lambda_init)
    o_ref[0] = o.astype(BF16)


def _diff_attention(dq, dk, dv, bias_tiles, lam_params, subln_g, layer, lambda_init, t):
    b, s, _ = dq.shape
    hw = 2 * DIFF_HEAD_DIM
    qspec = pl.BlockSpec((1, t, hw), lambda bi, h, qi: (bi, qi, h))
    kspec = pl.BlockSpec((1, s, hw), lambda bi, h, qi: (bi, 0, h))
    bspec = pl.BlockSpec((1,) + bias_tiles.shape[1:], lambda bi, h, qi: (h, 0, 0, 0))
    whole = lambda a: pl.BlockSpec(a.shape, lambda bi, h, qi: (0,) * a.ndim)
    return pl.pallas_call(
        functools.partial(_diff_attn_kernel, layer=layer, lambda_init=lambda_init),
        grid=(b, DIFF_HEADS, s // t),
        in_specs=[qspec, kspec, kspec, bspec, whole(lam_params), whole(subln_g)],
        out_specs=qspec,
        out_shape=jax.ShapeDtypeStruct((b, s, DIFF_WIDTH), BF16),
        scratch_shapes=[pltpu.VMEM((t, LANES), F32), pltpu.VMEM((t, 2 * hw), F32),
                        pltpu.VMEM((t, LANES), F32), pltpu.VMEM((t, 2 * hw), F32)],
        compiler_params=_cparams("parallel", "parallel", "arbitrary"),
    )(dq, dk, dv, bias_tiles, lam_params, subln_g)


def _t5_tile_range(tq, sub):
    d_lo = (-(T5_MAX_DIST - 1) - sub) // sub
    d_hi = -(-(tq + T5_MAX_DIST - 1) // sub)
    return d_lo, d_hi


def _t5_bucket(rel):
    nb = T5_BUCKETS // 2
    max_exact = nb // 2
    side = jnp.where(rel > 0, nb, 0)
    n = jnp.abs(rel)
    nf = jnp.maximum(n, 1).astype(F32)
    large = max_exact + (jnp.log(nf / max_exact) / math.log(T5_MAX_DIST / max_exact) * (nb - max_exact)).astype(jnp.int32)
    large = jnp.minimum(large, nb - 1)
    return side + jnp.where(n < max_exact, n, large)


def _t5_bias_tiles(rel_bias, tq, sub):
    d_lo, d_hi = _t5_tile_range(tq, sub)
    i = jnp.arange(tq, dtype=jnp.int32)[:, None]
    j = jnp.arange(sub, dtype=jnp.int32)[None, :]
    d = jnp.arange(d_lo, d_hi + 1, dtype=jnp.int32)[:, None, None]
    rel = d * sub + (j - i)[None]
    tiles = jnp.einsum("dqkb,bh->dqkh", jax.nn.one_hot(_t5_bucket(rel), T5_BUCKETS, dtype=F32),
                       rel_bias.astype(F32), precision=lax.Precision.HIGHEST)
    return jnp.transpose(tiles, (3, 0, 1, 2)) * LOG2E


def _na_kernel(q_ref, k_ref, v_ref, bias_ref, o_ref, *, rows, rows_per_step):
    nkeys = NA_WIN_ROWS * GRID_W
    lane = lax.broadcasted_iota(jnp.int32, (GRID_W, LANES), 1)
    first_half = lane < NA_HEAD_DIM
    for j in range(rows_per_step):
        r = pl.program_id(1) * rows_per_step + j
        rs = jnp.clip(r - NA_WIN_ROWS // 2, 0, rows - NA_WIN_ROWS)
        case = r - rs
        q = q_ref[0, j]
        kwin = k_ref[0, pl.ds(rs, NA_WIN_ROWS)].reshape(nkeys, NA_WIDTH)
        vwin = v_ref[0, pl.ds(rs, NA_WIN_ROWS)].reshape(nkeys, NA_WIDTH)
        for hp in range(NA_HEADS // 2):
            c0 = hp * LANES
            qp = q[:, c0:c0 + LANES]
            zero = jnp.zeros_like(qp)
            q_stack = jnp.concatenate([jnp.where(first_half, qp, zero), jnp.where(first_half, zero, qp)], axis=0)
            vp = vwin[:, c0:c0 + LANES]
            s = _nt_dot(q_stack, kwin[:, c0:c0 + LANES]) + bias_ref[case, hp]
            m = jnp.max(s, axis=-1, keepdims=True)
            p = jnp.exp2(s - m)
            res = jnp.dot(p.astype(BF16), jnp.concatenate([vp, jnp.ones_like(vp)], axis=1),
                          preferred_element_type=F32)
            o = res[:, :LANES] / res[:, LANES:]
            o_ref[0, j, :, c0:c0 + LANES] = jnp.where(first_half, o[:GRID_W], o[GRID_W:]).astype(BF16)


def _na_attention(nq, nk, nv, bias_full, rows_per_step):
    b, s, _ = nq.shape
    rows = s // GRID_W
    assert rows >= NA_WIN_ROWS and rows % rows_per_step == 0
    shp = (b, rows, GRID_W, NA_WIDTH)
    qspec = pl.BlockSpec((1, rows_per_step, GRID_W, NA_WIDTH), lambda bi, r: (bi, r, 0, 0))
    kspec = pl.BlockSpec((1, rows, GRID_W, NA_WIDTH), lambda bi, r: (bi, 0, 0, 0))
    bspec = pl.BlockSpec(bias_full.shape, lambda bi, r: (0, 0, 0, 0))
    out = pl.pallas_call(
        functools.partial(_na_kernel, rows=rows, rows_per_step=rows_per_step),
        grid=(b, rows // rows_per_step),
        in_specs=[qspec, kspec, kspec, bspec],
        out_specs=qspec,
        out_shape=jax.ShapeDtypeStruct(shp, BF16),
        compiler_params=_cparams("parallel", "arbitrary"),
    )(nq.reshape(shp), nk.reshape(shp), nv.reshape(shp), bias_full)
    return out.reshape(b, s, NA_WIDTH)


def _na_bias(rpb):
    case = jnp.arange(NA_WIN_ROWS)[:, None]
    w = jnp.arange(NA_WIN_ROWS)[None, :]
    dr_idx = w - case + (NA_WIN_ROWS - 1)
    c = jnp.arange(GRID_W)[:, None]
    kc = jnp.arange(GRID_W)[None, :]
    cs = jnp.clip(c - NA_WIN_COLS // 2, 0, GRID_W - NA_WIN_COLS)
    valid = (kc >= cs) & (kc < cs + NA_WIN_COLS)
    dc_idx = jnp.clip(kc - c + NA_WIN_COLS - 1, 0, 2 * NA_WIN_COLS - 2)
    tab = rpb.astype(F32)[:, dr_idx]
    tab = tab[:, :, :, dc_idx]
    tab = jnp.where(valid[None, None, None], tab * LOG2E, NEG_BIG)
    tab = jnp.transpose(tab, (1, 0, 3, 2, 4))
    return tab.reshape(NA_WIN_ROWS, NA_HEADS // 2, 2 * GRID_W, NA_WIN_ROWS * GRID_W)


def _merge_kernel(x_ref, yd_ref, yn_ref, gate_ref, wbd_ref, wbn_ref, wo_ref, g2_ref, wrh_ref, wrl_ref,
                  x1_ref, h2_ref, aff_ref, *, layer):
    d = x_ref.shape[-1]
    bd = jnp.dot(yd_ref[...], wbd_ref[0], preferred_element_type=F32)
    bn = jnp.dot(yn_ref[...], wbn_ref[0], preferred_element_type=F32)
    merged = gate_ref[:, :d].astype(F32) * bd + gate_ref[:, d:].astype(F32) * bn
    x1 = x_ref[...] + jnp.dot(merged.astype(BF16), wo_ref[0], preferred_element_type=F32)
    x1_ref[...] = x1
    ms = jnp.mean(x1 * x1, axis=-1, keepdims=True)
    t = x1 * lax.rsqrt(ms + EPS) * g2_ref[layer]
    t_hi = t.astype(BF16)
    h2_ref[...] = t_hi
    t_lo = (t - t_hi.astype(F32)).astype(BF16)
    wrh = wrh_ref[layer]
    logits = _nt_dot(wrh, t_hi) + _nt_dot(wrh, t_lo) + _nt_dot(wrl_ref[layer], t_hi)
    mx = jnp.max(logits, axis=0, keepdims=True)
    e = jnp.exp(logits - mx)
    aff_ref[...] = e / jnp.sum(e, axis=0, keepdims=True)


def _merge_out(x2d, yd, yn, gates, wbd, wbn, wo, g2, wrh, wrl, layer, tm):
    n, d = x2d.shape
    ne = wrh.shape[1]
    whole = lambda a: pl.BlockSpec(a.shape, lambda i: (0,) * a.ndim)
    row = lambda w: pl.BlockSpec((tm, w), lambda i: (i, 0))
    return pl.pallas_call(
        functools.partial(_merge_kernel, layer=layer),
        grid=(n // tm,),
        in_specs=[row(d), row(DIFF_WIDTH), row(NA_WIDTH), row(2 * d), _layer_block(wbd, layer),
                  _layer_block(wbn, layer), _layer_block(wo, layer), whole(g2), whole(wrh), whole(wrl)],
        out_specs=[row(d), row(d), pl.BlockSpec((ne, tm), lambda i: (0, i))],
        out_shape=[jax.ShapeDtypeStruct((n, d), F32), jax.ShapeDtypeStruct((n, d), BF16),
                   jax.ShapeDtypeStruct((ne, n), F32)],
        compiler_params=_cparams("parallel"),
    )(x2d, yd, yn, gates, wbd, wbn, wo, g2, wrh, wrl)


def _select_kernel(aff_ref, pos_ref, tb_ref, *, cap):
    ne, nt, _ = aff_ref.shape
    bits = lambda e: pltpu.bitcast(aff_ref[e], jnp.int32)

    def bit_step(i, thr):
        cand = thr | lax.shift_left(jnp.int32(1), 30 - i)
        rows = []
        for e in range(ne):
            ce = cand[e:e + 1]
            ge = (bits(e) >= ce).astype(jnp.int32)
            cnt = jnp.sum(jnp.sum(ge, axis=0, keepdims=True), axis=1, keepdims=True)
            rows.append(jnp.where(cnt >= cap, ce, thr[e:e + 1]))
        return jnp.concatenate(rows, axis=0)

    thr = lax.fori_loop(0, 31, bit_step, jnp.zeros((ne, LANES), jnp.int32))

    li = lax.broadcasted_iota(jnp.int32, (LANES, LANES), 0)
    lj = lax.broadcasted_iota(jnp.int32, (LANES, LANES), 1)
    upper = (li < lj).astype(BF16)
    ones = jnp.ones((LANES, LANES), BF16)
    ti = lax.broadcasted_iota(jnp.int32, (nt, nt), 0)
    tj = lax.broadcasted_iota(jnp.int32, (nt, nt), 1)
    lower = (tj < ti).astype(BF16)

    def prefix(flags):
        fb = flags.astype(BF16)
        within = jnp.dot(fb, upper, preferred_element_type=F32)
        per_tile = jnp.dot(fb, ones, preferred_element_type=F32)
        before = jnp.dot(lower, per_tile.astype(BF16), preferred_element_type=F32)
        return before + within, before

    for e in range(ne):
        be = bits(e)
        te = thr[e:e + 1]
        gt = be > te
        eq = be == te
        n_gt = jnp.sum(jnp.sum(gt.astype(F32), axis=0, keepdims=True), axis=1, keepdims=True)
        eq_rank, _ = prefix(eq)
        sel = gt | (eq & (eq_rank < cap - n_gt))
        pos, before = prefix(sel)
        pos_ref[e] = jnp.where(sel, pos, -1.0).astype(jnp.int32)
        tb_ref[e] = before


def _select(aff_t, cap):
    ne, n = aff_t.shape
    nt = n // LANES
    shp = (ne, nt, LANES)
    spec = pl.BlockSpec(shp, lambda i: (0, 0, 0))
    pos, tb = pl.pallas_call(
        functools.partial(_select_kernel, cap=cap),
        grid=(1,),
        in_specs=[spec],
        out_specs=[spec, spec],
        out_shape=[jax.ShapeDtypeStruct(shp, jnp.int32), jax.ShapeDtypeStruct(shp, F32)],
        compiler_params=_cparams("arbitrary"),
    )(aff_t.reshape(shp))
    return pos.reshape(ne, n), tb[:, :, 0].astype(jnp.int32)


def _dispatch_kernel(tb_ref, nr_ref, h_ref, pos_ref, aff_ref, xe_hbm, gc_hbm, xs_ref, gs_ref, last_ref, pend_ref,
                     sem, *, cap):
    j = pl.program_id(0)
    ne = pos_ref.shape[0]
    tt = h_ref.shape[0]

    @pl.when(j == 0)
    def _():
        xs_ref[...] = jnp.zeros(xs_ref.shape, xs_ref.dtype)
        gs_ref[...] = jnp.zeros(gs_ref.shape, F32)
        for e in range(ne):
            last_ref[e] = 0
            pend_ref[e] = 0
        pad = [(pltpu.make_async_copy(xs_ref.at[e], xe_hbm.at[e, pl.ds(cap + q * DISPATCH_WIN, DISPATCH_WIN)], sem.at[0]),
                pltpu.make_async_copy(gs_ref.at[e], gc_hbm.at[e, pl.ds(cap + q * DISPATCH_WIN, DISPATCH_WIN)], sem.at[1]))
               for e in range(ne) for q in range((xe_hbm.shape[1] - cap) // DISPATCH_WIN)]
        for cx, cg in pad:
            cx.start()
            cg.start()
        for cx, cg in pad:
            cx.wait()
            cg.wait()

    h = h_ref[...]
    slot = lax.broadcasted_iota(jnp.int32, (DISPATCH_WIN, tt), 0)
    row = lax.broadcasted_iota(jnp.int32, (BF16_ROWS, 1), 0)

    def copies(e, start):
        dst = pl.ds(pl.multiple_of(start, BF16_ROWS), DISPATCH_WIN)
        return (pltpu.make_async_copy(xs_ref.at[e], xe_hbm.at[e, dst], sem.at[0]),
                pltpu.make_async_copy(gs_ref.at[e], gc_hbm.at[e, dst], sem.at[1]))

    def drain():
        for e in range(ne):
            @pl.when(pend_ref[e] == 1)
            def _():
                for c in copies(e, 0):
                    c.wait()
                pend_ref[e] = 0

    def one_round(r, carry):
        first = [tb_ref[e, j] for e in range(ne)]
        starts = [lax.shift_left(lax.shift_right_logical(first[e], 4), 4) + r * ROUTE_CHUNK for e in range(ne)]
        live = [(r == 0) | (tb_ref[e, j + 1] > first[e] + r * ROUTE_CHUNK) for e in range(ne)]
        hits = [pos_ref[e:e + 1, :] - starts[e] == slot for e in range(ne)]
        onehot = jnp.concatenate([hit.astype(BF16) for hit in hits], axis=0)
        xs = jnp.dot(onehot, h, preferred_element_type=F32)
        gates = [jnp.sum(jnp.where(hits[e], aff_ref[e:e + 1, :], 0.0), axis=1, keepdims=True) for e in range(ne)]
        drain()
        for e in range(ne):
            @pl.when(live[e])
            def _():
                n_carry = jnp.where(r == 0, first[e] - starts[e], 0)
                off = pl.multiple_of(jnp.where(r == 0, starts[e] - last_ref[e], 0), BF16_ROWS)
                keep = row < n_carry
                lo = e * DISPATCH_WIN
                gs = jnp.broadcast_to(gates[e], (DISPATCH_WIN, LANES))
                hd = BF16_ROWS
                x_head = xs[lo:lo + hd] + jnp.where(keep, xs_ref[e, pl.ds(off, hd)].astype(F32), 0.0)
                g_head = gs[:hd] + jnp.where(keep, gs_ref[e, pl.ds(off, hd)], 0.0)
                xs_ref[e, hd:] = xs[lo + hd:lo + DISPATCH_WIN].astype(xs_ref.dtype)
                gs_ref[e, hd:] = gs[hd:]
                xs_ref[e, :hd] = x_head.astype(xs_ref.dtype)
                gs_ref[e, :hd] = g_head
                last_ref[e] = starts[e]
                for c in copies(e, starts[e]):
                    c.start()
                pend_ref[e] = 1
        return carry

    lax.fori_loop(0, nr_ref[j], one_round, 0)

    @pl.when(j == pl.num_programs(0) - 1)
    def _():
        drain()


def _dispatch(h2, pos, aff_t, tb_tiles, n_rounds, cap):
    n, d = h2.shape
    ne = pos.shape[0]
    tt = ROUTE_TILE
    assert cap % BF16_ROWS == 0
    cap_pad = cap + -(-tt // ROUTE_CHUNK) * DISPATCH_WIN
    grid_spec = pltpu.PrefetchScalarGridSpec(
        num_scalar_prefetch=2,
        grid=(n // tt,),
        in_specs=[pl.BlockSpec((tt, d), lambda j, *_: (j, 0)),
                  pl.BlockSpec((ne, tt), lambda j, *_: (0, j)),
                  pl.BlockSpec((ne, tt), lambda j, *_: (0, j))],
        out_specs=[pl.BlockSpec(memory_space=pl.ANY), pl.BlockSpec(memory_space=pl.ANY)],
        scratch_shapes=[pltpu.VMEM((ne, DISPATCH_WIN, d), BF16), pltpu.VMEM((ne, DISPATCH_WIN, LANES), F32),
                        pltpu.SMEM((ne,), jnp.int32), pltpu.SMEM((ne,), jnp.int32),
                        pltpu.SemaphoreType.DMA((2,))],
    )
    return pl.pallas_call(
        functools.partial(_dispatch_kernel, cap=cap),
        grid_spec=grid_spec,
        out_shape=[jax.ShapeDtypeStruct((ne, cap_pad, d), BF16), jax.ShapeDtypeStruct((ne, cap_pad, LANES), F32)],
        compiler_params=_cparams("arbitrary"),
    )(tb_tiles, n_rounds, h2, pos, aff_t)


def _combine_kernel(tb_ref, nr_ref, x_ref, post_ref, tbt_ref, spread_ref, kcol_ref, ye_hbm, o_ref, ch_ref, sem,
                    *, cap):
    j = pl.program_id(0)
    ne = ye_hbm.shape[0]
    o_ref[...] = x_ref[...]
    spread = spread_ref[...]
    kcol = kcol_ref[...]
    last = cap - COMBINE_CHUNK

    buf = j % 2

    def chunk_copies(tile, r, b):
        def start_of(e):
            s = tb_ref[e, tile] + r * ROUTE_CHUNK
            return jnp.minimum(lax.shift_left(lax.shift_right_logical(s, 4), 4), last)
        return [pltpu.make_async_copy(ye_hbm.at[e, pl.ds(pl.multiple_of(start_of(e), BF16_ROWS), COMBINE_CHUNK)],
                                      ch_ref.at[b, e], sem.at[b]) for e in range(ne)]

    @pl.when(j == 0)
    def _():
        for c in chunk_copies(0, 0, 0):
            c.start()

    @pl.when(j + 1 < pl.num_programs(0))
    def _():
        for c in chunk_copies(j + 1, 0, 1 - buf):
            c.start()

    def one_round(r, carry):
        cps = chunk_copies(j, r, buf)

        @pl.when(r > 0)
        def _():
            for c in cps:
                c.start()
        first = tbt_ref[pl.ds(j, 1), :] + (r * ROUTE_CHUNK).astype(F32)
        sv = jnp.minimum(jnp.floor(first * (1.0 / BF16_ROWS)) * BF16_ROWS, float(last))
        post = post_ref[...]
        mine = (post >= first) & (post < first + ROUTE_CHUNK)
        rel = jnp.where(mine, post - sv, -1.0).astype(BF16)
        rel_wide = jnp.dot(rel, spread, preferred_element_type=F32)
        onehot = (rel_wide == kcol).astype(BF16)
        for c in cps:
            c.wait()
        rows = ch_ref[buf].reshape(ne * COMBINE_CHUNK, ch_ref.shape[-1])
        o_ref[...] += jnp.dot(onehot, rows, preferred_element_type=F32)
        return carry

    lax.fori_loop(0, nr_ref[j], one_round, 0)


def _combine(x1, pos_t, tb_tiles, tb_t, n_rounds, ye, cap):
    n, d = x1.shape
    ne = ye.shape[0]
    tt = ROUTE_TILE
    assert cap >= COMBINE_CHUNK and cap % BF16_ROWS == 0
    col = np.arange(ne * COMBINE_CHUNK)
    spread = jnp.asarray(np.arange(LANES)[:, None] == (col // COMBINE_CHUNK)[None, :], BF16)
    kcol = jnp.asarray((col % COMBINE_CHUNK)[None, :], F32)
    grid_spec = pltpu.PrefetchScalarGridSpec(
        num_scalar_prefetch=2,
        grid=(n // tt,),
        in_specs=[pl.BlockSpec((tt, d), lambda j, *_: (j, 0)),
                  pl.BlockSpec((tt, LANES), lambda j, *_: (j, 0)),
                  pl.BlockSpec(tb_t.shape, lambda j, *_: (0, 0)),
                  pl.BlockSpec(spread.shape, lambda j, *_: (0, 0)),
                  pl.BlockSpec(kcol.shape, lambda j, *_: (0, 0)),
                  pl.BlockSpec(memory_space=pl.ANY)],
        out_specs=pl.BlockSpec((tt, d), lambda j, *_: (j, 0)),
        scratch_shapes=[pltpu.VMEM((2, ne, COMBINE_CHUNK, d), BF16), pltpu.SemaphoreType.DMA((2,))],
    )
    return pl.pallas_call(
        functools.partial(_combine_kernel, cap=cap),
        grid_spec=grid_spec,
        out_shape=jax.ShapeDtypeStruct((n, d), F32),
        compiler_params=_cparams("arbitrary"),
    )(tb_tiles, n_rounds, x1, pos_t, tb_t, spread, kcol, ye)


def _expert_kernel(xe_ref, gate_ref, wg_ref, wu_ref, wd_ref, o_ref, *, ff_chunk):
    x = xe_ref[0]
    ff = wg_ref.shape[-1]
    acc = None
    for c in range(0, ff, ff_chunk):
        w = min(ff_chunk, ff - c)
        g = jnp.dot(x, wg_ref[0, 0, :, c:c + w], preferred_element_type=F32)
        u = jnp.dot(x, wu_ref[0, 0, :, c:c + w], preferred_element_type=F32)
        hmid = (g / (1.0 + jnp.exp(-g)) * u).astype(BF16)
        part = jnp.dot(hmid, wd_ref[0, 0, c:c + w, :], preferred_element_type=F32)
        acc = part if acc is None else acc + part
    gate = gate_ref[0]
    gate = jnp.concatenate([gate] * (acc.shape[1] // LANES), axis=1)
    o_ref[0] = (acc * gate).astype(o_ref.dtype)


def _expert_ffn(xe, gate, wg, wu, wd, layer, cap, tm, ff_chunk):
    ne, _, d = xe.shape
    ff = wg.shape[-1]
    return pl.pallas_call(
        functools.partial(_expert_kernel, ff_chunk=ff_chunk),
        grid=(ne, cap // tm),
        in_specs=[pl.BlockSpec((1, tm, d), lambda e, m: (e, m, 0)),
                  pl.BlockSpec((1, tm, LANES), lambda e, m: (e, m, 0)),
                  pl.BlockSpec((1, 1, d, ff), lambda e, m: (layer, e, 0, 0)),
                  pl.BlockSpec((1, 1, d, ff), lambda e, m: (layer, e, 0, 0)),
                  pl.BlockSpec((1, 1, ff, d), lambda e, m: (layer, e, 0, 0))],
        out_specs=pl.BlockSpec((1, tm, d), lambda e, m: (e, m, 0)),
        out_shape=jax.ShapeDtypeStruct((ne, cap, d), BF16),
        compiler_params=_cparams("parallel", "arbitrary"),
    )(xe, gate, wg, wu, wd)


def _pick_tile(n, pref):
    t = min(n, pref)
    assert n % t == 0
    return t


EXPERT_FF_CHUNK = 4 * MXU_DIM


def kernel(x_prompt, x_sample, norm1_g, w_in, diff_q_norm, diff_k_norm, lambda_q1, lambda_k1, lambda_q2,
           lambda_k2, diff_subln_g, rel_bias, na_q_norm, na_k_norm, na_rpb, w_branch_diff, w_branch_na,
           w_out, norm2_g, w_router, w_expert_gate, w_expert_up, w_expert_down):
    depth, d_model, _ = w_in.shape
    n_experts = w_router.shape[-1]
    ff = w_expert_gate.shape[-1]

    w_in_b = w_in.astype(BF16)
    wbd_b = w_branch_diff.astype(BF16)
    wbn_b = w_branch_na.astype(BF16)
    wo_b = w_out.astype(BF16)
    wg_b = w_expert_gate.astype(BF16)
    wu_b = w_expert_up.astype(BF16)
    wd_b = w_expert_down.astype(BF16)
    wr_t = jnp.swapaxes(w_router.astype(F32), 1, 2)
    wr_hi = wr_t.astype(BF16)
    wr_lo = (wr_t - wr_hi.astype(F32)).astype(BF16)
    g1 = norm1_g.astype(F32)[:, None, :]
    g2 = norm2_g.astype(F32)[:, None, :]
    tile_gain = lambda g, reps, scale: (jnp.tile(g.astype(F32), (1, reps)) * scale)[:, None, :]
    dqg = tile_gain(diff_q_norm, 2 * DIFF_HEADS, DIFF_HEAD_DIM ** -0.5 * LOG2E)
    dkg = tile_gain(diff_k_norm, 2 * DIFF_HEADS, 1.0)
    nqg = tile_gain(na_q_norm, NA_HEADS, NA_HEAD_DIM ** -0.5 * LOG2E)
    nkg = tile_gain(na_k_norm, NA_HEADS, 1.0)
    subln = diff_subln_g.astype(F32)[:, None, :]
    lam_params = jnp.stack([lambda_q1, lambda_k1, lambda_q2, lambda_k2], axis=1).astype(F32)
    seg_id = jnp.arange(MXU_DIM) // DIFF_HEAD_DIM
    seg = jnp.where(seg_id[:, None] == seg_id[None, :], 1.0 / DIFF_HEAD_DIM, 0.0).astype(BF16)
    na_bias = [_na_bias(na_rpb[l]) for l in range(depth)]

    def run(x):
        b, s, _ = x.shape
        n = b * s
        cap = CAPACITY_FACTOR * n // n_experts
        tm = _pick_tile(n, 512)
        t_attn = _pick_tile(s, 512)
        t5_tiles = _t5_bias_tiles(rel_bias, t_attn, min(s, MXU_DIM))
        x2d = x.reshape(n, d_model)
        for l in range(depth):
            lambda_init = 0.8 - 0.6 * math.exp(-0.3 * l)
            dq, dk, dv, nq, nk, nv, gates = _in_proj(x2d, l, g1, w_in_b, seg, dqg, dkg, nqg, nkg, tm)
            r3 = lambda a: a.reshape(b, s, a.shape[-1])
            yd = _diff_attention(r3(dq), r3(dk), r3(dv), t5_tiles, lam_params, subln, l, lambda_init, t_attn)
            yn = _na_attention(r3(nq), r3(nk), r3(nv), na_bias[l], NA_ROWS_PER_STEP)
            x1, h2, aff_t = _merge_out(x2d, yd.reshape(n, DIFF_WIDTH), yn.reshape(n, NA_WIDTH), gates,
                                       wbd_b, wbn_b, wo_b, g2, wr_hi, wr_lo, l, tm)
            pos, tb128 = _select(aff_t, cap)
            per = ROUTE_TILE // LANES
            tb_tiles = jnp.concatenate([tb128[:, ::per], jnp.full((n_experts, 1), cap, jnp.int32)], axis=1)
            counts = tb_tiles[:, 1:] - tb_tiles[:, :-1]
            n_rounds = jnp.maximum(jnp.max(-(-counts // ROUTE_CHUNK), axis=0), 1).astype(jnp.int32)
            tb_t = jnp.pad(tb_tiles.T.astype(F32), ((0, 7), (0, LANES - n_experts)))
            pos_t = jnp.pad(pos.T.astype(F32), ((0, 0), (0, LANES - n_experts)), constant_values=-1.0)
            xe, gate_rows = _dispatch(h2, pos, aff_t, tb_tiles, n_rounds, cap)
            ye = _expert_ffn(xe, gate_rows, wg_b, wu_b, wd_b, l, cap, _pick_tile(cap, 512), EXPERT_FF_CHUNK)
            x2d = _combine(x1, pos_t, tb_tiles, tb_t, n_rounds, ye, cap)
        return x2d.reshape(b, s, d_model)

    return (run(x_prompt), run(x_sample))
```

```python
import functools
import math

import numpy as np
import jax
import jax.numpy as jnp
from jax import lax
from jax.experimental import pallas as pl
from jax.experimental.pallas import tpu as pltpu

F32 = jnp.float32
BF16 = jnp.bfloat16

DIFF_HEADS = 4
DIFF_HEAD_DIM = 64
DIFF_WIDTH = DIFF_HEADS * 2 * DIFF_HEAD_DIM
NA_HEADS = 8
NA_HEAD_DIM = 64
NA_WIDTH = NA_HEADS * NA_HEAD_DIM
GRID_W = 64
NA_WIN_ROWS = 8
NA_WIN_COLS = 16
T5_BUCKETS = 32
T5_MAX_DIST = 128
CAPACITY_FACTOR = 2
EPS = 1e-6

LANES = 128
MXU_DIM = 256
VMEM_LIMIT_BYTES = 56 * 1024 * 1024
NEG_BIG = -1e30
LOG2E = math.log2(math.e)
NA_ROWS_PER_STEP = 8
BF16_ROWS = 16
ROUTE_TILE = 512
ROUTE_CHUNK = 96
COMBINE_CHUNK = ROUTE_CHUNK + BF16_ROWS
DISPATCH_WIN = ROUTE_CHUNK + BF16_ROWS


def _cparams(*sem):
    return pltpu.CompilerParams(dimension_semantics=sem, vmem_limit_bytes=VMEM_LIMIT_BYTES)


def _layer_block(a, layer):
    return pl.BlockSpec((1,) + a.shape[1:], lambda *_: (layer,) + (0,) * (a.ndim - 1))


def _nt_dot(a, b):
    return lax.dot_general(a, b, (((1,), (1,)), ((), ())), preferred_element_type=F32)


def _inproj_kernel(x_ref, g_ref, w_ref, seg_ref, dqg_ref, dkg_ref, nqg_ref, nkg_ref,
                   dq_ref, dk_ref, dv_ref, nq_ref, nk_ref, nv_ref, gate_ref, *, layer, gate_width):
    x = x_ref[...]
    ms = jnp.mean(x * x, axis=-1, keepdims=True)
    h = (x * lax.rsqrt(ms + EPS) * g_ref[layer]).astype(BF16)
    width = DIFF_WIDTH

    def proj(c0):
        return jnp.dot(h, w_ref[0, :, c0:c0 + width], preferred_element_type=F32)

    def head_norm(a, gain):
        sq = (a * a).astype(BF16)
        halves = [jnp.dot(sq[:, c:c + MXU_DIM], seg_ref[...], preferred_element_type=F32)
                  for c in range(0, width, MXU_DIM)]
        ms_seg = jnp.concatenate(halves, axis=1)
        return a * lax.rsqrt(ms_seg + EPS) * gain

    dq_ref[...] = head_norm(proj(0), dqg_ref[layer]).astype(BF16)
    dk_ref[...] = head_norm(proj(width), dkg_ref[layer]).astype(BF16)
    dv_ref[...] = proj(2 * width).astype(BF16)
    nq_ref[...] = head_norm(proj(3 * width), nqg_ref[layer]).astype(BF16)
    nk_ref[...] = head_norm(proj(4 * width), nkg_ref[layer]).astype(BF16)
    nv_ref[...] = proj(5 * width).astype(BF16)
    for c in range(0, gate_width, width):
        a = proj(6 * width + c)
        gate_ref[:, c:c + width] = (1.0 / (1.0 + jnp.exp(-a))).astype(BF16)


def _in_proj(x2d, layer, norm_g, w_in, seg, dqg, dkg, nqg, nkg, tm):
    n, d = x2d.shape
    in_width = w_in.shape[-1]
    gate_width = in_width - 6 * DIFF_WIDTH
    whole = lambda a: pl.BlockSpec(a.shape, lambda i: (0,) * a.ndim)
    row = lambda w: pl.BlockSpec((tm, w), lambda i: (i, 0))
    outs = [jax.ShapeDtypeStruct((n, DIFF_WIDTH), BF16)] * 6 + [jax.ShapeDtypeStruct((n, gate_width), BF16)]
    return pl.pallas_call(
        functools.partial(_inproj_kernel, layer=layer, gate_width=gate_width),
        grid=(n // tm,),
        in_specs=[row(d), whole(norm_g), _layer_block(w_in, layer), whole(seg), whole(dqg), whole(dkg),
                  whole(nqg), whole(nkg)],
        out_specs=[row(DIFF_WIDTH)] * 6 + [row(gate_width)],
        out_shape=outs,
        compiler_params=_cparams("parallel"),
    )(x2d, norm_g, w_in, seg, dqg, dkg, nqg, nkg)


def _diff_attn_kernel(q_ref, k_ref, v_ref, bias_ref, lam_ref, g_ref, o_ref,
                      m1_ref, a1_ref, m2_ref, a2_ref, *, layer, lambda_init):
    qi = pl.program_id(2)
    hw = 2 * DIFF_HEAD_DIM
    tq = q_ref.shape[1]
    seq = k_ref.shape[1]
    sub = bias_ref.shape[-1]

    q = q_ref[0]
    lane = lax.broadcasted_iota(jnp.int32, q.shape, 1)
    zero = jnp.zeros_like(q)
    q1 = jnp.where(lane < DIFF_HEAD_DIM, q, zero)
    q2 = jnp.where(lane >= DIFF_HEAD_DIM, q, zero)

    def update(qm, k, v_ones, bias, m_ref, a_ref, first):
        s = _nt_dot(qm, k) + bias
        m_cur = jnp.max(s, axis=-1, keepdims=True)
        if first:
            m_new = jnp.broadcast_to(m_cur, (tq, LANES))
        else:
            m_prev = m_ref[...]
            m_new = jnp.maximum(m_prev, m_cur)
        p = jnp.exp2(s - jnp.concatenate([m_new] * (sub // LANES), axis=1))
        pv = jnp.dot(p.astype(BF16), v_ones, preferred_element_type=F32)
        if first:
            a_ref[...] = pv
        else:
            alpha = jnp.exp2(m_prev - m_new)
            a_ref[...] = jnp.concatenate([alpha, alpha], axis=1) * a_ref[...] + pv
        m_ref[...] = m_new

    d_lo, _ = _t5_tile_range(tq, sub)
    n_tiles = bias_ref.shape[1]
    for c in range(seq // sub):
        k = k_ref[0, c * sub:(c + 1) * sub]
        v = v_ref[0, c * sub:(c + 1) * sub]
        v_ones = jnp.concatenate([v, jnp.ones_like(v)], axis=1)
        bias = bias_ref[0, jnp.clip(c - qi * (tq // sub) - d_lo, 0, n_tiles - 1)]
        update(q1, k, v_ones, bias, m1_ref, a1_ref, c == 0)
        update(q2, k, v_ones, bias, m2_ref, a2_ref, c == 0)

    lp = lam_ref[layer]
    lam = (jnp.exp(jnp.sum(lp[0:1] * lp[1:2], axis=-1, keepdims=True))
           - jnp.exp(jnp.sum(lp[2:3] * lp[3:4], axis=-1, keepdims=True)) + lambda_init)
    o = a1_ref[:, :hw] / a1_ref[:, hw:] - lam * (a2_ref[:, :hw] / a2_ref[:, hw:])
    ms = jnp.mean(o * o, axis=-1, keepdims=True)
    o = o * lax.rsqrt(ms + EPS) * g_ref[layer] * (1.0 - lambda_init)
    o_ref[0] = o.astype(BF16)


def _diff_attention(dq, dk, dv, bias_tiles, lam_params, subln_g, layer, lambda_init, t):
    b, s, _ = dq.shape
    hw = 2 * DIFF_HEAD_DIM
    qspec = pl.BlockSpec((1, t, hw), lambda bi, h, qi: (bi, qi, h))
    kspec = pl.BlockSpec((1, s, hw), lambda bi, h, qi: (bi, 0, h))
    bspec = pl.BlockSpec((1,) + bias_tiles.shape[1:], lambda bi, h, qi: (h, 0, 0, 0))
    whole = lambda a: pl.BlockSpec(a.shape, lambda bi, h, qi: (0,) * a.ndim)
    return pl.pallas_call(
        functools.partial(_diff_attn_kernel, layer=layer, lambda_init=lambda_init),
        grid=(b, DIFF_HEADS, s // t),
        in_specs=[qspec, kspec, kspec, bspec, whole(lam_params), whole(subln_g)],
        out_specs=qspec,
        out_shape=jax.ShapeDtypeStruct((b, s, DIFF_WIDTH), BF16),
        scratch_shapes=[pltpu.VMEM((t, LANES), F32), pltpu.VMEM((t, 2 * hw), F32),
                        pltpu.VMEM((t, LANES), F32), pltpu.VMEM((t, 2 * hw), F32)],
        compiler_params=_cparams("parallel", "parallel", "arbitrary"),
    )(dq, dk, dv, bias_tiles, lam_params, subln_g)


def _t5_tile_range(tq, sub):
    d_lo = (-(T5_MAX_DIST - 1) - sub) // sub
    d_hi = -(-(tq + T5_MAX_DIST - 1) // sub)
    return d_lo, d_hi


def _t5_bucket(rel):
    nb = T5_BUCKETS // 2
    max_exact = nb // 2
    side = jnp.where(rel > 0, nb, 0)
    n = jnp.abs(rel)
    nf = jnp.maximum(n, 1).astype(F32)
    large = max_exact + (jnp.log(nf / max_exact) / math.log(T5_MAX_DIST / max_exact) * (nb - max_exact)).astype(jnp.int32)
    large = jnp.minimum(large, nb - 1)
    return side + jnp.where(n < max_exact, n, large)


def _t5_bias_tiles(rel_bias, tq, sub):
    d_lo, d_hi = _t5_tile_range(tq, sub)
    i = jnp.arange(tq, dtype=jnp.int32)[:, None]
    j = jnp.arange(sub, dtype=jnp.int32)[None, :]
    d = jnp.arange(d_lo, d_hi + 1, dtype=jnp.int32)[:, None, None]
    rel = d * sub + (j - i)[None]
    tiles = jnp.einsum("dqkb,bh->dqkh", jax.nn.one_hot(_t5_bucket(rel), T5_BUCKETS, dtype=F32),
                       rel_bias.astype(F32), precision=lax.Precision.HIGHEST)
    return jnp.transpose(tiles, (3, 0, 1, 2)) * LOG2E


def _na_kernel(q_ref, k_ref, v_ref, bias_ref, o_ref, *, rows, rows_per_step):
    nkeys = NA_WIN_ROWS * GRID_W
    lane = lax.broadcasted_iota(jnp.int32, (GRID_W, LANES), 1)
    first_half = lane < NA_HEAD_DIM
    for j in range(rows_per_step):
        r = pl.program_id(1) * rows_per_step + j
        rs = jnp.clip(r - NA_WIN_ROWS // 2, 0, rows - NA_WIN_ROWS)
        case = r - rs
        q = q_ref[0, j]
        kwin = k_ref[0, pl.ds(rs, NA_WIN_ROWS)].reshape(nkeys, NA_WIDTH)
        vwin = v_ref[0, pl.ds(rs, NA_WIN_ROWS)].reshape(nkeys, NA_WIDTH)
        for hp in range(NA_HEADS // 2):
            c0 = hp * LANES
            qp = q[:, c0:c0 + LANES]
            zero = jnp.zeros_like(qp)
            q_stack = jnp.concatenate([jnp.where(first_half, qp, zero), jnp.where(first_half, zero, qp)], axis=0)
            vp = vwin[:, c0:c0 + LANES]
            s = _nt_dot(q_stack, kwin[:, c0:c0 + LANES]) + bias_ref[case, hp]
            m = jnp.max(s, axis=-1, keepdims=True)
            p = jnp.exp2(s - m)
            res = jnp.dot(p.astype(BF16), jnp.concatenate([vp, jnp.ones_like(vp)], axis=1),
                          preferred_element_type=F32)
            o = res[:, :LANES] / res[:, LANES:]
            o_ref[0, j, :, c0:c0 + LANES] = jnp.where(first_half, o[:GRID_W], o[GRID_W:]).astype(BF16)


def _na_attention(nq, nk, nv, bias_full, rows_per_step):
    b, s, _ = nq.shape
    rows = s // GRID_W
    assert rows >= NA_WIN_ROWS and rows % rows_per_step == 0
    shp = (b, rows, GRID_W, NA_WIDTH)
    qspec = pl.BlockSpec((1, rows_per_step, GRID_W, NA_WIDTH), lambda bi, r: (bi, r, 0, 0))
    kspec = pl.BlockSpec((1, rows, GRID_W, NA_WIDTH), lambda bi, r: (bi, 0, 0, 0))
    bspec = pl.BlockSpec(bias_full.shape, lambda bi, r: (0, 0, 0, 0))
    out = pl.pallas_call(
        functools.partial(_na_kernel, rows=rows, rows_per_step=rows_per_step),
        grid=(b, rows // rows_per_step),
        in_specs=[qspec, kspec, kspec, bspec],
        out_specs=qspec,
        out_shape=jax.ShapeDtypeStruct(shp, BF16),
        compiler_params=_cparams("parallel", "arbitrary"),
    )(nq.reshape(shp), nk.reshape(shp), nv.reshape(shp), bias_full)
    return out.reshape(b, s, NA_WIDTH)


def _na_bias(rpb):
    case = jnp.arange(NA_WIN_ROWS)[:, None]
    w = jnp.arange(NA_WIN_ROWS)[None, :]
    dr_idx = w - case + (NA_WIN_ROWS - 1)
    c = jnp.arange(GRID_W)[:, None]
    kc = jnp.arange(GRID_W)[None, :]
    cs = jnp.clip(c - NA_WIN_COLS // 2, 0, GRID_W - NA_WIN_COLS)
    valid = (kc >= cs) & (kc < cs + NA_WIN_COLS)
    dc_idx = jnp.clip(kc - c + NA_WIN_COLS - 1, 0, 2 * NA_WIN_COLS - 2)
    tab = rpb.astype(F32)[:, dr_idx]
    tab = tab[:, :, :, dc_idx]
    tab = jnp.where(valid[None, None, None], tab * LOG2E, NEG_BIG)
    tab = jnp.transpose(tab, (1, 0, 3, 2, 4))
    return tab.reshape(NA_WIN_ROWS, NA_HEADS // 2, 2 * GRID_W, NA_WIN_ROWS * GRID_W)


def _merge_kernel(x_ref, yd_ref, yn_ref, gate_ref, wbd_ref, wbn_ref, wo_ref, g2_ref, wrh_ref, wrl_ref,
                  x1_ref, h2_ref, aff_ref, *, layer):
    d = x_ref.shape[-1]
    bd = jnp.dot(yd_ref[...], wbd_ref[0], preferred_element_type=F32)
    bn = jnp.dot(yn_ref[...], wbn_ref[0], preferred_element_type=F32)
    merged = gate_ref[:, :d].astype(F32) * bd + gate_ref[:, d:].astype(F32) * bn
    x1 = x_ref[...] + jnp.dot(merged.astype(BF16), wo_ref[0], preferred_element_type=F32)
    x1_ref[...] = x1
    ms = jnp.mean(x1 * x1, axis=-1, keepdims=True)
    t = x1 * lax.rsqrt(ms + EPS) * g2_ref[layer]
    t_hi = t.astype(BF16)
    h2_ref[...] = t_hi
    t_lo = (t - t_hi.astype(F32)).astype(BF16)
    wrh = wrh_ref[layer]
    logits = _nt_dot(wrh, t_hi) + _nt_dot(wrh, t_lo) + _nt_dot(wrl_ref[layer], t_hi)
    mx = jnp.max(logits, axis=0, keepdims=True)
    e = jnp.exp(logits - mx)
    aff_ref[...] = e / jnp.sum(e, axis=0, keepdims=True)


def _merge_out(x2d, yd, yn, gates, wbd, wbn, wo, g2, wrh, wrl, layer, tm):
    n, d = x2d.shape
    ne = wrh.shape[1]
    whole = lambda a: pl.BlockSpec(a.shape, lambda i: (0,) * a.ndim)
    row = lambda w: pl.BlockSpec((tm, w), lambda i: (i, 0))
    return pl.pallas_call(
        functools.partial(_merge_kernel, layer=layer),
        grid=(n // tm,),
        in_specs=[row(d), row(DIFF_WIDTH), row(NA_WIDTH), row(2 * d), _layer_block(wbd, layer),
                  _layer_block(wbn, layer), _layer_block(wo, layer), whole(g2), whole(wrh), whole(wrl)],
        out_specs=[row(d), row(d), pl.BlockSpec((ne, tm), lambda i: (0, i))],
        out_shape=[jax.ShapeDtypeStruct((n, d), F32), jax.ShapeDtypeStruct((n, d), BF16),
                   jax.ShapeDtypeStruct((ne, n), F32)],
        compiler_params=_cparams("parallel"),
    )(x2d, yd, yn, gates, wbd, wbn, wo, g2, wrh, wrl)


def _select_kernel(aff_ref, pos_ref, tb_ref, *, cap):
    ne, nt, _ = aff_ref.shape
    bits = lambda e: pltpu.bitcast(aff_ref[e], jnp.int32)

    def bit_step(i, thr):
        cand = thr | lax.shift_left(jnp.int32(1), 30 - i)
        rows = []
        for e in range(ne):
            ce = cand[e:e + 1]
            ge = (bits(e) >= ce).astype(jnp.int32)
            cnt = jnp.sum(jnp.sum(ge, axis=0, keepdims=True), axis=1, keepdims=True)
            rows.append(jnp.where(cnt >= cap, ce, thr[e:e + 1]))
        return jnp.concatenate(rows, axis=0)

    thr = lax.fori_loop(0, 31, bit_step, jnp.zeros((ne, LANES), jnp.int32))

    li = lax.broadcasted_iota(jnp.int32, (LANES, LANES), 0)
    lj = lax.broadcasted_iota(jnp.int32, (LANES, LANES), 1)
    upper = (li < lj).astype(BF16)
    ones = jnp.ones((LANES, LANES), BF16)
    ti = lax.broadcasted_iota(jnp.int32, (nt, nt), 0)
    tj = lax.broadcasted_iota(jnp.int32, (nt, nt), 1)
    lower = (tj < ti).astype(BF16)

    def prefix(flags):
        fb = flags.astype(BF16)
        within = jnp.dot(fb, upper, preferred_element_type=F32)
        per_tile = jnp.dot(fb, ones, preferred_element_type=F32)
        before = jnp.dot(lower, per_tile.astype(BF16), preferred_element_type=F32)
        return before + within, before

    for e in range(ne):
        be = bits(e)
        te = thr[e:e + 1]
        gt = be > te
        eq = be == te
        n_gt = jnp.sum(jnp.sum(gt.astype(F32), axis=0, keepdims=True), axis=1, keepdims=True)
        eq_rank, _ = prefix(eq)
        sel = gt | (eq & (eq_rank < cap - n_gt))
        pos, before = prefix(sel)
        pos_ref[e] = jnp.where(sel, pos, -1.0).astype(jnp.int32)
        tb_ref[e] = before


def _select(aff_t, cap):
    ne, n = aff_t.shape
    nt = n // LANES
    shp = (ne, nt, LANES)
    spec = pl.BlockSpec(shp, lambda i: (0, 0, 0))
    pos, tb = pl.pallas_call(
        functools.partial(_select_kernel, cap=cap),
        grid=(1,),
        in_specs=[spec],
        out_specs=[spec, spec],
        out_shape=[jax.ShapeDtypeStruct(shp, jnp.int32), jax.ShapeDtypeStruct(shp, F32)],
        compiler_params=_cparams("arbitrary"),
    )(aff_t.reshape(shp))
    return pos.reshape(ne, n), tb[:, :, 0].astype(jnp.int32)


def _dispatch_kernel(tb_ref, nr_ref, h_ref, pos_ref, aff_ref, xe_hbm, gc_hbm, xs_ref, gs_ref, last_ref, pend_ref,
                     sem, *, cap):
    j = pl.program_id(0)
    ne = pos_ref.shape[0]
    tt = h_ref.shape[0]

    @pl.when(j == 0)
    def _():
        xs_ref[...] = jnp.zeros(xs_ref.shape, xs_ref.dtype)
        gs_ref[...] = jnp.zeros(gs_ref.shape, F32)
        for e in range(ne):
            last_ref[e] = 0
            pend_ref[e] = 0
        pad = [(pltpu.make_async_copy(xs_ref.at[e], xe_hbm.at[e, pl.ds(cap + q * DISPATCH_WIN, DISPATCH_WIN)], sem.at[0, e]),
                pltpu.make_async_copy(gs_ref.at[e], gc_hbm.at[e, pl.ds(cap + q * DISPATCH_WIN, DISPATCH_WIN)], sem.at[1, e]))
               for e in range(ne) for q in range((xe_hbm.shape[1] - cap) // DISPATCH_WIN)]
        for cx, cg in pad:
            cx.start()
            cg.start()
        for cx, cg in pad:
            cx.wait()
            cg.wait()

    h = h_ref[...]
    slot = lax.broadcasted_iota(jnp.int32, (DISPATCH_WIN, tt), 0)
    row = lax.broadcasted_iota(jnp.int32, (BF16_ROWS, 1), 0)

    def copies(e, start):
        dst = pl.ds(pl.multiple_of(start, BF16_ROWS), DISPATCH_WIN)
        return (pltpu.make_async_copy(xs_ref.at[e], xe_hbm.at[e, dst], sem.at[0, e]),
                pltpu.make_async_copy(gs_ref.at[e], gc_hbm.at[e, dst], sem.at[1, e]))

    def drain():
        for e in range(ne):
            @pl.when(pend_ref[e] == 1)
            def _():
                for c in copies(e, 0):
                    c.wait()
                pend_ref[e] = 0

    def emit(e, start, xs_e, gate_col, n_carry, off):
        keep = row < n_carry
        gs = jnp.broadcast_to(gate_col, (DISPATCH_WIN, LANES))
        hd = BF16_ROWS
        x_head = xs_e[:hd] + jnp.where(keep, xs_ref[e, pl.ds(off, hd)].astype(F32), 0.0)
        g_head = gs[:hd] + jnp.where(keep, gs_ref[e, pl.ds(off, hd)], 0.0)
        xs_ref[e, hd:] = xs_e[hd:].astype(xs_ref.dtype)
        gs_ref[e, hd:] = gs[hd:]
        xs_ref[e, :hd] = x_head.astype(xs_ref.dtype)
        gs_ref[e, :hd] = g_head
        last_ref[e] = start
        for c in copies(e, start):
            c.start()
        pend_ref[e] = 1

    def gate_of(hit, e):
        return jnp.sum(jnp.where(hit, aff_ref[e:e + 1, :], 0.0), axis=1, keepdims=True)

    first = [tb_ref[e, j] for e in range(ne)]
    starts = [lax.shift_left(lax.shift_right_logical(first[e], 4), 4) for e in range(ne)]
    hits = [pos_ref[e:e + 1, :] - starts[e] == slot for e in range(ne)]
    xs = jnp.dot(jnp.concatenate([hit.astype(BF16) for hit in hits], axis=0), h, preferred_element_type=F32)
    gates = [gate_of(hits[e], e) for e in range(ne)]
    drain()
    for e in range(ne):
        emit(e, starts[e], xs[e * DISPATCH_WIN:(e + 1) * DISPATCH_WIN], gates[e], first[e] - starts[e],
             pl.multiple_of(starts[e] - last_ref[e], BF16_ROWS))

    def later_round(r, carry):
        for e in range(ne):
            @pl.when(tb_ref[e, j + 1] > first[e] + r * ROUTE_CHUNK)
            def _():
                start = starts[e] + r * ROUTE_CHUNK
                hit = pos_ref[e:e + 1, :] - start == slot
                xs_e = jnp.dot(hit.astype(BF16), h, preferred_element_type=F32)
                gate_col = gate_of(hit, e)

                @pl.when(pend_ref[e] == 1)
                def _():
                    for c in copies(e, 0):
                        c.wait()
                    pend_ref[e] = 0
                emit(e, start, xs_e, gate_col, 0, 0)
        return carry

    lax.fori_loop(1, nr_ref[j], later_round, 0)

    @pl.when(j == pl.num_programs(0) - 1)
    def _():
        drain()


def _dispatch(h2, pos, aff_t, tb_tiles, n_rounds, cap):
    n, d = h2.shape
    ne = pos.shape[0]
    tt = ROUTE_TILE
    assert cap % BF16_ROWS == 0
    cap_pad = cap + -(-tt // ROUTE_CHUNK) * DISPATCH_WIN
    grid_spec = pltpu.PrefetchScalarGridSpec(
        num_scalar_prefetch=2,
        grid=(n // tt,),
        in_specs=[pl.BlockSpec((tt, d), lambda j, *_: (j, 0)),
                  pl.BlockSpec((ne, tt), lambda j, *_: (0, j)),
                  pl.BlockSpec((ne, tt), lambda j, *_: (0, j))],
        out_specs=[pl.BlockSpec(memory_space=pl.ANY), pl.BlockSpec(memory_space=pl.ANY)],
        scratch_shapes=[pltpu.VMEM((ne, DISPATCH_WIN, d), BF16), pltpu.VMEM((ne, DISPATCH_WIN, LANES), F32),
                        pltpu.SMEM((ne,), jnp.int32), pltpu.SMEM((ne,), jnp.int32),
                        pltpu.SemaphoreType.DMA((2, ne))],
    )
    return pl.pallas_call(
        functools.partial(_dispatch_kernel, cap=cap),
        grid_spec=grid_spec,
        out_shape=[jax.ShapeDtypeStruct((ne, cap_pad, d), BF16), jax.ShapeDtypeStruct((ne, cap_pad, LANES), F32)],
        compiler_params=_cparams("arbitrary"),
    )(tb_tiles, n_rounds, h2, pos, aff_t)


def _combine_kernel(tb_ref, nr_ref, x_ref, post_ref, tbt_ref, spread_ref, kcol_ref, ye_hbm, o_ref, ch_ref, one_ref,
                    sem, *, cap):
    j = pl.program_id(0)
    ne = ye_hbm.shape[0]
    spread = spread_ref[...]
    kcol = kcol_ref[...]
    last = cap - COMBINE_CHUNK

    buf = j % 2

    def chunk_copies(tile, r, b):
        def start_of(e):
            s = tb_ref[e, tile] + r * ROUTE_CHUNK
            return jnp.minimum(lax.shift_left(lax.shift_right_logical(s, 4), 4), last)
        return [pltpu.make_async_copy(ye_hbm.at[e, pl.ds(pl.multiple_of(start_of(e), BF16_ROWS), COMBINE_CHUNK)],
                                      ch_ref.at[b, e], sem.at[b]) for e in range(ne)]

    @pl.when(j == 0)
    def _():
        for c in chunk_copies(0, 0, 0):
            c.start()

    @pl.when(j + 1 < pl.num_programs(0))
    def _():
        for c in chunk_copies(j + 1, 0, 1 - buf):
            c.start()

    post = post_ref[...]

    def rel_slots(r):
        first = tbt_ref[pl.ds(j, 1), :] + (r * ROUTE_CHUNK).astype(F32)
        sv = jnp.minimum(jnp.floor(first * (1.0 / BF16_ROWS)) * BF16_ROWS, float(last))
        mine = (post >= first) & (post < first + ROUTE_CHUNK)
        return jnp.where(mine, post - sv, -1.0).astype(BF16)

    rel_wide = jnp.dot(rel_slots(jnp.int32(0)), spread, preferred_element_type=F32)
    onehot = (rel_wide == kcol).astype(BF16)
    for c in chunk_copies(j, 0, buf):
        c.wait()
    rows = ch_ref[buf].reshape(ne * COMBINE_CHUNK, ch_ref.shape[-1])
    o_ref[...] = x_ref[...] + jnp.dot(onehot, rows, preferred_element_type=F32)

    @pl.when(j == 0)
    def _():
        one_ref[...] = jnp.zeros(one_ref.shape, one_ref.dtype)

    lane_row = lax.broadcasted_iota(jnp.int32, (LANES, LANES), 0)
    k_lane = lax.broadcasted_iota(jnp.int32, (1, LANES), 1).astype(F32)

    def later_round(r, carry):
        rel = rel_slots(r)
        for e in range(ne):
            @pl.when(tb_ref[e, j + 1] > tb_ref[e, j] + r * ROUTE_CHUNK)
            def _():
                s = tb_ref[e, j] + r * ROUTE_CHUNK
                start = jnp.minimum(lax.shift_left(lax.shift_right_logical(s, 4), 4), last)
                cp = pltpu.make_async_copy(ye_hbm.at[e, pl.ds(pl.multiple_of(start, BF16_ROWS), COMBINE_CHUNK)],
                                           one_ref.at[pl.ds(0, COMBINE_CHUNK)], sem.at[2])
                cp.start()
                pick = (lane_row == e).astype(BF16)
                rel_e = jnp.dot(rel, pick, preferred_element_type=F32)
                onehot_e = (rel_e == k_lane).astype(BF16)
                cp.wait()
                o_ref[...] += jnp.dot(onehot_e, one_ref[...], preferred_element_type=F32)
        return carry

    lax.fori_loop(1, nr_ref[j], later_round, 0)


def _combine(x1, pos_t, tb_tiles, tb_t, n_rounds, ye, cap):
    n, d = x1.shape
    ne = ye.shape[0]
    tt = ROUTE_TILE
    assert cap >= COMBINE_CHUNK and cap % BF16_ROWS == 0
    col = np.arange(ne * COMBINE_CHUNK)
    spread = jnp.asarray(np.arange(LANES)[:, None] == (col // COMBINE_CHUNK)[None, :], BF16)
    kcol = jnp.asarray((col % COMBINE_CHUNK)[None, :], F32)
    grid_spec = pltpu.PrefetchScalarGridSpec(
        num_scalar_prefetch=2,
        grid=(n // tt,),
        in_specs=[pl.BlockSpec((tt, d), lambda j, *_: (j, 0)),
                  pl.BlockSpec((tt, LANES), lambda j, *_: (j, 0)),
                  pl.BlockSpec(tb_t.shape, lambda j, *_: (0, 0)),
                  pl.BlockSpec(spread.shape, lambda j, *_: (0, 0)),
                  pl.BlockSpec(kcol.shape, lambda j, *_: (0, 0)),
                  pl.BlockSpec(memory_space=pl.ANY)],
        out_specs=pl.BlockSpec((tt, d), lambda j, *_: (j, 0)),
        scratch_shapes=[pltpu.VMEM((2, ne, COMBINE_CHUNK, d), BF16), pltpu.VMEM((LANES, d), BF16),
                        pltpu.SemaphoreType.DMA((3,))],
    )
    return pl.pallas_call(
        functools.partial(_combine_kernel, cap=cap),
        grid_spec=grid_spec,
        out_shape=jax.ShapeDtypeStruct((n, d), F32),
        compiler_params=_cparams("arbitrary"),
    )(tb_tiles, n_rounds, x1, pos_t, tb_t, spread, kcol, ye)


def _expert_kernel(xe_ref, gate_ref, wg_ref, wu_ref, wd_ref, o_ref, *, ff_chunk):
    x = xe_ref[0]
    ff = wg_ref.shape[-1]
    acc = None
    for c in range(0, ff, ff_chunk):
        w = min(ff_chunk, ff - c)
        g = jnp.dot(x, wg_ref[0, 0, :, c:c + w], preferred_element_type=F32)
        u = jnp.dot(x, wu_ref[0, 0, :, c:c + w], preferred_element_type=F32)
        hmid = (g / (1.0 + jnp.exp(-g)) * u).astype(BF16)
        part = jnp.dot(hmid, wd_ref[0, 0, c:c + w, :], preferred_element_type=F32)
        acc = part if acc is None else acc + part
    gate = gate_ref[0]
    gate = jnp.concatenate([gate] * (acc.shape[1] // LANES), axis=1)
    o_ref[0] = (acc * gate).astype(o_ref.dtype)


def _expert_ffn(xe, gate, wg, wu, wd, layer, cap, tm, ff_chunk):
    ne, _, d = xe.shape
    ff = wg.shape[-1]
    return pl.pallas_call(
        functools.partial(_expert_kernel, ff_chunk=ff_chunk),
        grid=(ne, cap // tm),
        in_specs=[pl.BlockSpec((1, tm, d), lambda e, m: (e, m, 0)),
                  pl.BlockSpec((1, tm, LANES), lambda e, m: (e, m, 0)),
                  pl.BlockSpec((1, 1, d, ff), lambda e, m: (layer, e, 0, 0)),
                  pl.BlockSpec((1, 1, d, ff), lambda e, m: (layer, e, 0, 0)),
                  pl.BlockSpec((1, 1, ff, d), lambda e, m: (layer, e, 0, 0))],
        out_specs=pl.BlockSpec((1, tm, d), lambda e, m: (e, m, 0)),
        out_shape=jax.ShapeDtypeStruct((ne, cap, d), BF16),
        compiler_params=_cparams("parallel", "arbitrary"),
    )(xe, gate, wg, wu, wd)


def _pick_tile(n, pref):
    t = min(n, pref)
    assert n % t == 0
    return t


EXPERT_FF_CHUNK = 4 * MXU_DIM


def kernel(x_prompt, x_sample, norm1_g, w_in, diff_q_norm, diff_k_norm, lambda_q1, lambda_k1, lambda_q2,
           lambda_k2, diff_subln_g, rel_bias, na_q_norm, na_k_norm, na_rpb, w_branch_diff, w_branch_na,
           w_out, norm2_g, w_router, w_expert_gate, w_expert_up, w_expert_down):
    depth, d_model, _ = w_in.shape
    n_experts = w_router.shape[-1]
    ff = w_expert_gate.shape[-1]

    w_in_b = w_in.astype(BF16)
    wbd_b = w_branch_diff.astype(BF16)
    wbn_b = w_branch_na.astype(BF16)
    wo_b = w_out.astype(BF16)
    wg_b = w_expert_gate.astype(BF16)
    wu_b = w_expert_up.astype(BF16)
    wd_b = w_expert_down.astype(BF16)
    wr_t = jnp.swapaxes(w_router.astype(F32), 1, 2)
    wr_hi = wr_t.astype(BF16)
    wr_lo = (wr_t - wr_hi.astype(F32)).astype(BF16)
    g1 = norm1_g.astype(F32)[:, None, :]
    g2 = norm2_g.astype(F32)[:, None, :]
    tile_gain = lambda g, reps, scale: (jnp.tile(g.astype(F32), (1, reps)) * scale)[:, None, :]
    dqg = tile_gain(diff_q_norm, 2 * DIFF_HEADS, DIFF_HEAD_DIM ** -0.5 * LOG2E)
    dkg = tile_gain(diff_k_norm, 2 * DIFF_HEADS, 1.0)
    nqg = tile_gain(na_q_norm, NA_HEADS, NA_HEAD_DIM ** -0.5 * LOG2E)
    nkg = tile_gain(na_k_norm, NA_HEADS, 1.0)
    subln = diff_subln_g.astype(F32)[:, None, :]
    lam_params = jnp.stack([lambda_q1, lambda_k1, lambda_q2, lambda_k2], axis=1).astype(F32)
    seg_id = jnp.arange(MXU_DIM) // DIFF_HEAD_DIM
    seg = jnp.where(seg_id[:, None] == seg_id[None, :], 1.0 / DIFF_HEAD_DIM, 0.0).astype(BF16)
    na_bias = [_na_bias(na_rpb[l]) for l in range(depth)]

    def run(x):
        b, s, _ = x.shape
        n = b * s
        cap = CAPACITY_FACTOR * n // n_experts
        tm = _pick_tile(n, 512)
        t_attn = _pick_tile(s, 512)
        t5_tiles = _t5_bias_tiles(rel_bias, t_attn, min(s, MXU_DIM))
        x2d = x.reshape(n, d_model)
        for l in range(depth):
            lambda_init = 0.8 - 0.6 * math.exp(-0.3 * l)
            dq, dk, dv, nq, nk, nv, gates = _in_proj(x2d, l, g1, w_in_b, seg, dqg, dkg, nqg, nkg, tm)
            r3 = lambda a: a.reshape(b, s, a.shape[-1])
            yd = _diff_attention(r3(dq), r3(dk), r3(dv), t5_tiles, lam_params, subln, l, lambda_init, t_attn)
            yn = _na_attention(r3(nq), r3(nk), r3(nv), na_bias[l], NA_ROWS_PER_STEP)
            x1, h2, aff_t = _merge_out(x2d, yd.reshape(n, DIFF_WIDTH), yn.reshape(n, NA_WIDTH), gates,
                                       wbd_b, wbn_b, wo_b, g2, wr_hi, wr_lo, l, tm)
            pos, tb128 = _select(aff_t, cap)
            per = ROUTE_TILE // LANES
            tb_tiles = jnp.concatenate([tb128[:, ::per], jnp.full((n_experts, 1), cap, jnp.int32)], axis=1)
            counts = tb_tiles[:, 1:] - tb_tiles[:, :-1]
            n_rounds = jnp.maximum(jnp.max(-(-counts // ROUTE_CHUNK), axis=0), 1).astype(jnp.int32)
            tb_t = jnp.pad(tb_tiles.T.astype(F32), ((0, 7), (0, LANES - n_experts)))
            pos_t = jnp.pad(pos.T.astype(F32), ((0, 0), (0, LANES - n_experts)), constant_values=-1.0)
            xe, gate_rows = _dispatch(h2, pos, aff_t, tb_tiles, n_rounds, cap)
            ye = _expert_ffn(xe, gate_rows, wg_b, wu_b, wd_b, l, cap, _pick_tile(cap, 512), EXPERT_FF_CHUNK)
            x2d = _combine(x1, pos_t, tb_tiles, tb_t, n_rounds, ye, cap)
        return x2d.reshape(b, s, d_model)

    return (run(x_prompt), run(x_sample))
```

```python
import functools
import math

import numpy as np
import jax
import jax.numpy as jnp
from jax import lax
from jax.experimental import pallas as pl
from jax.experimental.pallas import tpu as pltpu

F32 = jnp.float32
BF16 = jnp.bfloat16

DIFF_HEADS = 4
DIFF_HEAD_DIM = 64
DIFF_WIDTH = DIFF_HEADS * 2 * DIFF_HEAD_DIM
NA_HEADS = 8
NA_HEAD_DIM = 64
NA_WIDTH = NA_HEADS * NA_HEAD_DIM
GRID_W = 64
NA_WIN_ROWS = 8
NA_WIN_COLS = 16
T5_BUCKETS = 32
T5_MAX_DIST = 128
CAPACITY_FACTOR = 2
EPS = 1e-6

LANES = 128
MXU_DIM = 256
VMEM_LIMIT_BYTES = 56 * 1024 * 1024
NEG_BIG = -1e30
LOG2E = math.log2(math.e)
NA_ROWS_PER_STEP = 8
BF16_ROWS = 16
ROUTE_TILE = 512
ROUTE_CHUNK = 96
COMBINE_CHUNK = ROUTE_CHUNK + BF16_ROWS
DISPATCH_WIN = ROUTE_CHUNK + BF16_ROWS


def _cparams(*sem):
    return pltpu.CompilerParams(dimension_semantics=sem, vmem_limit_bytes=VMEM_LIMIT_BYTES)


def _layer_block(a, layer):
    return pl.BlockSpec((1,) + a.shape[1:], lambda *_: (layer,) + (0,) * (a.ndim - 1))


def _nt_dot(a, b):
    return lax.dot_general(a, b, (((1,), (1,)), ((), ())), preferred_element_type=F32)


def _inproj_kernel(x_ref, g_ref, w_ref, seg_ref, dqg_ref, dkg_ref, nqg_ref, nkg_ref,
                   dq_ref, dk_ref, dv_ref, nq_ref, nk_ref, nv_ref, gate_ref, *, layer, gate_width):
    x = x_ref[...]
    ms = jnp.mean(x * x, axis=-1, keepdims=True)
    h = (x * lax.rsqrt(ms + EPS) * g_ref[layer]).astype(BF16)
    width = DIFF_WIDTH

    def proj(c0):
        return jnp.dot(h, w_ref[0, :, c0:c0 + width], preferred_element_type=F32)

    def head_norm(a, gain):
        sq = (a * a).astype(BF16)
        halves = [jnp.dot(sq[:, c:c + MXU_DIM], seg_ref[...], preferred_element_type=F32)
                  for c in range(0, width, MXU_DIM)]
        ms_seg = jnp.concatenate(halves, axis=1)
        return a * lax.rsqrt(ms_seg + EPS) * gain

    dq_ref[...] = head_norm(proj(0), dqg_ref[layer]).astype(BF16)
    dk_ref[...] = head_norm(proj(width), dkg_ref[layer]).astype(BF16)
    dv_ref[...] = proj(2 * width).astype(BF16)
    nq_ref[...] = head_norm(proj(3 * width), nqg_ref[layer]).astype(BF16)
    nk_ref[...] = head_norm(proj(4 * width), nkg_ref[layer]).astype(BF16)
    nv_ref[...] = proj(5 * width).astype(BF16)
    for c in range(0, gate_width, width):
        a = proj(6 * width + c)
        gate_ref[:, c:c + width] = (1.0 / (1.0 + jnp.exp(-a))).astype(BF16)


def _in_proj(x2d, layer, norm_g, w_in, seg, dqg, dkg, nqg, nkg, tm):
    n, d = x2d.shape
    in_width = w_in.shape[-1]
    gate_width = in_width - 6 * DIFF_WIDTH
    whole = lambda a: pl.BlockSpec(a.shape, lambda i: (0,) * a.ndim)
    row = lambda w: pl.BlockSpec((tm, w), lambda i: (i, 0))
    outs = [jax.ShapeDtypeStruct((n, DIFF_WIDTH), BF16)] * 6 + [jax.ShapeDtypeStruct((n, gate_width), BF16)]
    return pl.pallas_call(
        functools.partial(_inproj_kernel, layer=layer, gate_width=gate_width),
        grid=(n // tm,),
        in_specs=[row(d), whole(norm_g), _layer_block(w_in, layer), whole(seg), whole(dqg), whole(dkg),
                  whole(nqg), whole(nkg)],
        out_specs=[row(DIFF_WIDTH)] * 6 + [row(gate_width)],
        out_shape=outs,
        compiler_params=_cparams("parallel"),
    )(x2d, norm_g, w_in, seg, dqg, dkg, nqg, nkg)


def _diff_attn_kernel(q_ref, k_ref, v_ref, bias_ref, lam_ref, g_ref, o_ref,
                      m1_ref, a1_ref, m2_ref, a2_ref, *, layer, lambda_init):
    qi = pl.program_id(2)
    hw = 2 * DIFF_HEAD_DIM
    tq = q_ref.shape[1]
    seq = k_ref.shape[1]
    sub = bias_ref.shape[-1]

    q = q_ref[0]
    lane = lax.broadcasted_iota(jnp.int32, q.shape, 1)
    zero = jnp.zeros_like(q)
    q1 = jnp.where(lane < DIFF_HEAD_DIM, q, zero)
    q2 = jnp.where(lane >= DIFF_HEAD_DIM, q, zero)

    def update(qm, k, v_ones, bias, m_ref, a_ref, first):
        s = _nt_dot(qm, k) + bias
        m_cur = jnp.max(s, axis=-1, keepdims=True)
        if first:
            m_new = jnp.broadcast_to(m_cur, (tq, LANES))
        else:
            m_prev = m_ref[...]
            m_new = jnp.maximum(m_prev, m_cur)
        p = jnp.exp2(s - jnp.concatenate([m_new] * (sub // LANES), axis=1))
        pv = jnp.dot(p.astype(BF16), v_ones, preferred_element_type=F32)
        if first:
            a_ref[...] = pv
        else:
            alpha = jnp.exp2(m_prev - m_new)
            a_ref[...] = jnp.concatenate([alpha, alpha], axis=1) * a_ref[...] + pv
        m_ref[...] = m_new

    d_lo, _ = _t5_tile_range(tq, sub)
    n_tiles = bias_ref.shape[1]
    for c in range(seq // sub):
        k = k_ref[0, c * sub:(c + 1) * sub]
        v = v_ref[0, c * sub:(c + 1) * sub]
        v_ones = jnp.concatenate([v, jnp.ones_like(v)], axis=1)
        bias = bias_ref[0, jnp.clip(c - qi * (tq // sub) - d_lo, 0, n_tiles - 1)]
        update(q1, k, v_ones, bias, m1_ref, a1_ref, c == 0)
        update(q2, k, v_ones, bias, m2_ref, a2_ref, c == 0)

    lp = lam_ref[layer]
    lam = (jnp.exp(jnp.sum(lp[0:1] * lp[1:2], axis=-1, keepdims=True))
           - jnp.exp(jnp.sum(lp[2:3] * lp[3:4], axis=-1, keepdims=True)) + lambda_init)
    o = a1_ref[:, :hw] / a1_ref[:, hw:] - lam * (a2_ref[:, :hw] / a2_ref[:, hw:])
    ms = jnp.mean(o * o, axis=-1, keepdims=True)
    o = o * lax.rsqrt(ms + EPS) * g_ref[layer] * (1.0 - lambda_init)
    o_ref[0] = o.astype(BF16)


def _diff_attention(dq, dk, dv, bias_tiles, lam_params, subln_g, layer, lambda_init, t):
    b, s, _ = dq.shape
    hw = 2 * DIFF_HEAD_DIM
    qspec = pl.BlockSpec((1, t, hw), lambda bi, h, qi: (bi, qi, h))
    kspec = pl.BlockSpec((1, s, hw), lambda bi, h, qi: (bi, 0, h))
    bspec = pl.BlockSpec((1,) + bias_tiles.shape[1:], lambda bi, h, qi: (h, 0, 0, 0))
    whole = lambda a: pl.BlockSpec(a.shape, lambda bi, h, qi: (0,) * a.ndim)
    return pl.pallas_call(
        functools.partial(_diff_attn_kernel, layer=layer, lambda_init=lambda_init),
        grid=(b, DIFF_HEADS, s // t),
        in_specs=[qspec, kspec, kspec, bspec, whole(lam_params), whole(subln_g)],
        out_specs=qspec,
        out_shape=jax.ShapeDtypeStruct((b, s, DIFF_WIDTH), BF16),
        scratch_shapes=[pltpu.VMEM((t, LANES), F32), pltpu.VMEM((t, 2 * hw), F32),
                        pltpu.VMEM((t, LANES), F32), pltpu.VMEM((t, 2 * hw), F32)],
        compiler_params=_cparams("parallel", "parallel", "arbitrary"),
    )(dq, dk, dv, bias_tiles, lam_params, subln_g)


def _t5_tile_range(tq, sub):
    d_lo = (-(T5_MAX_DIST - 1) - sub) // sub
    d_hi = -(-(tq + T5_MAX_DIST - 1) // sub)
    return d_lo, d_hi


def _t5_bucket(rel):
    nb = T5_BUCKETS // 2
    max_exact = nb // 2
    side = jnp.where(rel > 0, nb, 0)
    n = jnp.abs(rel)
    nf = jnp.maximum(n, 1).astype(F32)
    large = max_exact + (jnp.log(nf / max_exact) / math.log(T5_MAX_DIST / max_exact) * (nb - max_exact)).astype(jnp.int32)
    large = jnp.minimum(large, nb - 1)
    return side + jnp.where(n < max_exact, n, large)


def _t5_bias_tiles(rel_bias, tq, sub):
    d_lo, d_hi = _t5_tile_range(tq, sub)
    i = jnp.arange(tq, dtype=jnp.int32)[:, None]
    j = jnp.arange(sub, dtype=jnp.int32)[None, :]
    d = jnp.arange(d_lo, d_hi + 1, dtype=jnp.int32)[:, None, None]
    rel = d * sub + (j - i)[None]
    tiles = jnp.einsum("dqkb,bh->dqkh", jax.nn.one_hot(_t5_bucket(rel), T5_BUCKETS, dtype=F32),
                       rel_bias.astype(F32), precision=lax.Precision.HIGHEST)
    return jnp.transpose(tiles, (3, 0, 1, 2)) * LOG2E


def _na_kernel(q_ref, k_ref, v_ref, bias_ref, o_ref, *, rows, rows_per_step):
    nkeys = NA_WIN_ROWS * GRID_W
    lane = lax.broadcasted_iota(jnp.int32, (GRID_W, LANES), 1)
    first_half = lane < NA_HEAD_DIM
    for j in range(rows_per_step):
        r = pl.program_id(1) * rows_per_step + j
        rs = jnp.clip(r - NA_WIN_ROWS // 2, 0, rows - NA_WIN_ROWS)
        case = r - rs
        q = q_ref[0, j]
        kwin = k_ref[0, pl.ds(rs, NA_WIN_ROWS)].reshape(nkeys, NA_WIDTH)
        vwin = v_ref[0, pl.ds(rs, NA_WIN_ROWS)].reshape(nkeys, NA_WIDTH)
        for hp in range(NA_HEADS // 2):
            c0 = hp * LANES
            qp = q[:, c0:c0 + LANES]
            zero = jnp.zeros_like(qp)
            q_stack = jnp.concatenate([jnp.where(first_half, qp, zero), jnp.where(first_half, zero, qp)], axis=0)
            vp = vwin[:, c0:c0 + LANES]
            s = _nt_dot(q_stack, kwin[:, c0:c0 + LANES]) + bias_ref[case, hp]
            m = jnp.max(s, axis=-1, keepdims=True)
            p = jnp.exp2(s - m)
            res = jnp.dot(p.astype(BF16), jnp.concatenate([vp, jnp.ones_like(vp)], axis=1),
                          preferred_element_type=F32)
            o = res[:, :LANES] / res[:, LANES:]
            o_ref[0, j, :, c0:c0 + LANES] = jnp.where(first_half, o[:GRID_W], o[GRID_W:]).astype(BF16)


def _na_attention(nq, nk, nv, bias_full, rows_per_step):
    b, s, _ = nq.shape
    rows = s // GRID_W
    assert rows >= NA_WIN_ROWS and rows % rows_per_step == 0
    shp = (b, rows, GRID_W, NA_WIDTH)
    qspec = pl.BlockSpec((1, rows_per_step, GRID_W, NA_WIDTH), lambda bi, r: (bi, r, 0, 0))
    kspec = pl.BlockSpec((1, rows, GRID_W, NA_WIDTH), lambda bi, r: (bi, 0, 0, 0))
    bspec = pl.BlockSpec(bias_full.shape, lambda bi, r: (0, 0, 0, 0))
    out = pl.pallas_call(
        functools.partial(_na_kernel, rows=rows, rows_per_step=rows_per_step),
        grid=(b, rows // rows_per_step),
        in_specs=[qspec, kspec, kspec, bspec],
        out_specs=qspec,
        out_shape=jax.ShapeDtypeStruct(shp, BF16),
        compiler_params=_cparams("parallel", "arbitrary"),
    )(nq.reshape(shp), nk.reshape(shp), nv.reshape(shp), bias_full)
    return out.reshape(b, s, NA_WIDTH)


def _na_bias(rpb):
    case = jnp.arange(NA_WIN_ROWS)[:, None]
    w = jnp.arange(NA_WIN_ROWS)[None, :]
    dr_idx = w - case + (NA_WIN_ROWS - 1)
    c = jnp.arange(GRID_W)[:, None]
    kc = jnp.arange(GRID_W)[None, :]
    cs = jnp.clip(c - NA_WIN_COLS // 2, 0, GRID_W - NA_WIN_COLS)
    valid = (kc >= cs) & (kc < cs + NA_WIN_COLS)
    dc_idx = jnp.clip(kc - c + NA_WIN_COLS - 1, 0, 2 * NA_WIN_COLS - 2)
    tab = rpb.astype(F32)[:, dr_idx]
    tab = jnp.einsum("hawd,ckd->hawck", tab, jax.nn.one_hot(dc_idx, 2 * NA_WIN_COLS - 1, dtype=F32),
                     precision=lax.Precision.HIGHEST)
    tab = jnp.where(valid[None, None, None], tab * LOG2E, NEG_BIG)
    tab = jnp.transpose(tab, (1, 0, 3, 2, 4))
    return tab.reshape(NA_WIN_ROWS, NA_HEADS // 2, 2 * GRID_W, NA_WIN_ROWS * GRID_W)


def _merge_kernel(x_ref, yd_ref, yn_ref, gate_ref, wbd_ref, wbn_ref, wo_ref, g2_ref, wrh_ref, wrl_ref,
                  x1_ref, h2_ref, aff_ref, *, layer):
    d = x_ref.shape[-1]
    bd = jnp.dot(yd_ref[...], wbd_ref[0], preferred_element_type=F32)
    bn = jnp.dot(yn_ref[...], wbn_ref[0], preferred_element_type=F32)
    merged = gate_ref[:, :d].astype(F32) * bd + gate_ref[:, d:].astype(F32) * bn
    x1 = x_ref[...] + jnp.dot(merged.astype(BF16), wo_ref[0], preferred_element_type=F32)
    x1_ref[...] = x1
    ms = jnp.mean(x1 * x1, axis=-1, keepdims=True)
    t = x1 * lax.rsqrt(ms + EPS) * g2_ref[layer]
    t_hi = t.astype(BF16)
    h2_ref[...] = t_hi
    t_lo = (t - t_hi.astype(F32)).astype(BF16)
    wrh = wrh_ref[layer]
    logits = _nt_dot(wrh, t_hi) + _nt_dot(wrh, t_lo) + _nt_dot(wrl_ref[layer], t_hi)
    mx = jnp.max(logits, axis=0, keepdims=True)
    e = jnp.exp(logits - mx)
    aff_ref[...] = e / jnp.sum(e, axis=0, keepdims=True)


def _merge_out(x2d, yd, yn, gates, wbd, wbn, wo, g2, wrh, wrl, layer, tm):
    n, d = x2d.shape
    ne = wrh.shape[1]
    whole = lambda a: pl.BlockSpec(a.shape, lambda i: (0,) * a.ndim)
    row = lambda w: pl.BlockSpec((tm, w), lambda i: (i, 0))
    return pl.pallas_call(
        functools.partial(_merge_kernel, layer=layer),
        grid=(n // tm,),
        in_specs=[row(d), row(DIFF_WIDTH), row(NA_WIDTH), row(2 * d), _layer_block(wbd, layer),
                  _layer_block(wbn, layer), _layer_block(wo, layer), whole(g2), whole(wrh), whole(wrl)],
        out_specs=[row(d), row(d), pl.BlockSpec((ne, tm), lambda i: (0, i))],
        out_shape=[jax.ShapeDtypeStruct((n, d), F32), jax.ShapeDtypeStruct((n, d), BF16),
                   jax.ShapeDtypeStruct((ne, n), F32)],
        compiler_params=_cparams("parallel"),
    )(x2d, yd, yn, gates, wbd, wbn, wo, g2, wrh, wrl)


def _select_kernel(aff_ref, pos_ref, tb_ref, *, cap):
    ne, nt, _ = aff_ref.shape
    bits = lambda e: pltpu.bitcast(aff_ref[e], jnp.int32)

    def bit_step(i, thr):
        cand = thr | lax.shift_left(jnp.int32(1), 30 - i)
        rows = []
        for e in range(ne):
            ce = cand[e:e + 1]
            ge = (bits(e) >= ce).astype(jnp.int32)
            cnt = jnp.sum(jnp.sum(ge, axis=0, keepdims=True), axis=1, keepdims=True)
            rows.append(jnp.where(cnt >= cap, ce, thr[e:e + 1]))
        return jnp.concatenate(rows, axis=0)

    thr = lax.fori_loop(0, 31, bit_step, jnp.zeros((ne, LANES), jnp.int32))

    li = lax.broadcasted_iota(jnp.int32, (LANES, LANES), 0)
    lj = lax.broadcasted_iota(jnp.int32, (LANES, LANES), 1)
    upper = (li < lj).astype(BF16)
    ones = jnp.ones((LANES, LANES), BF16)
    ti = lax.broadcasted_iota(jnp.int32, (nt, nt), 0)
    tj = lax.broadcasted_iota(jnp.int32, (nt, nt), 1)
    lower = (tj < ti).astype(BF16)

    def prefix(flags):
        fb = flags.astype(BF16)
        within = jnp.dot(fb, upper, preferred_element_type=F32)
        per_tile = jnp.dot(fb, ones, preferred_element_type=F32)
        before = jnp.dot(lower, per_tile.astype(BF16), preferred_element_type=F32)
        return before + within, before

    for e in range(ne):
        be = bits(e)
        te = thr[e:e + 1]
        gt = be > te
        eq = be == te
        n_gt = jnp.sum(jnp.sum(gt.astype(F32), axis=0, keepdims=True), axis=1, keepdims=True)
        eq_rank, _ = prefix(eq)
        sel = gt | (eq & (eq_rank < cap - n_gt))
        pos, before = prefix(sel)
        pos_ref[e] = jnp.where(sel, pos, -1.0).astype(jnp.int32)
        tb_ref[e] = before


def _select(aff_t, cap):
    ne, n = aff_t.shape
    nt = n // LANES
    shp = (ne, nt, LANES)
    spec = pl.BlockSpec(shp, lambda i: (0, 0, 0))
    pos, tb = pl.pallas_call(
        functools.partial(_select_kernel, cap=cap),
        grid=(1,),
        in_specs=[spec],
        out_specs=[spec, spec],
        out_shape=[jax.ShapeDtypeStruct(shp, jnp.int32), jax.ShapeDtypeStruct(shp, F32)],
        compiler_params=_cparams("arbitrary"),
    )(aff_t.reshape(shp))
    return pos.reshape(ne, n), tb[:, :, 0].astype(jnp.int32)


def _dispatch_kernel(tb_ref, nr_ref, h_ref, pos_ref, aff_ref, xe_hbm, gc_hbm, xs_ref, gs_ref, last_ref, pend_ref,
                     sem, *, cap):
    j = pl.program_id(0)
    ne = pos_ref.shape[0]
    tt = h_ref.shape[0]

    @pl.when(j == 0)
    def _():
        xs_ref[...] = jnp.zeros(xs_ref.shape, xs_ref.dtype)
        gs_ref[...] = jnp.zeros(gs_ref.shape, F32)
        for e in range(ne):
            last_ref[e] = 0
            pend_ref[e] = 0
        pad = [(pltpu.make_async_copy(xs_ref.at[e], xe_hbm.at[e, pl.ds(cap + q * DISPATCH_WIN, DISPATCH_WIN)], sem.at[0, e]),
                pltpu.make_async_copy(gs_ref.at[e], gc_hbm.at[e, pl.ds(cap + q * DISPATCH_WIN, DISPATCH_WIN)], sem.at[1, e]))
               for e in range(ne) for q in range((xe_hbm.shape[1] - cap) // DISPATCH_WIN)]
        for cx, cg in pad:
            cx.start()
            cg.start()
        for cx, cg in pad:
            cx.wait()
            cg.wait()

    h = h_ref[...]
    slot = lax.broadcasted_iota(jnp.int32, (DISPATCH_WIN, tt), 0)
    row = lax.broadcasted_iota(jnp.int32, (BF16_ROWS, 1), 0)

    def copies(e, start):
        dst = pl.ds(pl.multiple_of(start, BF16_ROWS), DISPATCH_WIN)
        return (pltpu.make_async_copy(xs_ref.at[e], xe_hbm.at[e, dst], sem.at[0, e]),
                pltpu.make_async_copy(gs_ref.at[e], gc_hbm.at[e, dst], sem.at[1, e]))

    def drain():
        for e in range(ne):
            @pl.when(pend_ref[e] == 1)
            def _():
                for c in copies(e, 0):
                    c.wait()
                pend_ref[e] = 0

    def emit(e, start, xs_e, gate_col, n_carry, off):
        keep = row < n_carry
        gs = jnp.broadcast_to(gate_col, (DISPATCH_WIN, LANES))
        hd = BF16_ROWS
        x_head = xs_e[:hd] + jnp.where(keep, xs_ref[e, pl.ds(off, hd)].astype(F32), 0.0)
        g_head = gs[:hd] + jnp.where(keep, gs_ref[e, pl.ds(off, hd)], 0.0)
        xs_ref[e, hd:] = xs_e[hd:].astype(xs_ref.dtype)
        gs_ref[e, hd:] = gs[hd:]
        xs_ref[e, :hd] = x_head.astype(xs_ref.dtype)
        gs_ref[e, :hd] = g_head
        last_ref[e] = start
        for c in copies(e, start):
            c.start()
        pend_ref[e] = 1

    def gate_of(hit, e):
        return jnp.sum(jnp.where(hit, aff_ref[e:e + 1, :], 0.0), axis=1, keepdims=True)

    first = [tb_ref[e, j] for e in range(ne)]
    starts = [lax.shift_left(lax.shift_right_logical(first[e], 4), 4) for e in range(ne)]
    hits = [pos_ref[e:e + 1, :] - starts[e] == slot for e in range(ne)]
    xs = jnp.dot(jnp.concatenate([hit.astype(BF16) for hit in hits], axis=0), h, preferred_element_type=F32)
    gates = [gate_of(hits[e], e) for e in range(ne)]
    drain()
    for e in range(ne):
        emit(e, starts[e], xs[e * DISPATCH_WIN:(e + 1) * DISPATCH_WIN], gates[e], first[e] - starts[e],
             pl.multiple_of(starts[e] - last_ref[e], BF16_ROWS))

    def later_round(r, carry):
        for e in range(ne):
            @pl.when(tb_ref[e, j + 1] > first[e] + r * ROUTE_CHUNK)
            def _():
                start = starts[e] + r * ROUTE_CHUNK
                hit = pos_ref[e:e + 1, :] - start == slot
                xs_e = jnp.dot(hit.astype(BF16), h, preferred_element_type=F32)
                gate_col = gate_of(hit, e)

                @pl.when(pend_ref[e] == 1)
                def _():
                    for c in copies(e, 0):
                        c.wait()
                    pend_ref[e] = 0
                emit(e, start, xs_e, gate_col, 0, 0)
        return carry

    lax.fori_loop(1, nr_ref[j], later_round, 0)

    @pl.when(j == pl.num_programs(0) - 1)
    def _():
        drain()


def _dispatch(h2, pos, aff_t, tb_tiles, n_rounds, cap):
    n, d = h2.shape
    ne = pos.shape[0]
    tt = ROUTE_TILE
    assert cap % BF16_ROWS == 0
    cap_pad = cap + -(-tt // ROUTE_CHUNK) * DISPATCH_WIN
    grid_spec = pltpu.PrefetchScalarGridSpec(
        num_scalar_prefetch=2,
        grid=(n // tt,),
        in_specs=[pl.BlockSpec((tt, d), lambda j, *_: (j, 0)),
                  pl.BlockSpec((ne, tt), lambda j, *_: (0, j)),
                  pl.BlockSpec((ne, tt), lambda j, *_: (0, j))],
        out_specs=[pl.BlockSpec(memory_space=pl.ANY), pl.BlockSpec(memory_space=pl.ANY)],
        scratch_shapes=[pltpu.VMEM((ne, DISPATCH_WIN, d), BF16), pltpu.VMEM((ne, DISPATCH_WIN, LANES), F32),
                        pltpu.SMEM((ne,), jnp.int32), pltpu.SMEM((ne,), jnp.int32),
                        pltpu.SemaphoreType.DMA((2, ne))],
    )
    return pl.pallas_call(
        functools.partial(_dispatch_kernel, cap=cap),
        grid_spec=grid_spec,
        out_shape=[jax.ShapeDtypeStruct((ne, cap_pad, d), BF16), jax.ShapeDtypeStruct((ne, cap_pad, LANES), F32)],
        compiler_params=_cparams("arbitrary"),
    )(tb_tiles, n_rounds, h2, pos, aff_t)


def _combine_kernel(tb_ref, nr_ref, x_ref, post_ref, tbt_ref, spread_ref, kcol_ref, ye_hbm, o_ref, ch_ref, one_ref,
                    sem, *, cap):
    j = pl.program_id(0)
    ne = ye_hbm.shape[0]
    spread = spread_ref[...]
    kcol = kcol_ref[...]
    last = cap - COMBINE_CHUNK

    buf = j % 2

    def chunk_copies(tile, r, b):
        def start_of(e):
            s = tb_ref[e, tile] + r * ROUTE_CHUNK
            return jnp.minimum(lax.shift_left(lax.shift_right_logical(s, 4), 4), last)
        return [pltpu.make_async_copy(ye_hbm.at[e, pl.ds(pl.multiple_of(start_of(e), BF16_ROWS), COMBINE_CHUNK)],
                                      ch_ref.at[b, e], sem.at[b]) for e in range(ne)]

    @pl.when(j == 0)
    def _():
        for c in chunk_copies(0, 0, 0):
            c.start()

    @pl.when(j + 1 < pl.num_programs(0))
    def _():
        for c in chunk_copies(j + 1, 0, 1 - buf):
            c.start()

    def late_copy(e, r):
        s = tb_ref[e, j] + r * ROUTE_CHUNK
        start = jnp.minimum(lax.shift_left(lax.shift_right_logical(s, 4), 4), last)
        return pltpu.make_async_copy(ye_hbm.at[e, pl.ds(pl.multiple_of(start, BF16_ROWS), COMBINE_CHUNK)],
                                     one_ref.at[e, pl.ds(0, COMBINE_CHUNK)], sem.at[2 + e])

    def in_round(e, r):
        return tb_ref[e, j + 1] > tb_ref[e, j] + r * ROUTE_CHUNK

    @pl.when(j == 0)
    def _():
        one_ref[:, COMBINE_CHUNK:, :] = jnp.zeros((ne, LANES - COMBINE_CHUNK, one_ref.shape[-1]), one_ref.dtype)

    for e in range(ne):
        @pl.when(in_round(e, 1))
        def _():
            late_copy(e, 1).start()

    post = post_ref[...]

    def rel_slots(r):
        first = tbt_ref[pl.ds(j, 1), :] + (r * ROUTE_CHUNK).astype(F32)
        sv = jnp.minimum(jnp.floor(first * (1.0 / BF16_ROWS)) * BF16_ROWS, float(last))
        mine = (post >= first) & (post < first + ROUTE_CHUNK)
        return jnp.where(mine, post - sv, -1.0).astype(BF16)

    rel_wide = jnp.dot(rel_slots(jnp.int32(0)), spread, preferred_element_type=F32)
    onehot = (rel_wide == kcol).astype(BF16)
    for c in chunk_copies(j, 0, buf):
        c.wait()
    rows = ch_ref[buf].reshape(ne * COMBINE_CHUNK, ch_ref.shape[-1])
    o_ref[...] = x_ref[...] + jnp.dot(onehot, rows, preferred_element_type=F32)


    lane_row = lax.broadcasted_iota(jnp.int32, (LANES, LANES), 0)
    k_lane = lax.broadcasted_iota(jnp.int32, (1, LANES), 1).astype(F32)

    def later_round(r, carry):
        rel = rel_slots(r)
        for e in range(ne):
            @pl.when(tb_ref[e, j + 1] > tb_ref[e, j] + r * ROUTE_CHUNK)
            def _():
                cp = late_copy(e, r)

                @pl.when(r > 1)
                def _():
                    cp.start()
                pick = (lane_row == e).astype(BF16)
                rel_e = jnp.dot(rel, pick, preferred_element_type=F32)
                onehot_e = (rel_e == k_lane).astype(BF16)
                cp.wait()
                o_ref[...] += jnp.dot(onehot_e, one_ref[e], preferred_element_type=F32)
        return carry

    lax.fori_loop(1, nr_ref[j], later_round, 0)


def _combine(x1, pos_t, tb_tiles, tb_t, n_rounds, ye, cap):
    n, d = x1.shape
    ne = ye.shape[0]
    tt = ROUTE_TILE
    assert cap >= COMBINE_CHUNK and cap % BF16_ROWS == 0
    col = np.arange(ne * COMBINE_CHUNK)
    spread = jnp.asarray(np.arange(LANES)[:, None] == (col // COMBINE_CHUNK)[None, :], BF16)
    kcol = jnp.asarray((col % COMBINE_CHUNK)[None, :], F32)
    grid_spec = pltpu.PrefetchScalarGridSpec(
        num_scalar_prefetch=2,
        grid=(n // tt,),
        in_specs=[pl.BlockSpec((tt, d), lambda j, *_: (j, 0)),
                  pl.BlockSpec((tt, LANES), lambda j, *_: (j, 0)),
                  pl.BlockSpec(tb_t.shape, lambda j, *_: (0, 0)),
                  pl.BlockSpec(spread.shape, lambda j, *_: (0, 0)),
                  pl.BlockSpec(kcol.shape, lambda j, *_: (0, 0)),
                  pl.BlockSpec(memory_space=pl.ANY)],
        out_specs=pl.BlockSpec((tt, d), lambda j, *_: (j, 0)),
        scratch_shapes=[pltpu.VMEM((2, ne, COMBINE_CHUNK, d), BF16), pltpu.VMEM((ne, LANES, d), BF16),
                        pltpu.SemaphoreType.DMA((2 + ne,))],
    )
    return pl.pallas_call(
        functools.partial(_combine_kernel, cap=cap),
        grid_spec=grid_spec,
        out_shape=jax.ShapeDtypeStruct((n, d), F32),
        compiler_params=_cparams("arbitrary"),
    )(tb_tiles, n_rounds, x1, pos_t, tb_t, spread, kcol, ye)


def _expert_kernel(xe_ref, gate_ref, wg_ref, wu_ref, wd_ref, o_ref, *, ff_chunk):
    x = xe_ref[0]
    ff = wg_ref.shape[-1]
    acc = None
    for c in range(0, ff, ff_chunk):
        w = min(ff_chunk, ff - c)
        g = jnp.dot(x, wg_ref[0, 0, :, c:c + w], preferred_element_type=F32)
        u = jnp.dot(x, wu_ref[0, 0, :, c:c + w], preferred_element_type=F32)
        hmid = (g / (1.0 + jnp.exp(-g)) * u).astype(BF16)
        part = jnp.dot(hmid, wd_ref[0, 0, c:c + w, :], preferred_element_type=F32)
        acc = part if acc is None else acc + part
    gate = gate_ref[0]
    gate = jnp.concatenate([gate] * (acc.shape[1] // LANES), axis=1)
    o_ref[0] = (acc * gate).astype(o_ref.dtype)


def _expert_ffn(xe, gate, wg, wu, wd, layer, cap, tm, ff_chunk):
    ne, _, d = xe.shape
    ff = wg.shape[-1]
    return pl.pallas_call(
        functools.partial(_expert_kernel, ff_chunk=ff_chunk),
        grid=(ne, cap // tm),
        in_specs=[pl.BlockSpec((1, tm, d), lambda e, m: (e, m, 0)),
                  pl.BlockSpec((1, tm, LANES), lambda e, m: (e, m, 0)),
                  pl.BlockSpec((1, 1, d, ff), lambda e, m: (layer, e, 0, 0)),
                  pl.BlockSpec((1, 1, d, ff), lambda e, m: (layer, e, 0, 0)),
                  pl.BlockSpec((1, 1, ff, d), lambda e, m: (layer, e, 0, 0))],
        out_specs=pl.BlockSpec((1, tm, d), lambda e, m: (e, m, 0)),
        out_shape=jax.ShapeDtypeStruct((ne, cap, d), BF16),
        compiler_params=_cparams("parallel", "arbitrary"),
    )(xe, gate, wg, wu, wd)


def _pick_tile(n, pref):
    t = min(n, pref)
    assert n % t == 0
    return t


EXPERT_FF_CHUNK = 4 * MXU_DIM


def kernel(x_prompt, x_sample, norm1_g, w_in, diff_q_norm, diff_k_norm, lambda_q1, lambda_k1, lambda_q2,
           lambda_k2, diff_subln_g, rel_bias, na_q_norm, na_k_norm, na_rpb, w_branch_diff, w_branch_na,
           w_out, norm2_g, w_router, w_expert_gate, w_expert_up, w_expert_down):
    depth, d_model, _ = w_in.shape
    n_experts = w_router.shape[-1]
    ff = w_expert_gate.shape[-1]

    w_in_b = w_in.astype(BF16)
    wbd_b = w_branch_diff.astype(BF16)
    wbn_b = w_branch_na.astype(BF16)
    wo_b = w_out.astype(BF16)
    wg_b = w_expert_gate.astype(BF16)
    wu_b = w_expert_up.astype(BF16)
    wd_b = w_expert_down.astype(BF16)
    wr_t = jnp.swapaxes(w_router.astype(F32), 1, 2)
    wr_hi = wr_t.astype(BF16)
    wr_lo = (wr_t - wr_hi.astype(F32)).astype(BF16)
    g1 = norm1_g.astype(F32)[:, None, :]
    g2 = norm2_g.astype(F32)[:, None, :]
    tile_gain = lambda g, reps, scale: (jnp.tile(g.astype(F32), (1, reps)) * scale)[:, None, :]
    dqg = tile_gain(diff_q_norm, 2 * DIFF_HEADS, DIFF_HEAD_DIM ** -0.5 * LOG2E)
    dkg = tile_gain(diff_k_norm, 2 * DIFF_HEADS, 1.0)
    nqg = tile_gain(na_q_norm, NA_HEADS, NA_HEAD_DIM ** -0.5 * LOG2E)
    nkg = tile_gain(na_k_norm, NA_HEADS, 1.0)
    subln = diff_subln_g.astype(F32)[:, None, :]
    lam_params = jnp.stack([lambda_q1, lambda_k1, lambda_q2, lambda_k2], axis=1).astype(F32)
    seg_id = jnp.arange(MXU_DIM) // DIFF_HEAD_DIM
    seg = jnp.where(seg_id[:, None] == seg_id[None, :], 1.0 / DIFF_HEAD_DIM, 0.0).astype(BF16)
    na_bias = [_na_bias(na_rpb[l]) for l in range(depth)]

    def run(x):
        b, s, _ = x.shape
        n = b * s
        cap = CAPACITY_FACTOR * n // n_experts
        tm = _pick_tile(n, 512)
        t_attn = _pick_tile(s, 512)
        t5_tiles = _t5_bias_tiles(rel_bias, t_attn, min(s, MXU_DIM))
        x2d = x.reshape(n, d_model)
        for l in range(depth):
            lambda_init = 0.8 - 0.6 * math.exp(-0.3 * l)
            dq, dk, dv, nq, nk, nv, gates = _in_proj(x2d, l, g1, w_in_b, seg, dqg, dkg, nqg, nkg, tm)
            r3 = lambda a: a.reshape(b, s, a.shape[-1])
            yd = _diff_attention(r3(dq), r3(dk), r3(dv), t5_tiles, lam_params, subln, l, lambda_init, t_attn)
            yn = _na_attention(r3(nq), r3(nk), r3(nv), na_bias[l], NA_ROWS_PER_STEP)
            x1, h2, aff_t = _merge_out(x2d, yd.reshape(n, DIFF_WIDTH), yn.reshape(n, NA_WIDTH), gates,
                                       wbd_b, wbn_b, wo_b, g2, wr_hi, wr_lo, l, tm)
            pos, tb128 = _select(aff_t, cap)
            per = ROUTE_TILE // LANES
            tb_tiles = jnp.concatenate([tb128[:, ::per], jnp.full((n_experts, 1), cap, jnp.int32)], axis=1)
            counts = tb_tiles[:, 1:] - tb_tiles[:, :-1]
            n_rounds = jnp.maximum(jnp.max(-(-counts // ROUTE_CHUNK), axis=0), 1).astype(jnp.int32)
            tb_t = jnp.pad(tb_tiles.T.astype(F32), ((0, 7), (0, LANES - n_experts)))
            pos_t = jnp.pad(pos.T.astype(F32), ((0, 0), (0, LANES - n_experts)), constant_values=-1.0)
            xe, gate_rows = _dispatch(h2, pos, aff_t, tb_tiles, n_rounds, cap)
            ye = _expert_ffn(xe, gate_rows, wg_b, wu_b, wd_b, l, cap, _pick_tile(cap, 512), EXPERT_FF_CHUNK)
            x2d = _combine(x1, pos_t, tb_tiles, tb_t, n_rounds, ye, cap)
        return x2d.reshape(b, s, d_model)

    return (run(x_prompt), run(x_sample))
```

```python
import functools
import math

import numpy as np
import jax
import jax.numpy as jnp
from jax import lax
from jax.experimental import pallas as pl
from jax.experimental.pallas import tpu as pltpu

F32 = jnp.float32
BF16 = jnp.bfloat16

DIFF_HEADS = 4
DIFF_HEAD_DIM = 64
DIFF_WIDTH = DIFF_HEADS * 2 * DIFF_HEAD_DIM
NA_HEADS = 8
NA_HEAD_DIM = 64
NA_WIDTH = NA_HEADS * NA_HEAD_DIM
GRID_W = 64
NA_WIN_ROWS = 8
NA_WIN_COLS = 16
T5_BUCKETS = 32
T5_MAX_DIST = 128
CAPACITY_FACTOR = 2
EPS = 1e-6

LANES = 128
MXU_DIM = 256
VMEM_LIMIT_BYTES = 56 * 1024 * 1024
NEG_BIG = -1e30
LOG2E = math.log2(math.e)
NA_ROWS_PER_STEP = 8
BF16_ROWS = 16
ROUTE_TILE = 512
ROUTE_CHUNK = 96
COMBINE_CHUNK = ROUTE_CHUNK + BF16_ROWS
DISPATCH_WIN = ROUTE_CHUNK + BF16_ROWS


def _cparams(*sem):
    return pltpu.CompilerParams(dimension_semantics=sem, vmem_limit_bytes=VMEM_LIMIT_BYTES)


def _layer_block(a, layer):
    return pl.BlockSpec((1,) + a.shape[1:], lambda *_: (layer,) + (0,) * (a.ndim - 1))


def _nt_dot(a, b):
    return lax.dot_general(a, b, (((1,), (1,)), ((), ())), preferred_element_type=F32)


def _inproj_kernel(x_ref, g_ref, w_ref, seg_ref, dqg_ref, dkg_ref, nqg_ref, nkg_ref,
                   dq_ref, dk_ref, dv_ref, nq_ref, nk_ref, nv_ref, gate_ref, *, layer, gate_width):
    x = x_ref[...]
    ms = jnp.mean(x * x, axis=-1, keepdims=True)
    h = (x * lax.rsqrt(ms + EPS) * g_ref[layer]).astype(BF16)
    width = DIFF_WIDTH

    def proj(c0):
        return jnp.dot(h, w_ref[0, :, c0:c0 + width], preferred_element_type=F32)

    def head_norm(a, gain):
        sq = (a * a).astype(BF16)
        halves = [jnp.dot(sq[:, c:c + MXU_DIM], seg_ref[...], preferred_element_type=F32)
                  for c in range(0, width, MXU_DIM)]
        ms_seg = jnp.concatenate(halves, axis=1)
        return a * lax.rsqrt(ms_seg + EPS) * gain

    dq_ref[...] = head_norm(proj(0), dqg_ref[layer]).astype(BF16)
    dk_ref[...] = head_norm(proj(width), dkg_ref[layer]).astype(BF16)
    dv_ref[...] = proj(2 * width).astype(BF16)
    nq_ref[...] = head_norm(proj(3 * width), nqg_ref[layer]).astype(BF16)
    nk_ref[...] = head_norm(proj(4 * width), nkg_ref[layer]).astype(BF16)
    nv_ref[...] = proj(5 * width).astype(BF16)
    for c in range(0, gate_width, width):
        a = proj(6 * width + c)
        gate_ref[:, c:c + width] = (1.0 / (1.0 + jnp.exp(-a))).astype(BF16)


def _in_proj(x2d, layer, norm_g, w_in, seg, dqg, dkg, nqg, nkg, tm):
    n, d = x2d.shape
    in_width = w_in.shape[-1]
    gate_width = in_width - 6 * DIFF_WIDTH
    whole = lambda a: pl.BlockSpec(a.shape, lambda i: (0,) * a.ndim)
    row = lambda w: pl.BlockSpec((tm, w), lambda i: (i, 0))
    outs = [jax.ShapeDtypeStruct((n, DIFF_WIDTH), BF16)] * 6 + [jax.ShapeDtypeStruct((n, gate_width), BF16)]
    return pl.pallas_call(
        functools.partial(_inproj_kernel, layer=layer, gate_width=gate_width),
        grid=(n // tm,),
        in_specs=[row(d), whole(norm_g), _layer_block(w_in, layer), whole(seg), whole(dqg), whole(dkg),
                  whole(nqg), whole(nkg)],
        out_specs=[row(DIFF_WIDTH)] * 6 + [row(gate_width)],
        out_shape=outs,
        compiler_params=_cparams("parallel"),
    )(x2d, norm_g, w_in, seg, dqg, dkg, nqg, nkg)


def _diff_attn_kernel(q_ref, k_ref, v_ref, bias_ref, lam_ref, g_ref, o_ref,
                      m1_ref, a1_ref, m2_ref, a2_ref, *, layer, lambda_init):
    qi = pl.program_id(2)
    hw = 2 * DIFF_HEAD_DIM
    tq = q_ref.shape[1]
    seq = k_ref.shape[1]
    sub = bias_ref.shape[-1]

    q = q_ref[0]
    lane = lax.broadcasted_iota(jnp.int32, q.shape, 1)
    zero = jnp.zeros_like(q)
    q1 = jnp.where(lane < DIFF_HEAD_DIM, q, zero)
    q2 = jnp.where(lane >= DIFF_HEAD_DIM, q, zero)

    def update(qm, k, v_ones, bias, m_ref, a_ref, first):
        s = _nt_dot(qm, k) + bias
        m_cur = jnp.max(s, axis=-1, keepdims=True)
        if first:
            m_new = jnp.broadcast_to(m_cur, (tq, LANES))
        else:
            m_prev = m_ref[...]
            m_new = jnp.maximum(m_prev, m_cur)
        p = jnp.exp2(s - jnp.concatenate([m_new] * (sub // LANES), axis=1))
        pv = jnp.dot(p.astype(BF16), v_ones, preferred_element_type=F32)
        if first:
            a_ref[...] = pv
        else:
            alpha = jnp.exp2(m_prev - m_new)
            a_ref[...] = jnp.concatenate([alpha, alpha], axis=1) * a_ref[...] + pv
        m_ref[...] = m_new

    d_lo, _ = _t5_tile_range(tq, sub)
    n_tiles = bias_ref.shape[1]
    for c in range(seq // sub):
        k = k_ref[0, c * sub:(c + 1) * sub]
        v = v_ref[0, c * sub:(c + 1) * sub]
        v_ones = jnp.concatenate([v, jnp.ones_like(v)], axis=1)
        bias = bias_ref[0, jnp.clip(c - qi * (tq // sub) - d_lo, 0, n_tiles - 1)]
        update(q1, k, v_ones, bias, m1_ref, a1_ref, c == 0)
        update(q2, k, v_ones, bias, m2_ref, a2_ref, c == 0)

    lp = lam_ref[layer]
    lam = (jnp.exp(jnp.sum(lp[0:1] * lp[1:2], axis=-1, keepdims=True))
           - jnp.exp(jnp.sum(lp[2:3] * lp[3:4], axis=-1, keepdims=True)) + lambda_init)
    o = a1_ref[:, :hw] / a1_ref[:, hw:] - lam * (a2_ref[:, :hw] / a2_ref[:, hw:])
    ms = jnp.mean(o * o, axis=-1, keepdims=True)
    o = o * lax.rsqrt(ms + EPS) * g_ref[layer] * (1.0 - lambda_init)
    o_ref[0] = o.astype(BF16)


def _diff_attention(dq, dk, dv, bias_tiles, lam_params, subln_g, layer, lambda_init, t):
    b, s, _ = dq.shape
    hw = 2 * DIFF_HEAD_DIM
    qspec = pl.BlockSpec((1, t, hw), lambda bi, h, qi: (bi, qi, h))
    kspec = pl.BlockSpec((1, s, hw), lambda bi, h, qi: (bi, 0, h))
    bspec = pl.BlockSpec((1,) + bias_tiles.shape[1:], lambda bi, h, qi: (h, 0, 0, 0))
    whole = lambda a: pl.BlockSpec(a.shape, lambda bi, h, qi: (0,) * a.ndim)
    return pl.pallas_call(
        functools.partial(_diff_attn_kernel, layer=layer, lambda_init=lambda_init),
        grid=(b, DIFF_HEADS, s // t),
        in_specs=[qspec, kspec, kspec, bspec, whole(lam_params), whole(subln_g)],
        out_specs=qspec,
        out_shape=jax.ShapeDtypeStruct((b, s, DIFF_WIDTH), BF16),
        scratch_shapes=[pltpu.VMEM((t, LANES), F32), pltpu.VMEM((t, 2 * hw), F32),
                        pltpu.VMEM((t, LANES), F32), pltpu.VMEM((t, 2 * hw), F32)],
        compiler_params=_cparams("parallel", "parallel", "arbitrary"),
    )(dq, dk, dv, bias_tiles, lam_params, subln_g)


def _t5_tile_range(tq, sub):
    d_lo = (-(T5_MAX_DIST - 1) - sub) // sub
    d_hi = -(-(tq + T5_MAX_DIST - 1) // sub)
    return d_lo, d_hi


def _t5_bucket(rel):
    nb = T5_BUCKETS // 2
    max_exact = nb // 2
    side = jnp.where(rel > 0, nb, 0)
    n = jnp.abs(rel)
    nf = jnp.maximum(n, 1).astype(F32)
    large = max_exact + (jnp.log(nf / max_exact) / math.log(T5_MAX_DIST / max_exact) * (nb - max_exact)).astype(jnp.int32)
    large = jnp.minimum(large, nb - 1)
    return side + jnp.where(n < max_exact, n, large)


def _t5_bias_tiles(rel_bias, tq, sub):
    d_lo, d_hi = _t5_tile_range(tq, sub)
    i = jnp.arange(tq, dtype=jnp.int32)[:, None]
    j = jnp.arange(sub, dtype=jnp.int32)[None, :]
    d = jnp.arange(d_lo, d_hi + 1, dtype=jnp.int32)[:, None, None]
    rel = d * sub + (j - i)[None]
    tiles = jnp.einsum("dqkb,bh->dqkh", jax.nn.one_hot(_t5_bucket(rel), T5_BUCKETS, dtype=F32),
                       rel_bias.astype(F32), precision=lax.Precision.HIGHEST)
    return jnp.transpose(tiles, (3, 0, 1, 2)) * LOG2E


def _na_kernel(q_ref, k_ref, v_ref, bias_ref, o_ref, *, rows, rows_per_step):
    nkeys = NA_WIN_ROWS * GRID_W
    lane = lax.broadcasted_iota(jnp.int32, (GRID_W, LANES), 1)
    first_half = lane < NA_HEAD_DIM
    for j in range(rows_per_step):
        r = pl.program_id(1) * rows_per_step + j
        rs = jnp.clip(r - NA_WIN_ROWS // 2, 0, rows - NA_WIN_ROWS)
        case = r - rs
        q = q_ref[0, j]
        kwin = k_ref[0, pl.ds(rs, NA_WIN_ROWS)].reshape(nkeys, NA_WIDTH)
        vwin = v_ref[0, pl.ds(rs, NA_WIN_ROWS)].reshape(nkeys, NA_WIDTH)
        for hp in range(NA_HEADS // 2):
            c0 = hp * LANES
            qp = q[:, c0:c0 + LANES]
            zero = jnp.zeros_like(qp)
            q_stack = jnp.concatenate([jnp.where(first_half, qp, zero), jnp.where(first_half, zero, qp)], axis=0)
            vp = vwin[:, c0:c0 + LANES]
            s = _nt_dot(q_stack, kwin[:, c0:c0 + LANES]) + bias_ref[case, hp]
            m = jnp.max(s, axis=-1, keepdims=True)
            p = jnp.exp2(s - m)
            res = jnp.dot(p.astype(BF16), jnp.concatenate([vp, jnp.ones_like(vp)], axis=1),
                          preferred_element_type=F32)
            o = res[:, :LANES] / res[:, LANES:]
            o_ref[0, j, :, c0:c0 + LANES] = jnp.where(first_half, o[:GRID_W], o[GRID_W:]).astype(BF16)


def _na_attention(nq, nk, nv, bias_full, rows_per_step):
    b, s, _ = nq.shape
    rows = s // GRID_W
    assert rows >= NA_WIN_ROWS and rows % rows_per_step == 0
    shp = (b, rows, GRID_W, NA_WIDTH)
    qspec = pl.BlockSpec((1, rows_per_step, GRID_W, NA_WIDTH), lambda bi, r: (bi, r, 0, 0))
    kspec = pl.BlockSpec((1, rows, GRID_W, NA_WIDTH), lambda bi, r: (bi, 0, 0, 0))
    bspec = pl.BlockSpec(bias_full.shape, lambda bi, r: (0, 0, 0, 0))
    out = pl.pallas_call(
        functools.partial(_na_kernel, rows=rows, rows_per_step=rows_per_step),
        grid=(b, rows // rows_per_step),
        in_specs=[qspec, kspec, kspec, bspec],
        out_specs=qspec,
        out_shape=jax.ShapeDtypeStruct(shp, BF16),
        compiler_params=_cparams("parallel", "arbitrary"),
    )(nq.reshape(shp), nk.reshape(shp), nv.reshape(shp), bias_full)
    return out.reshape(b, s, NA_WIDTH)


def _na_bias(rpb):
    case = jnp.arange(NA_WIN_ROWS)[:, None]
    w = jnp.arange(NA_WIN_ROWS)[None, :]
    dr_idx = w - case + (NA_WIN_ROWS - 1)
    c = jnp.arange(GRID_W)[:, None]
    kc = jnp.arange(GRID_W)[None, :]
    cs = jnp.clip(c - NA_WIN_COLS // 2, 0, GRID_W - NA_WIN_COLS)
    valid = (kc >= cs) & (kc < cs + NA_WIN_COLS)
    dc_idx = jnp.clip(kc - c + NA_WIN_COLS - 1, 0, 2 * NA_WIN_COLS - 2)
    tab = rpb.astype(F32)[:, dr_idx]
    tab = jnp.einsum("hawd,ckd->hawck", tab, jax.nn.one_hot(dc_idx, 2 * NA_WIN_COLS - 1, dtype=F32),
                     precision=lax.Precision.HIGHEST)
    tab = jnp.where(valid[None, None, None], tab * LOG2E, NEG_BIG)
    tab = jnp.transpose(tab, (1, 0, 3, 2, 4))
    return tab.reshape(NA_WIN_ROWS, NA_HEADS // 2, 2 * GRID_W, NA_WIN_ROWS * GRID_W)


def _merge_kernel(x_ref, yd_ref, yn_ref, gate_ref, wbd_ref, wbn_ref, wo_ref, g2_ref, wrh_ref, wrl_ref,
                  x1_ref, h2_ref, aff_ref, *, layer):
    d = x_ref.shape[-1]
    bd = jnp.dot(yd_ref[...], wbd_ref[0], preferred_element_type=F32)
    bn = jnp.dot(yn_ref[...], wbn_ref[0], preferred_element_type=F32)
    merged = gate_ref[:, :d].astype(F32) * bd + gate_ref[:, d:].astype(F32) * bn
    x1 = x_ref[...] + jnp.dot(merged.astype(BF16), wo_ref[0], preferred_element_type=F32)
    x1_ref[...] = x1
    ms = jnp.mean(x1 * x1, axis=-1, keepdims=True)
    t = x1 * lax.rsqrt(ms + EPS) * g2_ref[layer]
    t_hi = t.astype(BF16)
    h2_ref[...] = t_hi
    t_lo = (t - t_hi.astype(F32)).astype(BF16)
    wrh = wrh_ref[layer]
    ne = wrh.shape[0]
    both = _nt_dot(jnp.concatenate([wrh, wrl_ref[layer]], axis=0), t_hi)
    logits = both[:ne] + _nt_dot(wrh, t_lo) + both[ne:]
    mx = jnp.max(logits, axis=0, keepdims=True)
    e = jnp.exp(logits - mx)
    aff_ref[...] = e / jnp.sum(e, axis=0, keepdims=True)


def _merge_out(x2d, yd, yn, gates, wbd, wbn, wo, g2, wrh, wrl, layer, tm):
    n, d = x2d.shape
    ne = wrh.shape[1]
    whole = lambda a: pl.BlockSpec(a.shape, lambda i: (0,) * a.ndim)
    row = lambda w: pl.BlockSpec((tm, w), lambda i: (i, 0))
    return pl.pallas_call(
        functools.partial(_merge_kernel, layer=layer),
        grid=(n // tm,),
        in_specs=[row(d), row(DIFF_WIDTH), row(NA_WIDTH), row(2 * d), _layer_block(wbd, layer),
                  _layer_block(wbn, layer), _layer_block(wo, layer), whole(g2), whole(wrh), whole(wrl)],
        out_specs=[row(d), row(d), pl.BlockSpec((ne, tm), lambda i: (0, i))],
        out_shape=[jax.ShapeDtypeStruct((n, d), F32), jax.ShapeDtypeStruct((n, d), BF16),
                   jax.ShapeDtypeStruct((ne, n), F32)],
        compiler_params=_cparams("parallel"),
    )(x2d, yd, yn, gates, wbd, wbn, wo, g2, wrh, wrl)


def _select_kernel(aff_ref, pos_ref, tb_ref, *, cap):
    ne, nt, _ = aff_ref.shape
    bits = lambda e: pltpu.bitcast(aff_ref[e], jnp.int32)

    def bit_step(i, thr):
        cand = thr | lax.shift_left(jnp.int32(1), 30 - i)
        rows = []
        for e in range(ne):
            ce = cand[e:e + 1]
            ge = (bits(e) >= ce).astype(jnp.int32)
            cnt = jnp.sum(jnp.sum(ge, axis=0, keepdims=True), axis=1, keepdims=True)
            rows.append(jnp.where(cnt >= cap, ce, thr[e:e + 1]))
        return jnp.concatenate(rows, axis=0)

    thr = lax.fori_loop(0, 31, bit_step, jnp.zeros((ne, LANES), jnp.int32))

    li = lax.broadcasted_iota(jnp.int32, (LANES, LANES), 0)
    lj = lax.broadcasted_iota(jnp.int32, (LANES, LANES), 1)
    upper = (li < lj).astype(BF16)
    ones = jnp.ones((LANES, LANES), BF16)
    ti = lax.broadcasted_iota(jnp.int32, (nt, nt), 0)
    tj = lax.broadcasted_iota(jnp.int32, (nt, nt), 1)
    lower = (tj < ti).astype(BF16)

    def prefix(flags):
        fb = flags.astype(BF16)
        within = jnp.dot(fb, upper, preferred_element_type=F32)
        per_tile = jnp.dot(fb, ones, preferred_element_type=F32)
        before = jnp.dot(lower, per_tile.astype(BF16), preferred_element_type=F32)
        return before + within, before

    for e in range(ne):
        be = bits(e)
        te = thr[e:e + 1]
        gt = be > te
        eq = be == te
        n_gt = jnp.sum(jnp.sum(gt.astype(F32), axis=0, keepdims=True), axis=1, keepdims=True)
        eq_rank, _ = prefix(eq)
        sel = gt | (eq & (eq_rank < cap - n_gt))
        pos, before = prefix(sel)
        pos_ref[e] = jnp.where(sel, pos, -1.0).astype(jnp.int32)
        tb_ref[e] = before


def _select(aff_t, cap):
    ne, n = aff_t.shape
    nt = n // LANES
    shp = (ne, nt, LANES)
    spec = pl.BlockSpec(shp, lambda i: (0, 0, 0))
    pos, tb = pl.pallas_call(
        functools.partial(_select_kernel, cap=cap),
        grid=(1,),
        in_specs=[spec],
        out_specs=[spec, spec],
        out_shape=[jax.ShapeDtypeStruct(shp, jnp.int32), jax.ShapeDtypeStruct(shp, F32)],
        compiler_params=_cparams("arbitrary"),
    )(aff_t.reshape(shp))
    return pos.reshape(ne, n), tb[:, :, 0].astype(jnp.int32)


def _dispatch_kernel(tb_ref, nr_ref, h_ref, pos_ref, aff_ref, xe_hbm, gc_hbm, xs_ref, gs_ref, last_ref, pend_ref,
                     sem, *, cap):
    j = pl.program_id(0)
    ne = pos_ref.shape[0]
    tt = h_ref.shape[0]

    @pl.when(j == 0)
    def _():
        xs_ref[...] = jnp.zeros(xs_ref.shape, xs_ref.dtype)
        gs_ref[...] = jnp.zeros(gs_ref.shape, F32)
        for e in range(ne):
            last_ref[e] = 0
            pend_ref[e] = 0
        pad = [(pltpu.make_async_copy(xs_ref.at[e], xe_hbm.at[e, pl.ds(cap + q * DISPATCH_WIN, DISPATCH_WIN)], sem.at[0, e]),
                pltpu.make_async_copy(gs_ref.at[e], gc_hbm.at[e, pl.ds(cap + q * DISPATCH_WIN, DISPATCH_WIN)], sem.at[1, e]))
               for e in range(ne) for q in range((xe_hbm.shape[1] - cap) // DISPATCH_WIN)]
        for cx, cg in pad:
            cx.start()
            cg.start()
        for cx, cg in pad:
            cx.wait()
            cg.wait()

    h = h_ref[...]
    slot = lax.broadcasted_iota(jnp.int32, (DISPATCH_WIN, tt), 0)
    row = lax.broadcasted_iota(jnp.int32, (BF16_ROWS, 1), 0)

    def copies(e, start):
        dst = pl.ds(pl.multiple_of(start, BF16_ROWS), DISPATCH_WIN)
        return (pltpu.make_async_copy(xs_ref.at[e], xe_hbm.at[e, dst], sem.at[0, e]),
                pltpu.make_async_copy(gs_ref.at[e], gc_hbm.at[e, dst], sem.at[1, e]))

    def drain():
        for e in range(ne):
            @pl.when(pend_ref[e] == 1)
            def _():
                for c in copies(e, 0):
                    c.wait()
                pend_ref[e] = 0

    def emit(e, start, xs_e, gate_col, n_carry, off):
        keep = row < n_carry
        gs = jnp.broadcast_to(gate_col, (DISPATCH_WIN, LANES))
        hd = BF16_ROWS
        x_head = xs_e[:hd] + jnp.where(keep, xs_ref[e, pl.ds(off, hd)].astype(F32), 0.0)
        g_head = gs[:hd] + jnp.where(keep, gs_ref[e, pl.ds(off, hd)], 0.0)
        xs_ref[e, hd:] = xs_e[hd:].astype(xs_ref.dtype)
        gs_ref[e, hd:] = gs[hd:]
        xs_ref[e, :hd] = x_head.astype(xs_ref.dtype)
        gs_ref[e, :hd] = g_head
        last_ref[e] = start
        for c in copies(e, start):
            c.start()
        pend_ref[e] = 1

    def gate_of(hit, e):
        return jnp.sum(jnp.where(hit, aff_ref[e:e + 1, :], 0.0), axis=1, keepdims=True)

    first = [tb_ref[e, j] for e in range(ne)]
    starts = [lax.shift_left(lax.shift_right_logical(first[e], 4), 4) for e in range(ne)]
    hits = [pos_ref[e:e + 1, :] - starts[e] == slot for e in range(ne)]
    xs = jnp.dot(jnp.concatenate([hit.astype(BF16) for hit in hits], axis=0), h, preferred_element_type=F32)
    gates = [gate_of(hits[e], e) for e in range(ne)]
    drain()
    for e in range(ne):
        emit(e, starts[e], xs[e * DISPATCH_WIN:(e + 1) * DISPATCH_WIN], gates[e], first[e] - starts[e],
             pl.multiple_of(starts[e] - last_ref[e], BF16_ROWS))

    def later_round(r, carry):
        for e in range(ne):
            @pl.when(tb_ref[e, j + 1] > first[e] + r * ROUTE_CHUNK)
            def _():
                start = starts[e] + r * ROUTE_CHUNK
                hit = pos_ref[e:e + 1, :] - start == slot
                xs_e = jnp.dot(hit.astype(BF16), h, preferred_element_type=F32)
                gate_col = gate_of(hit, e)

                @pl.when(pend_ref[e] == 1)
                def _():
                    for c in copies(e, 0):
                        c.wait()
                    pend_ref[e] = 0
                emit(e, start, xs_e, gate_col, 0, 0)
        return carry

    lax.fori_loop(1, nr_ref[j], later_round, 0)

    @pl.when(j == pl.num_programs(0) - 1)
    def _():
        drain()


def _dispatch(h2, pos, aff_t, tb_tiles, n_rounds, cap):
    n, d = h2.shape
    ne = pos.shape[0]
    tt = ROUTE_TILE
    assert cap % BF16_ROWS == 0
    cap_pad = cap + -(-tt // ROUTE_CHUNK) * DISPATCH_WIN
    grid_spec = pltpu.PrefetchScalarGridSpec(
        num_scalar_prefetch=2,
        grid=(n // tt,),
        in_specs=[pl.BlockSpec((tt, d), lambda j, *_: (j, 0)),
                  pl.BlockSpec((ne, tt), lambda j, *_: (0, j)),
                  pl.BlockSpec((ne, tt), lambda j, *_: (0, j))],
        out_specs=[pl.BlockSpec(memory_space=pl.ANY), pl.BlockSpec(memory_space=pl.ANY)],
        scratch_shapes=[pltpu.VMEM((ne, DISPATCH_WIN, d), BF16), pltpu.VMEM((ne, DISPATCH_WIN, LANES), F32),
                        pltpu.SMEM((ne,), jnp.int32), pltpu.SMEM((ne,), jnp.int32),
                        pltpu.SemaphoreType.DMA((2, ne))],
    )
    return pl.pallas_call(
        functools.partial(_dispatch_kernel, cap=cap),
        grid_spec=grid_spec,
        out_shape=[jax.ShapeDtypeStruct((ne, cap_pad, d), BF16), jax.ShapeDtypeStruct((ne, cap_pad, LANES), F32)],
        compiler_params=_cparams("arbitrary"),
    )(tb_tiles, n_rounds, h2, pos, aff_t)


def _combine_kernel(tb_ref, nr_ref, x_ref, post_ref, tbt_ref, spread_ref, kcol_ref, ye_hbm, o_ref, ch_ref, one_ref,
                    sem, *, cap):
    j = pl.program_id(0)
    ne = ye_hbm.shape[0]
    spread = spread_ref[...]
    kcol = kcol_ref[...]
    last = cap - COMBINE_CHUNK

    buf = j % 2

    def chunk_copies(tile, r, b):
        def start_of(e):
            s = tb_ref[e, tile] + r * ROUTE_CHUNK
            return jnp.minimum(lax.shift_left(lax.shift_right_logical(s, 4), 4), last)
        return [pltpu.make_async_copy(ye_hbm.at[e, pl.ds(pl.multiple_of(start_of(e), BF16_ROWS), COMBINE_CHUNK)],
                                      ch_ref.at[b, e], sem.at[b]) for e in range(ne)]

    @pl.when(j == 0)
    def _():
        for c in chunk_copies(0, 0, 0):
            c.start()

    @pl.when(j + 1 < pl.num_programs(0))
    def _():
        for c in chunk_copies(j + 1, 0, 1 - buf):
            c.start()

    def late_copy(e, r):
        s = tb_ref[e, j] + r * ROUTE_CHUNK
        start = jnp.minimum(lax.shift_left(lax.shift_right_logical(s, 4), 4), last)
        return pltpu.make_async_copy(ye_hbm.at[e, pl.ds(pl.multiple_of(start, BF16_ROWS), COMBINE_CHUNK)],
                                     one_ref.at[e, pl.ds(0, COMBINE_CHUNK)], sem.at[2 + e])

    def in_round(e, r):
        return tb_ref[e, j + 1] > tb_ref[e, j] + r * ROUTE_CHUNK

    @pl.when(j == 0)
    def _():
        one_ref[:, COMBINE_CHUNK:, :] = jnp.zeros((ne, LANES - COMBINE_CHUNK, one_ref.shape[-1]), one_ref.dtype)

    for e in range(ne):
        @pl.when(in_round(e, 1))
        def _():
            late_copy(e, 1).start()

    post = post_ref[...]

    def rel_slots(r):
        first = tbt_ref[pl.ds(j, 1), :] + (r * ROUTE_CHUNK).astype(F32)
        sv = jnp.minimum(jnp.floor(first * (1.0 / BF16_ROWS)) * BF16_ROWS, float(last))
        mine = (post >= first) & (post < first + ROUTE_CHUNK)
        return jnp.where(mine, post - sv, -1.0).astype(BF16)

    rel_wide = jnp.dot(rel_slots(jnp.int32(0)), spread, preferred_element_type=F32)
    onehot = (rel_wide == kcol).astype(BF16)
    for c in chunk_copies(j, 0, buf):
        c.wait()
    rows = ch_ref[buf].reshape(ne * COMBINE_CHUNK, ch_ref.shape[-1])
    o_ref[...] = x_ref[...] + jnp.dot(onehot, rows, preferred_element_type=F32)


    lane_row = lax.broadcasted_iota(jnp.int32, (LANES, LANES), 0)
    k_lane = lax.broadcasted_iota(jnp.int32, (1, LANES), 1).astype(F32)

    def later_round(r, carry):
        rel = rel_slots(r)
        for e in range(ne):
            @pl.when(tb_ref[e, j + 1] > tb_ref[e, j] + r * ROUTE_CHUNK)
            def _():
                cp = late_copy(e, r)

                @pl.when(r > 1)
                def _():
                    cp.start()
                pick = (lane_row == e).astype(BF16)
                rel_e = jnp.dot(rel, pick, preferred_element_type=F32)
                onehot_e = (rel_e == k_lane).astype(BF16)
                cp.wait()
                o_ref[...] += jnp.dot(onehot_e, one_ref[e], preferred_element_type=F32)
        return carry

    lax.fori_loop(1, nr_ref[j], later_round, 0)


def _combine(x1, pos_t, tb_tiles, tb_t, n_rounds, ye, cap):
    n, d = x1.shape
    ne = ye.shape[0]
    tt = ROUTE_TILE
    assert cap >= COMBINE_CHUNK and cap % BF16_ROWS == 0
    col = np.arange(ne * COMBINE_CHUNK)
    spread = jnp.asarray(np.arange(LANES)[:, None] == (col // COMBINE_CHUNK)[None, :], BF16)
    kcol = jnp.asarray((col % COMBINE_CHUNK)[None, :], F32)
    grid_spec = pltpu.PrefetchScalarGridSpec(
        num_scalar_prefetch=2,
        grid=(n // tt,),
        in_specs=[pl.BlockSpec((tt, d), lambda j, *_: (j, 0)),
                  pl.BlockSpec((tt, LANES), lambda j, *_: (j, 0)),
                  pl.BlockSpec(tb_t.shape, lambda j, *_: (0, 0)),
                  pl.BlockSpec(spread.shape, lambda j, *_: (0, 0)),
                  pl.BlockSpec(kcol.shape, lambda j, *_: (0, 0)),
                  pl.BlockSpec(memory_space=pl.ANY)],
        out_specs=pl.BlockSpec((tt, d), lambda j, *_: (j, 0)),
        scratch_shapes=[pltpu.VMEM((2, ne, COMBINE_CHUNK, d), BF16), pltpu.VMEM((ne, LANES, d), BF16),
                        pltpu.SemaphoreType.DMA((2 + ne,))],
    )
    return pl.pallas_call(
        functools.partial(_combine_kernel, cap=cap),
        grid_spec=grid_spec,
        out_shape=jax.ShapeDtypeStruct((n, d), F32),
        compiler_params=_cparams("arbitrary"),
    )(tb_tiles, n_rounds, x1, pos_t, tb_t, spread, kcol, ye)


def _expert_kernel(xe_ref, gate_ref, wg_ref, wu_ref, wd_ref, o_ref, *, ff_chunk):
    x = xe_ref[0]
    ff = wg_ref.shape[-1]
    acc = None
    for c in range(0, ff, ff_chunk):
        w = min(ff_chunk, ff - c)
        g = jnp.dot(x, wg_ref[0, 0, :, c:c + w], preferred_element_type=F32)
        u = jnp.dot(x, wu_ref[0, 0, :, c:c + w], preferred_element_type=F32)
        hmid = (g / (1.0 + jnp.exp(-g)) * u).astype(BF16)
        part = jnp.dot(hmid, wd_ref[0, 0, c:c + w, :], preferred_element_type=F32)
        acc = part if acc is None else acc + part
    gate = gate_ref[0]
    gate = jnp.concatenate([gate] * (acc.shape[1] // LANES), axis=1)
    o_ref[0] = (acc * gate).astype(o_ref.dtype)


def _expert_ffn(xe, gate, wg, wu, wd, layer, cap, tm, ff_chunk):
    ne, _, d = xe.shape
    ff = wg.shape[-1]
    return pl.pallas_call(
        functools.partial(_expert_kernel, ff_chunk=ff_chunk),
        grid=(ne, cap // tm),
        in_specs=[pl.BlockSpec((1, tm, d), lambda e, m: (e, m, 0)),
                  pl.BlockSpec((1, tm, LANES), lambda e, m: (e, m, 0)),
                  pl.BlockSpec((1, 1, d, ff), lambda e, m: (layer, e, 0, 0)),
                  pl.BlockSpec((1, 1, d, ff), lambda e, m: (layer, e, 0, 0)),
                  pl.BlockSpec((1, 1, ff, d), lambda e, m: (layer, e, 0, 0))],
        out_specs=pl.BlockSpec((1, tm, d), lambda e, m: (e, m, 0)),
        out_shape=jax.ShapeDtypeStruct((ne, cap, d), BF16),
        compiler_params=_cparams("parallel", "arbitrary"),
    )(xe, gate, wg, wu, wd)


def _pick_tile(n, pref):
    t = min(n, pref)
    assert n % t == 0
    return t


EXPERT_FF_CHUNK = 4 * MXU_DIM


def kernel(x_prompt, x_sample, norm1_g, w_in, diff_q_norm, diff_k_norm, lambda_q1, lambda_k1, lambda_q2,
           lambda_k2, diff_subln_g, rel_bias, na_q_norm, na_k_norm, na_rpb, w_branch_diff, w_branch_na,
           w_out, norm2_g, w_router, w_expert_gate, w_expert_up, w_expert_down):
    depth, d_model, _ = w_in.shape
    n_experts = w_router.shape[-1]
    ff = w_expert_gate.shape[-1]

    w_in_b = w_in.astype(BF16)
    wbd_b = w_branch_diff.astype(BF16)
    wbn_b = w_branch_na.astype(BF16)
    wo_b = w_out.astype(BF16)
    wg_b = w_expert_gate.astype(BF16)
    wu_b = w_expert_up.astype(BF16)
    wd_b = w_expert_down.astype(BF16)
    wr_t = jnp.swapaxes(w_router.astype(F32), 1, 2)
    wr_hi = wr_t.astype(BF16)
    wr_lo = (wr_t - wr_hi.astype(F32)).astype(BF16)
    g1 = norm1_g.astype(F32)[:, None, :]
    g2 = norm2_g.astype(F32)[:, None, :]
    tile_gain = lambda g, reps, scale: (jnp.tile(g.astype(F32), (1, reps)) * scale)[:, None, :]
    dqg = tile_gain(diff_q_norm, 2 * DIFF_HEADS, DIFF_HEAD_DIM ** -0.5 * LOG2E)
    dkg = tile_gain(diff_k_norm, 2 * DIFF_HEADS, 1.0)
    nqg = tile_gain(na_q_norm, NA_HEADS, NA_HEAD_DIM ** -0.5 * LOG2E)
    nkg = tile_gain(na_k_norm, NA_HEADS, 1.0)
    subln = diff_subln_g.astype(F32)[:, None, :]
    lam_params = jnp.stack([lambda_q1, lambda_k1, lambda_q2, lambda_k2], axis=1).astype(F32)
    seg_id = jnp.arange(MXU_DIM) // DIFF_HEAD_DIM
    seg = jnp.where(seg_id[:, None] == seg_id[None, :], 1.0 / DIFF_HEAD_DIM, 0.0).astype(BF16)
    na_bias = [_na_bias(na_rpb[l]) for l in range(depth)]

    def run(x):
        b, s, _ = x.shape
        n = b * s
        cap = CAPACITY_FACTOR * n // n_experts
        tm = _pick_tile(n, 512)
        t_attn = _pick_tile(s, 1024)
        t5_tiles = _t5_bias_tiles(rel_bias, t_attn, min(s, MXU_DIM))
        x2d = x.reshape(n, d_model)
        for l in range(depth):
            lambda_init = 0.8 - 0.6 * math.exp(-0.3 * l)
            dq, dk, dv, nq, nk, nv, gates = _in_proj(x2d, l, g1, w_in_b, seg, dqg, dkg, nqg, nkg, tm)
            r3 = lambda a: a.reshape(b, s, a.shape[-1])
            yd = _diff_attention(r3(dq), r3(dk), r3(dv), t5_tiles, lam_params, subln, l, lambda_init, t_attn)
            yn = _na_attention(r3(nq), r3(nk), r3(nv), na_bias[l], NA_ROWS_PER_STEP)
            x1, h2, aff_t = _merge_out(x2d, yd.reshape(n, DIFF_WIDTH), yn.reshape(n, NA_WIDTH), gates,
                                       wbd_b, wbn_b, wo_b, g2, wr_hi, wr_lo, l, tm)
            pos, tb128 = _select(aff_t, cap)
            per = ROUTE_TILE // LANES
            tb_tiles = jnp.concatenate([tb128[:, ::per], jnp.full((n_experts, 1), cap, jnp.int32)], axis=1)
            counts = tb_tiles[:, 1:] - tb_tiles[:, :-1]
            n_rounds = jnp.maximum(jnp.max(-(-counts // ROUTE_CHUNK), axis=0), 1).astype(jnp.int32)
            tb_t = jnp.pad(tb_tiles.T.astype(F32), ((0, 7), (0, LANES - n_experts)))
            pos_t = jnp.pad(pos.T.astype(F32), ((0, 0), (0, LANES - n_experts)), constant_values=-1.0)
            xe, gate_rows = _dispatch(h2, pos, aff_t, tb_tiles, n_rounds, cap)
            ye = _expert_ffn(xe, gate_rows, wg_b, wu_b, wd_b, l, cap, _pick_tile(cap, 512), EXPERT_FF_CHUNK)
            x2d = _combine(x1, pos_t, tb_tiles, tb_t, n_rounds, ye, cap)
        return x2d.reshape(b, s, d_model)

    return (run(x_prompt), run(x_sample))
```

```python
import functools
import math

import numpy as np
import jax
import jax.numpy as jnp
from jax import lax
from jax.experimental import pallas as pl
from jax.experimental.pallas import tpu as pltpu

F32 = jnp.float32
BF16 = jnp.bfloat16

DIFF_HEADS = 4
DIFF_HEAD_DIM = 64
DIFF_WIDTH = DIFF_HEADS * 2 * DIFF_HEAD_DIM
NA_HEADS = 8
NA_HEAD_DIM = 64
NA_WIDTH = NA_HEADS * NA_HEAD_DIM
GRID_W = 64
NA_WIN_ROWS = 8
NA_WIN_COLS = 16
T5_BUCKETS = 32
T5_MAX_DIST = 128
CAPACITY_FACTOR = 2
EPS = 1e-6

LANES = 128
MXU_DIM = 256
VMEM_LIMIT_BYTES = 56 * 1024 * 1024
NEG_BIG = -1e30
LOG2E = math.log2(math.e)
NA_ROWS_PER_STEP = 16
BF16_ROWS = 16
ROUTE_TILE = 512
ROUTE_CHUNK = 96
COMBINE_CHUNK = ROUTE_CHUNK + BF16_ROWS
DISPATCH_WIN = ROUTE_CHUNK + BF16_ROWS


def _cparams(*sem):
    return pltpu.CompilerParams(dimension_semantics=sem, vmem_limit_bytes=VMEM_LIMIT_BYTES)


def _layer_block(a, layer):
    return pl.BlockSpec((1,) + a.shape[1:], lambda *_: (layer,) + (0,) * (a.ndim - 1))


def _nt_dot(a, b):
    return lax.dot_general(a, b, (((1,), (1,)), ((), ())), preferred_element_type=F32)


def _inproj_kernel(x_ref, g_ref, w_ref, seg_ref, dqg_ref, dkg_ref, nqg_ref, nkg_ref,
                   dq_ref, dk_ref, dv_ref, nq_ref, nk_ref, nv_ref, gate_ref, *, layer, gate_width):
    x = x_ref[...]
    ms = jnp.mean(x * x, axis=-1, keepdims=True)
    h = (x * lax.rsqrt(ms + EPS) * g_ref[layer]).astype(BF16)
    width = DIFF_WIDTH

    def proj(c0):
        return jnp.dot(h, w_ref[0, :, c0:c0 + width], preferred_element_type=F32)

    def head_norm(a, gain):
        sq = (a * a).astype(BF16)
        halves = [jnp.dot(sq[:, c:c + MXU_DIM], seg_ref[...], preferred_element_type=F32)
                  for c in range(0, width, MXU_DIM)]
        ms_seg = jnp.concatenate(halves, axis=1)
        return a * lax.rsqrt(ms_seg + EPS) * gain

    dq_ref[...] = head_norm(proj(0), dqg_ref[layer]).astype(BF16)
    dk_ref[...] = head_norm(proj(width), dkg_ref[layer]).astype(BF16)
    dv_ref[...] = proj(2 * width).astype(BF16)
    nq_ref[...] = head_norm(proj(3 * width), nqg_ref[layer]).astype(BF16)
    nk_ref[...] = head_norm(proj(4 * width), nkg_ref[layer]).astype(BF16)
    nv_ref[...] = proj(5 * width).astype(BF16)
    for c in range(0, gate_width, width):
        a = proj(6 * width + c)
        gate_ref[:, c:c + width] = (1.0 / (1.0 + jnp.exp(-a))).astype(BF16)


def _in_proj(x2d, layer, norm_g, w_in, seg, dqg, dkg, nqg, nkg, tm):
    n, d = x2d.shape
    in_width = w_in.shape[-1]
    gate_width = in_width - 6 * DIFF_WIDTH
    whole = lambda a: pl.BlockSpec(a.shape, lambda i: (0,) * a.ndim)
    row = lambda w: pl.BlockSpec((tm, w), lambda i: (i, 0))
    outs = [jax.ShapeDtypeStruct((n, DIFF_WIDTH), BF16)] * 6 + [jax.ShapeDtypeStruct((n, gate_width), BF16)]
    return pl.pallas_call(
        functools.partial(_inproj_kernel, layer=layer, gate_width=gate_width),
        grid=(n // tm,),
        in_specs=[row(d), whole(norm_g), _layer_block(w_in, layer), whole(seg), whole(dqg), whole(dkg),
                  whole(nqg), whole(nkg)],
        out_specs=[row(DIFF_WIDTH)] * 6 + [row(gate_width)],
        out_shape=outs,
        compiler_params=_cparams("parallel"),
    )(x2d, norm_g, w_in, seg, dqg, dkg, nqg, nkg)


def _diff_attn_kernel(q_ref, k_ref, v_ref, bias_ref, lam_ref, g_ref, o_ref,
                      m1_ref, a1_ref, m2_ref, a2_ref, *, layer, lambda_init):
    qi = pl.program_id(2)
    hw = 2 * DIFF_HEAD_DIM
    tq = q_ref.shape[1]
    seq = k_ref.shape[1]
    sub = bias_ref.shape[-1]

    q = q_ref[0]
    lane = lax.broadcasted_iota(jnp.int32, q.shape, 1)
    zero = jnp.zeros_like(q)
    q1 = jnp.where(lane < DIFF_HEAD_DIM, q, zero)
    q2 = jnp.where(lane >= DIFF_HEAD_DIM, q, zero)

    def update(qm, k, v_ones, bias, m_ref, a_ref, first):
        s = _nt_dot(qm, k) + bias
        m_cur = jnp.max(s, axis=-1, keepdims=True)
        if first:
            m_new = jnp.broadcast_to(m_cur, (tq, LANES))
        else:
            m_prev = m_ref[...]
            m_new = jnp.maximum(m_prev, m_cur)
        p = jnp.exp2(s - jnp.concatenate([m_new] * (sub // LANES), axis=1))
        pv = jnp.dot(p.astype(BF16), v_ones, preferred_element_type=F32)
        if first:
            a_ref[...] = pv
        else:
            alpha = jnp.exp2(m_prev - m_new)
            a_ref[...] = jnp.concatenate([alpha, alpha], axis=1) * a_ref[...] + pv
        m_ref[...] = m_new

    d_lo, _ = _t5_tile_range(tq, sub)
    n_tiles = bias_ref.shape[1]
    for c in range(seq // sub):
        k = k_ref[0, c * sub:(c + 1) * sub]
        v = v_ref[0, c * sub:(c + 1) * sub]
        v_ones = jnp.concatenate([v, jnp.ones_like(v)], axis=1)
        bias = bias_ref[0, jnp.clip(c - qi * (tq // sub) - d_lo, 0, n_tiles - 1)]
        update(q1, k, v_ones, bias, m1_ref, a1_ref, c == 0)
        update(q2, k, v_ones, bias, m2_ref, a2_ref, c == 0)

    lp = lam_ref[layer]
    lam = (jnp.exp(jnp.sum(lp[0:1] * lp[1:2], axis=-1, keepdims=True))
           - jnp.exp(jnp.sum(lp[2:3] * lp[3:4], axis=-1, keepdims=True)) + lambda_init)
    o = a1_ref[:, :hw] / a1_ref[:, hw:] - lam * (a2_ref[:, :hw] / a2_ref[:, hw:])
    ms = jnp.mean(o * o, axis=-1, keepdims=True)
    o = o * lax.rsqrt(ms + EPS) * g_ref[layer] * (1.0 - lambda_init)
    o_ref[0] = o.astype(BF16)


def _diff_attention(dq, dk, dv, bias_tiles, lam_params, subln_g, layer, lambda_init, t):
    b, s, _ = dq.shape
    hw = 2 * DIFF_HEAD_DIM
    qspec = pl.BlockSpec((1, t, hw), lambda h, bi, qi: (bi, qi, h))
    kspec = pl.BlockSpec((1, s, hw), lambda h, bi, qi: (bi, 0, h))
    bspec = pl.BlockSpec((1,) + bias_tiles.shape[1:], lambda h, bi, qi: (h, 0, 0, 0))
    whole = lambda a: pl.BlockSpec(a.shape, lambda h, bi, qi: (0,) * a.ndim)
    return pl.pallas_call(
        functools.partial(_diff_attn_kernel, layer=layer, lambda_init=lambda_init),
        grid=(DIFF_HEADS, b, s // t),
        in_specs=[qspec, kspec, kspec, bspec, whole(lam_params), whole(subln_g)],
        out_specs=qspec,
        out_shape=jax.ShapeDtypeStruct((b, s, DIFF_WIDTH), BF16),
        scratch_shapes=[pltpu.VMEM((t, LANES), F32), pltpu.VMEM((t, 2 * hw), F32),
                        pltpu.VMEM((t, LANES), F32), pltpu.VMEM((t, 2 * hw), F32)],
        compiler_params=_cparams("parallel", "parallel", "arbitrary"),
    )(dq, dk, dv, bias_tiles, lam_params, subln_g)


def _t5_tile_range(tq, sub):
    d_lo = (-(T5_MAX_DIST - 1) - sub) // sub
    d_hi = -(-(tq + T5_MAX_DIST - 1) // sub)
    return d_lo, d_hi


def _t5_bucket(rel):
    nb = T5_BUCKETS // 2
    max_exact = nb // 2
    side = jnp.where(rel > 0, nb, 0)
    n = jnp.abs(rel)
    nf = jnp.maximum(n, 1).astype(F32)
    large = max_exact + (jnp.log(nf / max_exact) / math.log(T5_MAX_DIST / max_exact) * (nb - max_exact)).astype(jnp.int32)
    large = jnp.minimum(large, nb - 1)
    return side + jnp.where(n < max_exact, n, large)


def _t5_bias_tiles(rel_bias, tq, sub):
    d_lo, d_hi = _t5_tile_range(tq, sub)
    i = jnp.arange(tq, dtype=jnp.int32)[:, None]
    j = jnp.arange(sub, dtype=jnp.int32)[None, :]
    d = jnp.arange(d_lo, d_hi + 1, dtype=jnp.int32)[:, None, None]
    rel = d * sub + (j - i)[None]
    tiles = jnp.einsum("dqkb,bh->dqkh", jax.nn.one_hot(_t5_bucket(rel), T5_BUCKETS, dtype=F32),
                       rel_bias.astype(F32), precision=lax.Precision.HIGHEST)
    return jnp.transpose(tiles, (3, 0, 1, 2)) * LOG2E


def _na_kernel(q_ref, k_ref, v_ref, bias_ref, o_ref, *, rows, rows_per_step):
    nkeys = NA_WIN_ROWS * GRID_W
    lane = lax.broadcasted_iota(jnp.int32, (GRID_W, LANES), 1)
    first_half = lane < NA_HEAD_DIM
    for j in range(rows_per_step):
        r = pl.program_id(1) * rows_per_step + j
        rs = jnp.clip(r - NA_WIN_ROWS // 2, 0, rows - NA_WIN_ROWS)
        case = r - rs
        q = q_ref[0, j]
        kwin = k_ref[0, pl.ds(rs, NA_WIN_ROWS)].reshape(nkeys, NA_WIDTH)
        vwin = v_ref[0, pl.ds(rs, NA_WIN_ROWS)].reshape(nkeys, NA_WIDTH)
        for hp in range(NA_HEADS // 2):
            c0 = hp * LANES
            qp = q[:, c0:c0 + LANES]
            zero = jnp.zeros_like(qp)
            q_stack = jnp.concatenate([jnp.where(first_half, qp, zero), jnp.where(first_half, zero, qp)], axis=0)
            vp = vwin[:, c0:c0 + LANES]
            s = _nt_dot(q_stack, kwin[:, c0:c0 + LANES]) + bias_ref[case, hp]
            m = jnp.max(s, axis=-1, keepdims=True)
            p = jnp.exp2(s - m)
            res = jnp.dot(p.astype(BF16), jnp.concatenate([vp, jnp.ones_like(vp)], axis=1),
                          preferred_element_type=F32)
            o = res[:, :LANES] / res[:, LANES:]
            o_ref[0, j, :, c0:c0 + LANES] = jnp.where(first_half, o[:GRID_W], o[GRID_W:]).astype(BF16)


def _na_attention(nq, nk, nv, bias_full, rows_per_step):
    b, s, _ = nq.shape
    rows = s // GRID_W
    assert rows >= NA_WIN_ROWS and rows % rows_per_step == 0
    shp = (b, rows, GRID_W, NA_WIDTH)
    qspec = pl.BlockSpec((1, rows_per_step, GRID_W, NA_WIDTH), lambda bi, r: (bi, r, 0, 0))
    kspec = pl.BlockSpec((1, rows, GRID_W, NA_WIDTH), lambda bi, r: (bi, 0, 0, 0))
    bspec = pl.BlockSpec(bias_full.shape, lambda bi, r: (0, 0, 0, 0))
    out = pl.pallas_call(
        functools.partial(_na_kernel, rows=rows, rows_per_step=rows_per_step),
        grid=(b, rows // rows_per_step),
        in_specs=[qspec, kspec, kspec, bspec],
        out_specs=qspec,
        out_shape=jax.ShapeDtypeStruct(shp, BF16),
        compiler_params=_cparams("parallel", "arbitrary"),
    )(nq.reshape(shp), nk.reshape(shp), nv.reshape(shp), bias_full)
    return out.reshape(b, s, NA_WIDTH)


def _na_bias(rpb):
    case = jnp.arange(NA_WIN_ROWS)[:, None]
    w = jnp.arange(NA_WIN_ROWS)[None, :]
    dr_idx = w - case + (NA_WIN_ROWS - 1)
    c = jnp.arange(GRID_W)[:, None]
    kc = jnp.arange(GRID_W)[None, :]
    cs = jnp.clip(c - NA_WIN_COLS // 2, 0, GRID_W - NA_WIN_COLS)
    valid = (kc >= cs) & (kc < cs + NA_WIN_COLS)
    dc_idx = jnp.clip(kc - c + NA_WIN_COLS - 1, 0, 2 * NA_WIN_COLS - 2)
    tab = rpb.astype(F32)[:, dr_idx]
    tab = jnp.einsum("hawd,ckd->hawck", tab, jax.nn.one_hot(dc_idx, 2 * NA_WIN_COLS - 1, dtype=F32),
                     precision=lax.Precision.HIGHEST)
    tab = jnp.where(valid[None, None, None], tab * LOG2E, NEG_BIG)
    tab = jnp.transpose(tab, (1, 0, 3, 2, 4))
    return tab.reshape(NA_WIN_ROWS, NA_HEADS // 2, 2 * GRID_W, NA_WIN_ROWS * GRID_W)


def _merge_kernel(x_ref, yd_ref, yn_ref, gate_ref, wbd_ref, wbn_ref, wo_ref, g2_ref, wrh_ref, wrl_ref,
                  x1_ref, h2_ref, aff_ref, *, layer):
    d = x_ref.shape[-1]
    bd = jnp.dot(yd_ref[...], wbd_ref[0], preferred_element_type=F32)
    bn = jnp.dot(yn_ref[...], wbn_ref[0], preferred_element_type=F32)
    merged = gate_ref[:, :d].astype(F32) * bd + gate_ref[:, d:].astype(F32) * bn
    x1 = x_ref[...] + jnp.dot(merged.astype(BF16), wo_ref[0], preferred_element_type=F32)
    x1_ref[...] = x1
    ms = jnp.mean(x1 * x1, axis=-1, keepdims=True)
    t = x1 * lax.rsqrt(ms + EPS) * g2_ref[layer]
    t_hi = t.astype(BF16)
    h2_ref[...] = t_hi
    t_lo = (t - t_hi.astype(F32)).astype(BF16)
    wrh = wrh_ref[layer]
    ne = wrh.shape[0]
    both = _nt_dot(jnp.concatenate([wrh, wrl_ref[layer]], axis=0), t_hi)
    logits = both[:ne] + _nt_dot(wrh, t_lo) + both[ne:]
    mx = jnp.max(logits, axis=0, keepdims=True)
    e = jnp.exp(logits - mx)
    aff_ref[...] = e / jnp.sum(e, axis=0, keepdims=True)


def _merge_out(x2d, yd, yn, gates, wbd, wbn, wo, g2, wrh, wrl, layer, tm):
    n, d = x2d.shape
    ne = wrh.shape[1]
    whole = lambda a: pl.BlockSpec(a.shape, lambda i: (0,) * a.ndim)
    row = lambda w: pl.BlockSpec((tm, w), lambda i: (i, 0))
    return pl.pallas_call(
        functools.partial(_merge_kernel, layer=layer),
        grid=(n // tm,),
        in_specs=[row(d), row(DIFF_WIDTH), row(NA_WIDTH), row(2 * d), _layer_block(wbd, layer),
                  _layer_block(wbn, layer), _layer_block(wo, layer), whole(g2), whole(wrh), whole(wrl)],
        out_specs=[row(d), row(d), pl.BlockSpec((ne, tm), lambda i: (0, i))],
        out_shape=[jax.ShapeDtypeStruct((n, d), F32), jax.ShapeDtypeStruct((n, d), BF16),
                   jax.ShapeDtypeStruct((ne, n), F32)],
        compiler_params=_cparams("parallel"),
    )(x2d, yd, yn, gates, wbd, wbn, wo, g2, wrh, wrl)


def _select_kernel(aff_ref, pos_ref, tb_ref, *, cap):
    ne, nt, _ = aff_ref.shape
    bits = lambda e: pltpu.bitcast(aff_ref[e], jnp.int32)

    def bit_step(i, thr):
        cand = thr | lax.shift_left(jnp.int32(1), 30 - i)
        rows = []
        for e in range(ne):
            ce = cand[e:e + 1]
            ge = (bits(e) >= ce).astype(jnp.int32)
            cnt = jnp.sum(jnp.sum(ge, axis=0, keepdims=True), axis=1, keepdims=True)
            rows.append(jnp.where(cnt >= cap, ce, thr[e:e + 1]))
        return jnp.concatenate(rows, axis=0)

    thr = lax.fori_loop(0, 31, bit_step, jnp.zeros((ne, LANES), jnp.int32))

    li = lax.broadcasted_iota(jnp.int32, (LANES, LANES), 0)
    lj = lax.broadcasted_iota(jnp.int32, (LANES, LANES), 1)
    upper = (li < lj).astype(BF16)
    ones = jnp.ones((LANES, LANES), BF16)
    ti = lax.broadcasted_iota(jnp.int32, (nt, nt), 0)
    tj = lax.broadcasted_iota(jnp.int32, (nt, nt), 1)
    lower = (tj < ti).astype(BF16)

    def prefix(flags):
        fb = flags.astype(BF16)
        within = jnp.dot(fb, upper, preferred_element_type=F32)
        per_tile = jnp.dot(fb, ones, preferred_element_type=F32)
        before = jnp.dot(lower, per_tile.astype(BF16), preferred_element_type=F32)
        return before + within, before

    for e in range(ne):
        be = bits(e)
        te = thr[e:e + 1]
        gt = be > te
        eq = be == te
        n_gt = jnp.sum(jnp.sum(gt.astype(F32), axis=0, keepdims=True), axis=1, keepdims=True)
        eq_rank, _ = prefix(eq)
        sel = gt | (eq & (eq_rank < cap - n_gt))
        pos, before = prefix(sel)
        pos_ref[e] = jnp.where(sel, pos, -1.0).astype(jnp.int32)
        tb_ref[e] = before


def _select(aff_t, cap):
    ne, n = aff_t.shape
    nt = n // LANES
    shp = (ne, nt, LANES)
    spec = pl.BlockSpec(shp, lambda i: (0, 0, 0))
    pos, tb = pl.pallas_call(
        functools.partial(_select_kernel, cap=cap),
        grid=(1,),
        in_specs=[spec],
        out_specs=[spec, spec],
        out_shape=[jax.ShapeDtypeStruct(shp, jnp.int32), jax.ShapeDtypeStruct(shp, F32)],
        compiler_params=_cparams("arbitrary"),
    )(aff_t.reshape(shp))
    return pos.reshape(ne, n), tb[:, :, 0].astype(jnp.int32)


def _dispatch_kernel(tb_ref, nr_ref, h_ref, pos_ref, aff_ref, xe_hbm, gc_hbm, xs_ref, gs_ref, last_ref, pend_ref,
                     sem, *, cap):
    j = pl.program_id(0)
    ne = pos_ref.shape[0]
    tt = h_ref.shape[0]

    @pl.when(j == 0)
    def _():
        xs_ref[...] = jnp.zeros(xs_ref.shape, xs_ref.dtype)
        gs_ref[...] = jnp.zeros(gs_ref.shape, F32)
        for e in range(ne):
            last_ref[e] = 0
            pend_ref[e] = 0
        pad = [(pltpu.make_async_copy(xs_ref.at[e], xe_hbm.at[e, pl.ds(cap + q * DISPATCH_WIN, DISPATCH_WIN)], sem.at[0, e]),
                pltpu.make_async_copy(gs_ref.at[e], gc_hbm.at[e, pl.ds(cap + q * DISPATCH_WIN, DISPATCH_WIN)], sem.at[1, e]))
               for e in range(ne) for q in range((xe_hbm.shape[1] - cap) // DISPATCH_WIN)]
        for cx, cg in pad:
            cx.start()
            cg.start()
        for cx, cg in pad:
            cx.wait()
            cg.wait()

    h = h_ref[...]
    slot = lax.broadcasted_iota(jnp.int32, (DISPATCH_WIN, tt), 0)
    row = lax.broadcasted_iota(jnp.int32, (BF16_ROWS, 1), 0)

    def copies(e, start):
        dst = pl.ds(pl.multiple_of(start, BF16_ROWS), DISPATCH_WIN)
        return (pltpu.make_async_copy(xs_ref.at[e], xe_hbm.at[e, dst], sem.at[0, e]),
                pltpu.make_async_copy(gs_ref.at[e], gc_hbm.at[e, dst], sem.at[1, e]))

    def drain():
        for e in range(ne):
            @pl.when(pend_ref[e] == 1)
            def _():
                for c in copies(e, 0):
                    c.wait()
                pend_ref[e] = 0

    def emit(e, start, xs_e, gate_col, n_carry, off):
        keep = row < n_carry
        gs = jnp.broadcast_to(gate_col, (DISPATCH_WIN, LANES))
        hd = BF16_ROWS
        x_head = xs_e[:hd] + jnp.where(keep, xs_ref[e, pl.ds(off, hd)].astype(F32), 0.0)
        g_head = gs[:hd] + jnp.where(keep, gs_ref[e, pl.ds(off, hd)], 0.0)
        xs_ref[e, hd:] = xs_e[hd:].astype(xs_ref.dtype)
        gs_ref[e, hd:] = gs[hd:]
        xs_ref[e, :hd] = x_head.astype(xs_ref.dtype)
        gs_ref[e, :hd] = g_head
        last_ref[e] = start
        for c in copies(e, start):
            c.start()
        pend_ref[e] = 1

    def gate_of(hit, e):
        return jnp.sum(jnp.where(hit, aff_ref[e:e + 1, :], 0.0), axis=1, keepdims=True)

    first = [tb_ref[e, j] for e in range(ne)]
    starts = [lax.shift_left(lax.shift_right_logical(first[e], 4), 4) for e in range(ne)]
    hits = [pos_ref[e:e + 1, :] - starts[e] == slot for e in range(ne)]
    xs = jnp.dot(jnp.concatenate([hit.astype(BF16) for hit in hits], axis=0), h, preferred_element_type=F32)
    gates = [gate_of(hits[e], e) for e in range(ne)]
    drain()
    for e in range(ne):
        emit(e, starts[e], xs[e * DISPATCH_WIN:(e + 1) * DISPATCH_WIN], gates[e], first[e] - starts[e],
             pl.multiple_of(starts[e] - last_ref[e], BF16_ROWS))

    def later_round(r, carry):
        for e in range(ne):
            @pl.when(tb_ref[e, j + 1] > first[e] + r * ROUTE_CHUNK)
            def _():
                start = starts[e] + r * ROUTE_CHUNK
                hit = pos_ref[e:e + 1, :] - start == slot
                xs_e = jnp.dot(hit.astype(BF16), h, preferred_element_type=F32)
                gate_col = gate_of(hit, e)

                @pl.when(pend_ref[e] == 1)
                def _():
                    for c in copies(e, 0):
                        c.wait()
                    pend_ref[e] = 0
                emit(e, start, xs_e, gate_col, 0, 0)
        return carry

    lax.fori_loop(1, nr_ref[j], later_round, 0)

    @pl.when(j == pl.num_programs(0) - 1)
    def _():
        drain()


def _dispatch(h2, pos, aff_t, tb_tiles, n_rounds, cap):
    n, d = h2.shape
    ne = pos.shape[0]
    tt = ROUTE_TILE
    assert cap % BF16_ROWS == 0
    cap_pad = cap + -(-tt // ROUTE_CHUNK) * DISPATCH_WIN
    grid_spec = pltpu.PrefetchScalarGridSpec(
        num_scalar_prefetch=2,
        grid=(n // tt,),
        in_specs=[pl.BlockSpec((tt, d), lambda j, *_: (j, 0)),
                  pl.BlockSpec((ne, tt), lambda j, *_: (0, j)),
                  pl.BlockSpec((ne, tt), lambda j, *_: (0, j))],
        out_specs=[pl.BlockSpec(memory_space=pl.ANY), pl.BlockSpec(memory_space=pl.ANY)],
        scratch_shapes=[pltpu.VMEM((ne, DISPATCH_WIN, d), BF16), pltpu.VMEM((ne, DISPATCH_WIN, LANES), F32),
                        pltpu.SMEM((ne,), jnp.int32), pltpu.SMEM((ne,), jnp.int32),
                        pltpu.SemaphoreType.DMA((2, ne))],
    )
    return pl.pallas_call(
        functools.partial(_dispatch_kernel, cap=cap),
        grid_spec=grid_spec,
        out_shape=[jax.ShapeDtypeStruct((ne, cap_pad, d), BF16), jax.ShapeDtypeStruct((ne, cap_pad, LANES), F32)],
        compiler_params=_cparams("arbitrary"),
    )(tb_tiles, n_rounds, h2, pos, aff_t)


def _combine_kernel(tb_ref, nr_ref, x_ref, post_ref, tbt_ref, spread_ref, kcol_ref, ye_hbm, o_ref, ch_ref, one_ref,
                    sem, *, cap):
    j = pl.program_id(0)
    ne = ye_hbm.shape[0]
    spread = spread_ref[...]
    kcol = kcol_ref[...]
    last = cap - COMBINE_CHUNK

    buf = j % 2

    def chunk_copies(tile, r, b):
        def start_of(e):
            s = tb_ref[e, tile] + r * ROUTE_CHUNK
            return jnp.minimum(lax.shift_left(lax.shift_right_logical(s, 4), 4), last)
        return [pltpu.make_async_copy(ye_hbm.at[e, pl.ds(pl.multiple_of(start_of(e), BF16_ROWS), COMBINE_CHUNK)],
                                      ch_ref.at[b, e], sem.at[b]) for e in range(ne)]

    @pl.when(j == 0)
    def _():
        for c in chunk_copies(0, 0, 0):
            c.start()

    @pl.when(j + 1 < pl.num_programs(0))
    def _():
        for c in chunk_copies(j + 1, 0, 1 - buf):
            c.start()

    def late_copy(e, r):
        s = tb_ref[e, j] + r * ROUTE_CHUNK
        start = jnp.minimum(lax.shift_left(lax.shift_right_logical(s, 4), 4), last)
        return pltpu.make_async_copy(ye_hbm.at[e, pl.ds(pl.multiple_of(start, BF16_ROWS), COMBINE_CHUNK)],
                                     one_ref.at[e, pl.ds(0, COMBINE_CHUNK)], sem.at[2 + e])

    def in_round(e, r):
        return tb_ref[e, j + 1] > tb_ref[e, j] + r * ROUTE_CHUNK

    @pl.when(j == 0)
    def _():
        one_ref[:, COMBINE_CHUNK:, :] = jnp.zeros((ne, LANES - COMBINE_CHUNK, one_ref.shape[-1]), one_ref.dtype)

    for e in range(ne):
        @pl.when(in_round(e, 1))
        def _():
            late_copy(e, 1).start()

    post = post_ref[...]

    def rel_slots(r):
        first = tbt_ref[pl.ds(j, 1), :] + (r * ROUTE_CHUNK).astype(F32)
        sv = jnp.minimum(jnp.floor(first * (1.0 / BF16_ROWS)) * BF16_ROWS, float(last))
        mine = (post >= first) & (post < first + ROUTE_CHUNK)
        return jnp.where(mine, post - sv, -1.0).astype(BF16)

    rel_wide = jnp.dot(rel_slots(jnp.int32(0)), spread, preferred_element_type=F32)
    onehot = (rel_wide == kcol).astype(BF16)
    for c in chunk_copies(j, 0, buf):
        c.wait()
    rows = ch_ref[buf].reshape(ne * COMBINE_CHUNK, ch_ref.shape[-1])
    o_ref[...] = x_ref[...] + jnp.dot(onehot, rows, preferred_element_type=F32)


    lane_row = lax.broadcasted_iota(jnp.int32, (LANES, LANES), 0)
    k_lane = lax.broadcasted_iota(jnp.int32, (1, LANES), 1).astype(F32)

    def later_round(r, carry):
        rel = rel_slots(r)
        for e in range(ne):
            @pl.when(tb_ref[e, j + 1] > tb_ref[e, j] + r * ROUTE_CHUNK)
            def _():
                cp = late_copy(e, r)

                @pl.when(r > 1)
                def _():
                    cp.start()
                pick = (lane_row == e).astype(BF16)
                rel_e = jnp.dot(rel, pick, preferred_element_type=F32)
                onehot_e = (rel_e == k_lane).astype(BF16)
                cp.wait()
                o_ref[...] += jnp.dot(onehot_e, one_ref[e], preferred_element_type=F32)
        return carry

    lax.fori_loop(1, nr_ref[j], later_round, 0)


def _combine(x1, pos_t, tb_tiles, tb_t, n_rounds, ye, cap):
    n, d = x1.shape
    ne = ye.shape[0]
    tt = ROUTE_TILE
    assert cap >= COMBINE_CHUNK and cap % BF16_ROWS == 0
    col = np.arange(ne * COMBINE_CHUNK)
    spread = jnp.asarray(np.arange(LANES)[:, None] == (col // COMBINE_CHUNK)[None, :], BF16)
    kcol = jnp.asarray((col % COMBINE_CHUNK)[None, :], F32)
    grid_spec = pltpu.PrefetchScalarGridSpec(
        num_scalar_prefetch=2,
        grid=(n // tt,),
        in_specs=[pl.BlockSpec((tt, d), lambda j, *_: (j, 0)),
                  pl.BlockSpec((tt, LANES), lambda j, *_: (j, 0)),
                  pl.BlockSpec(tb_t.shape, lambda j, *_: (0, 0)),
                  pl.BlockSpec(spread.shape, lambda j, *_: (0, 0)),
                  pl.BlockSpec(kcol.shape, lambda j, *_: (0, 0)),
                  pl.BlockSpec(memory_space=pl.ANY)],
        out_specs=pl.BlockSpec((tt, d), lambda j, *_: (j, 0)),
        scratch_shapes=[pltpu.VMEM((2, ne, COMBINE_CHUNK, d), BF16), pltpu.VMEM((ne, LANES, d), BF16),
                        pltpu.SemaphoreType.DMA((2 + ne,))],
    )
    return pl.pallas_call(
        functools.partial(_combine_kernel, cap=cap),
        grid_spec=grid_spec,
        out_shape=jax.ShapeDtypeStruct((n, d), F32),
        compiler_params=_cparams("arbitrary"),
    )(tb_tiles, n_rounds, x1, pos_t, tb_t, spread, kcol, ye)


def _expert_kernel(xe_ref, gate_ref, wg_ref, wu_ref, wd_ref, o_ref, *, ff_chunk):
    x = xe_ref[0]
    ff = wg_ref.shape[-1]
    acc = None
    for c in range(0, ff, ff_chunk):
        w = min(ff_chunk, ff - c)
        g = jnp.dot(x, wg_ref[0, 0, :, c:c + w], preferred_element_type=F32)
        u = jnp.dot(x, wu_ref[0, 0, :, c:c + w], preferred_element_type=F32)
        hmid = (g / (1.0 + jnp.exp(-g)) * u).astype(BF16)
        part = jnp.dot(hmid, wd_ref[0, 0, c:c + w, :], preferred_element_type=F32)
        acc = part if acc is None else acc + part
    gate = gate_ref[0]
    gate = jnp.concatenate([gate] * (acc.shape[1] // LANES), axis=1)
    o_ref[0] = (acc * gate).astype(o_ref.dtype)


def _expert_ffn(xe, gate, wg, wu, wd, layer, cap, tm, ff_chunk):
    ne, _, d = xe.shape
    ff = wg.shape[-1]
    return pl.pallas_call(
        functools.partial(_expert_kernel, ff_chunk=ff_chunk),
        grid=(ne, cap // tm),
        in_specs=[pl.BlockSpec((1, tm, d), lambda e, m: (e, m, 0)),
                  pl.BlockSpec((1, tm, LANES), lambda e, m: (e, m, 0)),
                  pl.BlockSpec((1, 1, d, ff), lambda e, m: (layer, e, 0, 0)),
                  pl.BlockSpec((1, 1, d, ff), lambda e, m: (layer, e, 0, 0)),
                  pl.BlockSpec((1, 1, ff, d), lambda e, m: (layer, e, 0, 0))],
        out_specs=pl.BlockSpec((1, tm, d), lambda e, m: (e, m, 0)),
        out_shape=jax.ShapeDtypeStruct((ne, cap, d), BF16),
        compiler_params=_cparams("parallel", "arbitrary"),
    )(xe, gate, wg, wu, wd)


def _pick_tile(n, pref):
    t = min(n, pref)
    assert n % t == 0
    return t


EXPERT_FF_CHUNK = 4 * MXU_DIM


def kernel(x_prompt, x_sample, norm1_g, w_in, diff_q_norm, diff_k_norm, lambda_q1, lambda_k1, lambda_q2,
           lambda_k2, diff_subln_g, rel_bias, na_q_norm, na_k_norm, na_rpb, w_branch_diff, w_branch_na,
           w_out, norm2_g, w_router, w_expert_gate, w_expert_up, w_expert_down):
    depth, d_model, _ = w_in.shape
    n_experts = w_router.shape[-1]
    ff = w_expert_gate.shape[-1]

    w_in_b = w_in.astype(BF16)
    wbd_b = w_branch_diff.astype(BF16)
    wbn_b = w_branch_na.astype(BF16)
    wo_b = w_out.astype(BF16)
    wg_b = w_expert_gate.astype(BF16)
    wu_b = w_expert_up.astype(BF16)
    wd_b = w_expert_down.astype(BF16)
    wr_t = jnp.swapaxes(w_router.astype(F32), 1, 2)
    wr_hi = wr_t.astype(BF16)
    wr_lo = (wr_t - wr_hi.astype(F32)).astype(BF16)
    g1 = norm1_g.astype(F32)[:, None, :]
    g2 = norm2_g.astype(F32)[:, None, :]
    tile_gain = lambda g, reps, scale: (jnp.tile(g.astype(F32), (1, reps)) * scale)[:, None, :]
    dqg = tile_gain(diff_q_norm, 2 * DIFF_HEADS, DIFF_HEAD_DIM ** -0.5 * LOG2E)
    dkg = tile_gain(diff_k_norm, 2 * DIFF_HEADS, 1.0)
    nqg = tile_gain(na_q_norm, NA_HEADS, NA_HEAD_DIM ** -0.5 * LOG2E)
    nkg = tile_gain(na_k_norm, NA_HEADS, 1.0)
    subln = diff_subln_g.astype(F32)[:, None, :]
    lam_params = jnp.stack([lambda_q1, lambda_k1, lambda_q2, lambda_k2], axis=1).astype(F32)
    seg_id = jnp.arange(MXU_DIM) // DIFF_HEAD_DIM
    seg = jnp.where(seg_id[:, None] == seg_id[None, :], 1.0 / DIFF_HEAD_DIM, 0.0).astype(BF16)
    na_bias = [_na_bias(na_rpb[l]) for l in range(depth)]

    def run(x):
        b, s, _ = x.shape
        n = b * s
        cap = CAPACITY_FACTOR * n // n_experts
        tm = _pick_tile(n, 512)
        t_attn = _pick_tile(s, 1024)
        t5_tiles = _t5_bias_tiles(rel_bias, t_attn, min(s, MXU_DIM))
        x2d = x.reshape(n, d_model)
        for l in range(depth):
            lambda_init = 0.8 - 0.6 * math.exp(-0.3 * l)
            dq, dk, dv, nq, nk, nv, gates = _in_proj(x2d, l, g1, w_in_b, seg, dqg, dkg, nqg, nkg, tm)
            r3 = lambda a: a.reshape(b, s, a.shape[-1])
            yd = _diff_attention(r3(dq), r3(dk), r3(dv), t5_tiles, lam_params, subln, l, lambda_init, t_attn)
            yn = _na_attention(r3(nq), r3(nk), r3(nv), na_bias[l], NA_ROWS_PER_STEP)
            x1, h2, aff_t = _merge_out(x2d, yd.reshape(n, DIFF_WIDTH), yn.reshape(n, NA_WIDTH), gates,
                                       wbd_b, wbn_b, wo_b, g2, wr_hi, wr_lo, l, tm)
            pos, tb128 = _select(aff_t, cap)
            per = ROUTE_TILE // LANES
            tb_tiles = jnp.concatenate([tb128[:, ::per], jnp.full((n_experts, 1), cap, jnp.int32)], axis=1)
            counts = tb_tiles[:, 1:] - tb_tiles[:, :-1]
            n_rounds = jnp.maximum(jnp.max(-(-counts // ROUTE_CHUNK), axis=0), 1).astype(jnp.int32)
            tb_t = jnp.pad(tb_tiles.T.astype(F32), ((0, 7), (0, LANES - n_experts)))
            pos_t = jnp.pad(pos.T.astype(F32), ((0, 0), (0, LANES - n_experts)), constant_values=-1.0)
            xe, gate_rows = _dispatch(h2, pos, aff_t, tb_tiles, n_rounds, cap)
            ye = _expert_ffn(xe, gate_rows, wg_b, wu_b, wd_b, l, cap, _pick_tile(cap, 512), EXPERT_FF_CHUNK)
            x2d = _combine(x1, pos_t, tb_tiles, tb_t, n_rounds, ye, cap)
        return x2d.reshape(b, s, d_model)

    return (run(x_prompt), run(x_sample))
```

```python
import functools
import math

import numpy as np
import jax
import jax.numpy as jnp
from jax import lax
from jax.experimental import pallas as pl
from jax.experimental.pallas import tpu as pltpu

F32 = jnp.float32
BF16 = jnp.bfloat16

DIFF_HEADS = 4
DIFF_HEAD_DIM = 64
DIFF_WIDTH = DIFF_HEADS * 2 * DIFF_HEAD_DIM
NA_HEADS = 8
NA_HEAD_DIM = 64
NA_WIDTH = NA_HEADS * NA_HEAD_DIM
GRID_W = 64
NA_WIN_ROWS = 8
NA_WIN_COLS = 16
T5_BUCKETS = 32
T5_MAX_DIST = 128
CAPACITY_FACTOR = 2
EPS = 1e-6

LANES = 128
MXU_DIM = 256
VMEM_LIMIT_BYTES = 56 * 1024 * 1024
NEG_BIG = -1e30
LOG2E = math.log2(math.e)
NA_ROWS_PER_STEP = 16
BF16_ROWS = 16
ROUTE_TILE = 512
ROUTE_CHUNK = 96
COMBINE_CHUNK = ROUTE_CHUNK + BF16_ROWS
DISPATCH_WIN = ROUTE_CHUNK + BF16_ROWS


def _cparams(*sem):
    return pltpu.CompilerParams(dimension_semantics=sem, vmem_limit_bytes=VMEM_LIMIT_BYTES)


def _layer_block(a, layer):
    return pl.BlockSpec((1,) + a.shape[1:], lambda *_: (layer,) + (0,) * (a.ndim - 1))


def _nt_dot(a, b):
    return lax.dot_general(a, b, (((1,), (1,)), ((), ())), preferred_element_type=F32)


def _inproj_kernel(x_ref, g_ref, w_ref, seg_ref, dqg_ref, dkg_ref, nqg_ref, nkg_ref,
                   dq_ref, dk_ref, dv_ref, nq_ref, nk_ref, nv_ref, gate_ref, *, layer, gate_width):
    x = x_ref[...]
    ms = jnp.mean(x * x, axis=-1, keepdims=True)
    h = (x * lax.rsqrt(ms + EPS) * g_ref[layer]).astype(BF16)
    width = DIFF_WIDTH

    def proj(c0):
        return jnp.dot(h, w_ref[0, :, c0:c0 + width], preferred_element_type=F32)

    def head_norm(a, gain):
        sq = (a * a).astype(BF16)
        halves = [jnp.dot(sq[:, c:c + MXU_DIM], seg_ref[...], preferred_element_type=F32)
                  for c in range(0, width, MXU_DIM)]
        ms_seg = jnp.concatenate(halves, axis=1)
        return a * lax.rsqrt(ms_seg + EPS) * gain

    dq_ref[...] = head_norm(proj(0), dqg_ref[layer]).astype(BF16)
    dk_ref[...] = head_norm(proj(width), dkg_ref[layer]).astype(BF16)
    dv_ref[...] = proj(2 * width).astype(BF16)
    nq_ref[...] = head_norm(proj(3 * width), nqg_ref[layer]).astype(BF16)
    nk_ref[...] = head_norm(proj(4 * width), nkg_ref[layer]).astype(BF16)
    nv_ref[...] = proj(5 * width).astype(BF16)
    for c in range(0, gate_width, width):
        a = proj(6 * width + c)
        gate_ref[:, c:c + width] = (1.0 / (1.0 + jnp.exp(-a))).astype(BF16)


def _in_proj(x2d, layer, norm_g, w_in, seg, dqg, dkg, nqg, nkg, tm):
    n, d = x2d.shape
    in_width = w_in.shape[-1]
    gate_width = in_width - 6 * DIFF_WIDTH
    whole = lambda a: pl.BlockSpec(a.shape, lambda i: (0,) * a.ndim)
    row = lambda w: pl.BlockSpec((tm, w), lambda i: (i, 0))
    outs = [jax.ShapeDtypeStruct((n, DIFF_WIDTH), BF16)] * 6 + [jax.ShapeDtypeStruct((n, gate_width), BF16)]
    return pl.pallas_call(
        functools.partial(_inproj_kernel, layer=layer, gate_width=gate_width),
        grid=(n // tm,),
        in_specs=[row(d), whole(norm_g), _layer_block(w_in, layer), whole(seg), whole(dqg), whole(dkg),
                  whole(nqg), whole(nkg)],
        out_specs=[row(DIFF_WIDTH)] * 6 + [row(gate_width)],
        out_shape=outs,
        compiler_params=_cparams("parallel"),
    )(x2d, norm_g, w_in, seg, dqg, dkg, nqg, nkg)


def _diff_attn_kernel(q_ref, k_ref, v_ref, bias_ref, lam_ref, g_ref, o_ref,
                      m1_ref, a1_ref, m2_ref, a2_ref, *, layer, lambda_init):
    qi = pl.program_id(2)
    hw = 2 * DIFF_HEAD_DIM
    tq = q_ref.shape[1]
    seq = k_ref.shape[1]
    sub = bias_ref.shape[-1]

    q = q_ref[0]
    lane = lax.broadcasted_iota(jnp.int32, q.shape, 1)
    zero = jnp.zeros_like(q)
    q1 = jnp.where(lane < DIFF_HEAD_DIM, q, zero)
    q2 = jnp.where(lane >= DIFF_HEAD_DIM, q, zero)

    def update(qm, k, v_ones, bias, m_ref, a_ref, first):
        s = _nt_dot(qm, k) + bias
        m_cur = jnp.max(s, axis=-1, keepdims=True)
        if first:
            m_new = jnp.broadcast_to(m_cur, (tq, LANES))
        else:
            m_prev = m_ref[...]
            m_new = jnp.maximum(m_prev, m_cur)
        p = jnp.exp2(s - jnp.concatenate([m_new] * (sub // LANES), axis=1))
        pv = jnp.dot(p.astype(BF16), v_ones, preferred_element_type=F32)
        if first:
            a_ref[...] = pv
        else:
            alpha = jnp.exp2(m_prev - m_new)
            a_ref[...] = jnp.concatenate([alpha, alpha], axis=1) * a_ref[...] + pv
        m_ref[...] = m_new

    d_lo, _ = _t5_tile_range(tq, sub)
    n_tiles = bias_ref.shape[1]
    for c in range(seq // sub):
        k = k_ref[0, c * sub:(c + 1) * sub]
        v = v_ref[0, c * sub:(c + 1) * sub]
        v_ones = jnp.concatenate([v, jnp.ones_like(v)], axis=1)
        bias = bias_ref[0, jnp.clip(c - qi * (tq // sub) - d_lo, 0, n_tiles - 1)]
        update(q1, k, v_ones, bias, m1_ref, a1_ref, c == 0)
        update(q2, k, v_ones, bias, m2_ref, a2_ref, c == 0)

    lp = lam_ref[layer]
    lam = (jnp.exp(jnp.sum(lp[0:1] * lp[1:2], axis=-1, keepdims=True))
           - jnp.exp(jnp.sum(lp[2:3] * lp[3:4], axis=-1, keepdims=True)) + lambda_init)
    o = a1_ref[:, :hw] / a1_ref[:, hw:] - lam * (a2_ref[:, :hw] / a2_ref[:, hw:])
    ms = jnp.mean(o * o, axis=-1, keepdims=True)
    o = o * lax.rsqrt(ms + EPS) * g_ref[layer] * (1.0 - lambda_init)
    o_ref[0] = o.astype(BF16)


def _diff_attention(dq, dk, dv, bias_tiles, lam_params, subln_g, layer, lambda_init, t):
    b, s, _ = dq.shape
    hw = 2 * DIFF_HEAD_DIM
    qspec = pl.BlockSpec((1, t, hw), lambda h, bi, qi: (bi, qi, h))
    kspec = pl.BlockSpec((1, s, hw), lambda h, bi, qi: (bi, 0, h))
    bspec = pl.BlockSpec((1,) + bias_tiles.shape[1:], lambda h, bi, qi: (h, 0, 0, 0))
    whole = lambda a: pl.BlockSpec(a.shape, lambda h, bi, qi: (0,) * a.ndim)
    return pl.pallas_call(
        functools.partial(_diff_attn_kernel, layer=layer, lambda_init=lambda_init),
        grid=(DIFF_HEADS, b, s // t),
        in_specs=[qspec, kspec, kspec, bspec, whole(lam_params), whole(subln_g)],
        out_specs=qspec,
        out_shape=jax.ShapeDtypeStruct((b, s, DIFF_WIDTH), BF16),
        scratch_shapes=[pltpu.VMEM((t, LANES), F32), pltpu.VMEM((t, 2 * hw), F32),
                        pltpu.VMEM((t, LANES), F32), pltpu.VMEM((t, 2 * hw), F32)],
        compiler_params=_cparams("parallel", "parallel", "arbitrary"),
    )(dq, dk, dv, bias_tiles, lam_params, subln_g)


def _t5_tile_range(tq, sub):
    d_lo = (-(T5_MAX_DIST - 1) - sub) // sub
    d_hi = -(-(tq + T5_MAX_DIST - 1) // sub)
    return d_lo, d_hi


def _t5_bucket(rel):
    nb = T5_BUCKETS // 2
    max_exact = nb // 2
    side = jnp.where(rel > 0, nb, 0)
    n = jnp.abs(rel)
    nf = jnp.maximum(n, 1).astype(F32)
    large = max_exact + (jnp.log(nf / max_exact) / math.log(T5_MAX_DIST / max_exact) * (nb - max_exact)).astype(jnp.int32)
    large = jnp.minimum(large, nb - 1)
    return side + jnp.where(n < max_exact, n, large)


def _t5_bias_tiles(rel_bias, tq, sub):
    d_lo, d_hi = _t5_tile_range(tq, sub)
    i = jnp.arange(tq, dtype=jnp.int32)[:, None]
    j = jnp.arange(sub, dtype=jnp.int32)[None, :]
    d = jnp.arange(d_lo, d_hi + 1, dtype=jnp.int32)[:, None, None]
    rel = d * sub + (j - i)[None]
    tiles = jnp.einsum("dqkb,bh->dqkh", jax.nn.one_hot(_t5_bucket(rel), T5_BUCKETS, dtype=F32),
                       rel_bias.astype(F32), precision=lax.Precision.HIGHEST)
    return jnp.transpose(tiles, (3, 0, 1, 2)) * LOG2E


def _na_kernel(q_ref, k_ref, v_ref, bias_ref, o_ref, *, rows, rows_per_step):
    nkeys = NA_WIN_ROWS * GRID_W
    lane = lax.broadcasted_iota(jnp.int32, (GRID_W, LANES), 1)
    first_half = lane < NA_HEAD_DIM
    for j in range(rows_per_step):
        r = pl.program_id(1) * rows_per_step + j
        rs = jnp.clip(r - NA_WIN_ROWS // 2, 0, rows - NA_WIN_ROWS)
        case = r - rs
        q = q_ref[0, j]
        kwin = k_ref[0, pl.ds(rs, NA_WIN_ROWS)].reshape(nkeys, NA_WIDTH)
        vwin = v_ref[0, pl.ds(rs, NA_WIN_ROWS)].reshape(nkeys, NA_WIDTH)
        for hp in range(NA_HEADS // 2):
            c0 = hp * LANES
            qp = q[:, c0:c0 + LANES]
            zero = jnp.zeros_like(qp)
            q_stack = jnp.concatenate([jnp.where(first_half, qp, zero), jnp.where(first_half, zero, qp)], axis=0)
            vp = vwin[:, c0:c0 + LANES]
            s = _nt_dot(q_stack, kwin[:, c0:c0 + LANES]) + bias_ref[case, hp]
            m = jnp.max(s, axis=-1, keepdims=True)
            p = jnp.exp2(s - m)
            res = jnp.dot(p.astype(BF16), jnp.concatenate([vp, jnp.ones_like(vp)], axis=1),
                          preferred_element_type=F32)
            o = res[:, :LANES] / res[:, LANES:]
            o_ref[0, j, :, c0:c0 + LANES] = jnp.where(first_half, o[:GRID_W], o[GRID_W:]).astype(BF16)


def _na_attention(nq, nk, nv, bias_full, rows_per_step):
    b, s, _ = nq.shape
    rows = s // GRID_W
    assert rows >= NA_WIN_ROWS and rows % rows_per_step == 0
    shp = (b, rows, GRID_W, NA_WIDTH)
    qspec = pl.BlockSpec((1, rows_per_step, GRID_W, NA_WIDTH), lambda bi, r: (bi, r, 0, 0))
    kspec = pl.BlockSpec((1, rows, GRID_W, NA_WIDTH), lambda bi, r: (bi, 0, 0, 0))
    bspec = pl.BlockSpec(bias_full.shape, lambda bi, r: (0, 0, 0, 0))
    out = pl.pallas_call(
        functools.partial(_na_kernel, rows=rows, rows_per_step=rows_per_step),
        grid=(b, rows // rows_per_step),
        in_specs=[qspec, kspec, kspec, bspec],
        out_specs=qspec,
        out_shape=jax.ShapeDtypeStruct(shp, BF16),
        compiler_params=_cparams("parallel", "arbitrary"),
    )(nq.reshape(shp), nk.reshape(shp), nv.reshape(shp), bias_full)
    return out.reshape(b, s, NA_WIDTH)


def _na_bias(rpb):
    case = jnp.arange(NA_WIN_ROWS)[:, None]
    w = jnp.arange(NA_WIN_ROWS)[None, :]
    dr_idx = w - case + (NA_WIN_ROWS - 1)
    c = jnp.arange(GRID_W)[:, None]
    kc = jnp.arange(GRID_W)[None, :]
    cs = jnp.clip(c - NA_WIN_COLS // 2, 0, GRID_W - NA_WIN_COLS)
    valid = (kc >= cs) & (kc < cs + NA_WIN_COLS)
    dc_idx = jnp.clip(kc - c + NA_WIN_COLS - 1, 0, 2 * NA_WIN_COLS - 2)
    tab = rpb.astype(F32)[:, dr_idx]
    tab = jnp.einsum("hawd,ckd->hawck", tab, jax.nn.one_hot(dc_idx, 2 * NA_WIN_COLS - 1, dtype=F32),
                     precision=lax.Precision.HIGHEST)
    tab = jnp.where(valid[None, None, None], tab * LOG2E, NEG_BIG)
    tab = jnp.transpose(tab, (1, 0, 3, 2, 4))
    return tab.reshape(NA_WIN_ROWS, NA_HEADS // 2, 2 * GRID_W, NA_WIN_ROWS * GRID_W)


def _merge_kernel(x_ref, yd_ref, yn_ref, gate_ref, wbd_ref, wbn_ref, wo_ref, g2_ref, wrh_ref, wrl_ref,
                  x1_ref, h2_ref, aff_ref, *, layer):
    d = x_ref.shape[-1]
    bd = jnp.dot(yd_ref[...], wbd_ref[0], preferred_element_type=F32)
    bn = jnp.dot(yn_ref[...], wbn_ref[0], preferred_element_type=F32)
    merged = gate_ref[:, :d].astype(F32) * bd + gate_ref[:, d:].astype(F32) * bn
    x1 = x_ref[...] + jnp.dot(merged.astype(BF16), wo_ref[0], preferred_element_type=F32)
    x1_ref[...] = x1
    ms = jnp.mean(x1 * x1, axis=-1, keepdims=True)
    t = x1 * lax.rsqrt(ms + EPS) * g2_ref[layer]
    t_hi = t.astype(BF16)
    h2_ref[...] = t_hi
    t_lo = (t - t_hi.astype(F32)).astype(BF16)
    wrh = wrh_ref[layer]
    ne = wrh.shape[0]
    both = _nt_dot(jnp.concatenate([wrh, wrl_ref[layer]], axis=0), t_hi)
    logits = both[:ne] + _nt_dot(wrh, t_lo) + both[ne:]
    mx = jnp.max(logits, axis=0, keepdims=True)
    e = jnp.exp(logits - mx)
    aff_ref[...] = e / jnp.sum(e, axis=0, keepdims=True)


def _merge_out(x2d, yd, yn, gates, wbd, wbn, wo, g2, wrh, wrl, layer, tm):
    n, d = x2d.shape
    ne = wrh.shape[1]
    whole = lambda a: pl.BlockSpec(a.shape, lambda i: (0,) * a.ndim)
    row = lambda w: pl.BlockSpec((tm, w), lambda i: (i, 0))
    return pl.pallas_call(
        functools.partial(_merge_kernel, layer=layer),
        grid=(n // tm,),
        in_specs=[row(d), row(DIFF_WIDTH), row(NA_WIDTH), row(2 * d), _layer_block(wbd, layer),
                  _layer_block(wbn, layer), _layer_block(wo, layer), whole(g2), whole(wrh), whole(wrl)],
        out_specs=[row(d), row(d), pl.BlockSpec((ne, tm), lambda i: (0, i))],
        out_shape=[jax.ShapeDtypeStruct((n, d), F32), jax.ShapeDtypeStruct((n, d), BF16),
                   jax.ShapeDtypeStruct((ne, n), F32)],
        compiler_params=_cparams("parallel"),
    )(x2d, yd, yn, gates, wbd, wbn, wo, g2, wrh, wrl)


def _select_kernel(aff_ref, pos_ref, tb_ref, *, cap):
    ne, nt, _ = aff_ref.shape
    bits = lambda e: pltpu.bitcast(aff_ref[e], jnp.int32)

    def bit_step(i, thr):
        cand = thr | lax.shift_left(jnp.int32(1), 30 - i)
        rows = []
        for e in range(ne):
            ce = cand[e:e + 1]
            ge = (bits(e) >= ce).astype(jnp.int32)
            cnt = jnp.sum(jnp.sum(ge, axis=0, keepdims=True), axis=1, keepdims=True)
            rows.append(jnp.where(cnt >= cap, ce, thr[e:e + 1]))
        return jnp.concatenate(rows, axis=0)

    thr = lax.fori_loop(0, 31, bit_step, jnp.zeros((ne, LANES), jnp.int32))

    li = lax.broadcasted_iota(jnp.int32, (LANES, LANES), 0)
    lj = lax.broadcasted_iota(jnp.int32, (LANES, LANES), 1)
    upper = (li < lj).astype(BF16)
    ones = jnp.ones((LANES, LANES), BF16)
    ti = lax.broadcasted_iota(jnp.int32, (nt, nt), 0)
    tj = lax.broadcasted_iota(jnp.int32, (nt, nt), 1)
    lower = (tj < ti).astype(BF16)

    def prefix(flags):
        fb = flags.astype(BF16)
        within = jnp.dot(fb, upper, preferred_element_type=F32)
        per_tile = jnp.dot(fb, ones, preferred_element_type=F32)
        before = jnp.dot(lower, per_tile.astype(BF16), preferred_element_type=F32)
        return before + within, before

    for e in range(ne):
        be = bits(e)
        te = thr[e:e + 1]
        gt = be > te
        eq = be == te
        n_gt = jnp.sum(jnp.sum(gt.astype(F32), axis=0, keepdims=True), axis=1, keepdims=True)
        eq_rank, _ = prefix(eq)
        sel = gt | (eq & (eq_rank < cap - n_gt))
        pos, before = prefix(sel)
        pos_ref[e] = jnp.where(sel, pos, -1.0).astype(jnp.int32)
        tb_ref[e] = before


def _select(aff_t, cap):
    ne, n = aff_t.shape
    nt = n // LANES
    shp = (ne, nt, LANES)
    spec = pl.BlockSpec(shp, lambda i: (0, 0, 0))
    pos, tb = pl.pallas_call(
        functools.partial(_select_kernel, cap=cap),
        grid=(1,),
        in_specs=[spec],
        out_specs=[spec, spec],
        out_shape=[jax.ShapeDtypeStruct(shp, jnp.int32), jax.ShapeDtypeStruct(shp, F32)],
        compiler_params=_cparams("arbitrary"),
    )(aff_t.reshape(shp))
    return pos.reshape(ne, n), tb[:, :, 0].astype(jnp.int32)


def _dispatch_kernel(tb_ref, nr_ref, h_ref, pos_ref, aff_ref, xe_hbm, gc_hbm, xs_ref, gs_ref, last_ref, pend_ref,
                     sem, *, cap):
    j = pl.program_id(0)
    ne = pos_ref.shape[0]
    tt = h_ref.shape[0]

    @pl.when(j == 0)
    def _():
        xs_ref[...] = jnp.zeros(xs_ref.shape, xs_ref.dtype)
        gs_ref[...] = jnp.zeros(gs_ref.shape, F32)
        for e in range(ne):
            last_ref[e] = 0
            pend_ref[e] = 0
        pad = [(pltpu.make_async_copy(xs_ref.at[e], xe_hbm.at[e, pl.ds(cap + q * DISPATCH_WIN, DISPATCH_WIN)], sem.at[0, e]),
                pltpu.make_async_copy(gs_ref.at[e], gc_hbm.at[e, pl.ds(cap + q * DISPATCH_WIN, DISPATCH_WIN)], sem.at[1, e]))
               for e in range(ne) for q in range((xe_hbm.shape[1] - cap) // DISPATCH_WIN)]
        for cx, cg in pad:
            cx.start()
            cg.start()
        for cx, cg in pad:
            cx.wait()
            cg.wait()

    h = h_ref[...]
    slot = lax.broadcasted_iota(jnp.int32, (DISPATCH_WIN, tt), 0)
    row = lax.broadcasted_iota(jnp.int32, (BF16_ROWS, 1), 0)

    def copies(e, start):
        dst = pl.ds(pl.multiple_of(start, BF16_ROWS), DISPATCH_WIN)
        return (pltpu.make_async_copy(xs_ref.at[e], xe_hbm.at[e, dst], sem.at[0, e]),
                pltpu.make_async_copy(gs_ref.at[e], gc_hbm.at[e, dst], sem.at[1, e]))

    def drain():
        for e in range(ne):
            @pl.when(pend_ref[e] == 1)
            def _():
                for c in copies(e, 0):
                    c.wait()
                pend_ref[e] = 0

    def emit(e, start, xs_e, gate_col, n_carry, off):
        keep = row < n_carry
        gs = jnp.broadcast_to(gate_col, (DISPATCH_WIN, LANES))
        hd = BF16_ROWS
        x_head = xs_e[:hd] + jnp.where(keep, xs_ref[e, pl.ds(off, hd)].astype(F32), 0.0)
        g_head = gs[:hd] + jnp.where(keep, gs_ref[e, pl.ds(off, hd)], 0.0)
        xs_ref[e, hd:] = xs_e[hd:].astype(xs_ref.dtype)
        gs_ref[e, hd:] = gs[hd:]
        xs_ref[e, :hd] = x_head.astype(xs_ref.dtype)
        gs_ref[e, :hd] = g_head
        last_ref[e] = start
        for c in copies(e, start):
            c.start()
        pend_ref[e] = 1

    def gate_of(hit, e):
        return jnp.sum(jnp.where(hit, aff_ref[e:e + 1, :], 0.0), axis=1, keepdims=True)

    first = [tb_ref[e, j] for e in range(ne)]
    starts = [lax.shift_left(lax.shift_right_logical(first[e], 4), 4) for e in range(ne)]
    hits = [pos_ref[e:e + 1, :] - starts[e] == slot for e in range(ne)]
    xs = jnp.dot(jnp.concatenate([hit.astype(BF16) for hit in hits], axis=0), h, preferred_element_type=F32)
    gates = [gate_of(hits[e], e) for e in range(ne)]
    drain()
    for e in range(ne):
        emit(e, starts[e], xs[e * DISPATCH_WIN:(e + 1) * DISPATCH_WIN], gates[e], first[e] - starts[e],
             pl.multiple_of(starts[e] - last_ref[e], BF16_ROWS))

    def later_round(r, carry):
        for e in range(ne):
            @pl.when(tb_ref[e, j + 1] > first[e] + r * ROUTE_CHUNK)
            def _():
                start = starts[e] + r * ROUTE_CHUNK
                hit = pos_ref[e:e + 1, :] - start == slot
                xs_e = jnp.dot(hit.astype(BF16), h, preferred_element_type=F32)
                gate_col = gate_of(hit, e)

                @pl.when(pend_ref[e] == 1)
                def _():
                    for c in copies(e, 0):
                        c.wait()
                    pend_ref[e] = 0
                emit(e, start, xs_e, gate_col, 0, 0)
        return carry

    lax.fori_loop(1, nr_ref[j], later_round, 0)

    @pl.when(j == pl.num_programs(0) - 1)
    def _():
        drain()


def _dispatch(h2, pos, aff_t, tb_tiles, n_rounds, cap):
    n, d = h2.shape
    ne = pos.shape[0]
    tt = ROUTE_TILE
    assert cap % BF16_ROWS == 0
    cap_pad = cap + -(-tt // ROUTE_CHUNK) * DISPATCH_WIN
    grid_spec = pltpu.PrefetchScalarGridSpec(
        num_scalar_prefetch=2,
        grid=(n // tt,),
        in_specs=[pl.BlockSpec((tt, d), lambda j, *_: (j, 0)),
                  pl.BlockSpec((ne, tt), lambda j, *_: (0, j)),
                  pl.BlockSpec((ne, tt), lambda j, *_: (0, j))],
        out_specs=[pl.BlockSpec(memory_space=pl.ANY), pl.BlockSpec(memory_space=pl.ANY)],
        scratch_shapes=[pltpu.VMEM((ne, DISPATCH_WIN, d), BF16), pltpu.VMEM((ne, DISPATCH_WIN, LANES), F32),
                        pltpu.SMEM((ne,), jnp.int32), pltpu.SMEM((ne,), jnp.int32),
                        pltpu.SemaphoreType.DMA((2, ne))],
    )
    return pl.pallas_call(
        functools.partial(_dispatch_kernel, cap=cap),
        grid_spec=grid_spec,
        out_shape=[jax.ShapeDtypeStruct((ne, cap_pad, d), BF16), jax.ShapeDtypeStruct((ne, cap_pad, LANES), F32)],
        compiler_params=_cparams("arbitrary"),
    )(tb_tiles, n_rounds, h2, pos, aff_t)


def _combine_kernel(tb_ref, nr_ref, x_ref, post_ref, tbt_ref, spread_ref, kcol_ref, ye_hbm, o_ref, ch_ref, one_ref,
                    sem, *, cap):
    j = pl.program_id(0)
    ne = ye_hbm.shape[0]
    spread = spread_ref[...]
    kcol = kcol_ref[...]
    last = cap - COMBINE_CHUNK

    buf = j % 2

    def chunk_copies(tile, r, b):
        def start_of(e):
            s = tb_ref[e, tile] + r * ROUTE_CHUNK
            return jnp.minimum(lax.shift_left(lax.shift_right_logical(s, 4), 4), last)
        return [pltpu.make_async_copy(ye_hbm.at[e, pl.ds(pl.multiple_of(start_of(e), BF16_ROWS), COMBINE_CHUNK)],
                                      ch_ref.at[b, e], sem.at[b]) for e in range(ne)]

    @pl.when(j == 0)
    def _():
        for c in chunk_copies(0, 0, 0):
            c.start()

    @pl.when(j + 1 < pl.num_programs(0))
    def _():
        for c in chunk_copies(j + 1, 0, 1 - buf):
            c.start()

    def late_copy(e, r):
        s = tb_ref[e, j] + r * ROUTE_CHUNK
        start = jnp.minimum(lax.shift_left(lax.shift_right_logical(s, 4), 4), last)
        return pltpu.make_async_copy(ye_hbm.at[e, pl.ds(pl.multiple_of(start, BF16_ROWS), COMBINE_CHUNK)],
                                     one_ref.at[e, pl.ds(0, COMBINE_CHUNK)], sem.at[2 + e])

    def in_round(e, r):
        return tb_ref[e, j + 1] > tb_ref[e, j] + r * ROUTE_CHUNK

    @pl.when(j == 0)
    def _():
        one_ref[:, COMBINE_CHUNK:, :] = jnp.zeros((ne, LANES - COMBINE_CHUNK, one_ref.shape[-1]), one_ref.dtype)

    for e in range(ne):
        @pl.when(in_round(e, 1))
        def _():
            late_copy(e, 1).start()

    post = post_ref[...]

    def rel_slots(r):
        first = tbt_ref[pl.ds(j, 1), :] + (r * ROUTE_CHUNK).astype(F32)
        sv = jnp.minimum(jnp.floor(first * (1.0 / BF16_ROWS)) * BF16_ROWS, float(last))
        mine = (post >= first) & (post < first + ROUTE_CHUNK)
        return jnp.where(mine, post - sv, -1.0).astype(BF16)

    rel_wide = jnp.dot(rel_slots(jnp.int32(0)), spread, preferred_element_type=F32)
    onehot = (rel_wide == kcol).astype(BF16)
    for c in chunk_copies(j, 0, buf):
        c.wait()
    rows = ch_ref[buf].reshape(ne * COMBINE_CHUNK, ch_ref.shape[-1])
    o_ref[...] = x_ref[...] + jnp.dot(onehot, rows, preferred_element_type=F32)


    lane_row = lax.broadcasted_iota(jnp.int32, (LANES, LANES), 0)
    k_lane = lax.broadcasted_iota(jnp.int32, (1, LANES), 1).astype(F32)

    def later_round(r, carry):
        rel = rel_slots(r)
        for e in range(ne):
            @pl.when(tb_ref[e, j + 1] > tb_ref[e, j] + r * ROUTE_CHUNK)
            def _():
                cp = late_copy(e, r)

                @pl.when(r > 1)
                def _():
                    cp.start()
                pick = (lane_row == e).astype(BF16)
                rel_e = jnp.dot(rel, pick, preferred_element_type=F32)
                onehot_e = (rel_e == k_lane).astype(BF16)
                cp.wait()
                o_ref[...] += jnp.dot(onehot_e, one_ref[e], preferred_element_type=F32)
        return carry

    lax.fori_loop(1, nr_ref[j], later_round, 0)


def _combine(x1, pos_t, tb_tiles, tb_t, n_rounds, ye, cap):
    n, d = x1.shape
    ne = ye.shape[0]
    tt = ROUTE_TILE
    assert cap >= COMBINE_CHUNK and cap % BF16_ROWS == 0
    col = np.arange(ne * COMBINE_CHUNK)
    spread = jnp.asarray(np.arange(LANES)[:, None] == (col // COMBINE_CHUNK)[None, :], BF16)
    kcol = jnp.asarray((col % COMBINE_CHUNK)[None, :], F32)
    grid_spec = pltpu.PrefetchScalarGridSpec(
        num_scalar_prefetch=2,
        grid=(n // tt,),
        in_specs=[pl.BlockSpec((tt, d), lambda j, *_: (j, 0)),
                  pl.BlockSpec((tt, LANES), lambda j, *_: (j, 0)),
                  pl.BlockSpec(tb_t.shape, lambda j, *_: (0, 0)),
                  pl.BlockSpec(spread.shape, lambda j, *_: (0, 0)),
                  pl.BlockSpec(kcol.shape, lambda j, *_: (0, 0)),
                  pl.BlockSpec(memory_space=pl.ANY)],
        out_specs=pl.BlockSpec((tt, d), lambda j, *_: (j, 0)),
        scratch_shapes=[pltpu.VMEM((2, ne, COMBINE_CHUNK, d), BF16), pltpu.VMEM((ne, LANES, d), BF16),
                        pltpu.SemaphoreType.DMA((2 + ne,))],
    )
    return pl.pallas_call(
        functools.partial(_combine_kernel, cap=cap),
        grid_spec=grid_spec,
        out_shape=jax.ShapeDtypeStruct((n, d), F32),
        compiler_params=_cparams("arbitrary"),
    )(tb_tiles, n_rounds, x1, pos_t, tb_t, spread, kcol, ye)


def _expert_kernel(xe_ref, gate_ref, wg_ref, wu_ref, wd_ref, o_ref, *, ff_chunk):
    x = xe_ref[0]
    ff = wg_ref.shape[-1]
    acc = None
    for c in range(0, ff, ff_chunk):
        w = min(ff_chunk, ff - c)
        g = jnp.dot(x, wg_ref[0, 0, :, c:c + w], preferred_element_type=F32)
        u = jnp.dot(x, wu_ref[0, 0, :, c:c + w], preferred_element_type=F32)
        hmid = (g / (1.0 + jnp.exp(-g)) * u).astype(BF16)
        part = jnp.dot(hmid, wd_ref[0, 0, c:c + w, :], preferred_element_type=F32)
        acc = part if acc is None else acc + part
    gate = gate_ref[0]
    gate = jnp.concatenate([gate] * (acc.shape[1] // LANES), axis=1)
    o_ref[0] = (acc * gate).astype(o_ref.dtype)


def _expert_ffn(xe, gate, wg, wu, wd, layer, cap, tm, ff_chunk):
    ne, _, d = xe.shape
    ff = wg.shape[-1]
    return pl.pallas_call(
        functools.partial(_expert_kernel, ff_chunk=ff_chunk),
        grid=(ne, cap // tm),
        in_specs=[pl.BlockSpec((1, tm, d), lambda e, m: (e, m, 0)),
                  pl.BlockSpec((1, tm, LANES), lambda e, m: (e, m, 0)),
                  pl.BlockSpec((1, 1, d, ff), lambda e, m: (layer, e, 0, 0)),
                  pl.BlockSpec((1, 1, d, ff), lambda e, m: (layer, e, 0, 0)),
                  pl.BlockSpec((1, 1, ff, d), lambda e, m: (layer, e, 0, 0))],
        out_specs=pl.BlockSpec((1, tm, d), lambda e, m: (e, m, 0)),
        out_shape=jax.ShapeDtypeStruct((ne, cap, d), BF16),
        compiler_params=_cparams("parallel", "arbitrary"),
    )(xe, gate, wg, wu, wd)


def _pick_tile(n, pref):
    t = min(n, pref)
    assert n % t == 0
    return t


EXPERT_FF_CHUNK = 2 * MXU_DIM


def kernel(x_prompt, x_sample, norm1_g, w_in, diff_q_norm, diff_k_norm, lambda_q1, lambda_k1, lambda_q2,
           lambda_k2, diff_subln_g, rel_bias, na_q_norm, na_k_norm, na_rpb, w_branch_diff, w_branch_na,
           w_out, norm2_g, w_router, w_expert_gate, w_expert_up, w_expert_down):
    depth, d_model, _ = w_in.shape
    n_experts = w_router.shape[-1]
    ff = w_expert_gate.shape[-1]

    w_in_b = w_in.astype(BF16)
    wbd_b = w_branch_diff.astype(BF16)
    wbn_b = w_branch_na.astype(BF16)
    wo_b = w_out.astype(BF16)
    wg_b = w_expert_gate.astype(BF16)
    wu_b = w_expert_up.astype(BF16)
    wd_b = w_expert_down.astype(BF16)
    wr_t = jnp.swapaxes(w_router.astype(F32), 1, 2)
    wr_hi = wr_t.astype(BF16)
    wr_lo = (wr_t - wr_hi.astype(F32)).astype(BF16)
    g1 = norm1_g.astype(F32)[:, None, :]
    g2 = norm2_g.astype(F32)[:, None, :]
    tile_gain = lambda g, reps, scale: (jnp.tile(g.astype(F32), (1, reps)) * scale)[:, None, :]
    dqg = tile_gain(diff_q_norm, 2 * DIFF_HEADS, DIFF_HEAD_DIM ** -0.5 * LOG2E)
    dkg = tile_gain(diff_k_norm, 2 * DIFF_HEADS, 1.0)
    nqg = tile_gain(na_q_norm, NA_HEADS, NA_HEAD_DIM ** -0.5 * LOG2E)
    nkg = tile_gain(na_k_norm, NA_HEADS, 1.0)
    subln = diff_subln_g.astype(F32)[:, None, :]
    lam_params = jnp.stack([lambda_q1, lambda_k1, lambda_q2, lambda_k2], axis=1).astype(F32)
    seg_id = jnp.arange(MXU_DIM) // DIFF_HEAD_DIM
    seg = jnp.where(seg_id[:, None] == seg_id[None, :], 1.0 / DIFF_HEAD_DIM, 0.0).astype(BF16)
    na_bias = [_na_bias(na_rpb[l]) for l in range(depth)]

    def run(x):
        b, s, _ = x.shape
        n = b * s
        cap = CAPACITY_FACTOR * n // n_experts
        tm = _pick_tile(n, 512)
        t_attn = _pick_tile(s, 1024)
        t5_tiles = _t5_bias_tiles(rel_bias, t_attn, min(s, MXU_DIM))
        x2d = x.reshape(n, d_model)
        for l in range(depth):
            lambda_init = 0.8 - 0.6 * math.exp(-0.3 * l)
            dq, dk, dv, nq, nk, nv, gates = _in_proj(x2d, l, g1, w_in_b, seg, dqg, dkg, nqg, nkg, tm)
            r3 = lambda a: a.reshape(b, s, a.shape[-1])
            yd = _diff_attention(r3(dq), r3(dk), r3(dv), t5_tiles, lam_params, subln, l, lambda_init, t_attn)
            yn = _na_attention(r3(nq), r3(nk), r3(nv), na_bias[l], NA_ROWS_PER_STEP)
            x1, h2, aff_t = _merge_out(x2d, yd.reshape(n, DIFF_WIDTH), yn.reshape(n, NA_WIDTH), gates,
                                       wbd_b, wbn_b, wo_b, g2, wr_hi, wr_lo, l, tm)
            pos, tb128 = _select(aff_t, cap)
            per = ROUTE_TILE // LANES
            tb_tiles = jnp.concatenate([tb128[:, ::per], jnp.full((n_experts, 1), cap, jnp.int32)], axis=1)
            counts = tb_tiles[:, 1:] - tb_tiles[:, :-1]
            n_rounds = jnp.maximum(jnp.max(-(-counts // ROUTE_CHUNK), axis=0), 1).astype(jnp.int32)
            tb_t = jnp.pad(tb_tiles.T.astype(F32), ((0, 7), (0, LANES - n_experts)))
            pos_t = jnp.pad(pos.T.astype(F32), ((0, 0), (0, LANES - n_experts)), constant_values=-1.0)
            xe, gate_rows = _dispatch(h2, pos, aff_t, tb_tiles, n_rounds, cap)
            ye = _expert_ffn(xe, gate_rows, wg_b, wu_b, wd_b, l, cap, _pick_tile(cap, 512), EXPERT_FF_CHUNK)
            x2d = _combine(x1, pos_t, tb_tiles, tb_t, n_rounds, ye, cap)
        return x2d.reshape(b, s, d_model)

    return (run(x_prompt), run(x_sample))
```

```python
import functools
import math

import numpy as np
import jax
import jax.numpy as jnp
from jax import lax
from jax.experimental import pallas as pl
from jax.experimental.pallas import tpu as pltpu

F32 = jnp.float32
BF16 = jnp.bfloat16

DIFF_HEADS = 4
DIFF_HEAD_DIM = 64
DIFF_WIDTH = DIFF_HEADS * 2 * DIFF_HEAD_DIM
NA_HEADS = 8
NA_HEAD_DIM = 64
NA_WIDTH = NA_HEADS * NA_HEAD_DIM
GRID_W = 64
NA_WIN_ROWS = 8
NA_WIN_COLS = 16
T5_BUCKETS = 32
T5_MAX_DIST = 128
CAPACITY_FACTOR = 2
EPS = 1e-6

LANES = 128
MXU_DIM = 256
VMEM_LIMIT_BYTES = 56 * 1024 * 1024
NEG_BIG = -1e30
LOG2E = math.log2(math.e)
NA_ROWS_PER_STEP = 16
BF16_ROWS = 16
ROUTE_TILE = 512
ROUTE_CHUNK = 96
COMBINE_CHUNK = ROUTE_CHUNK + BF16_ROWS
DISPATCH_WIN = ROUTE_CHUNK + BF16_ROWS


def _cparams(*sem):
    return pltpu.CompilerParams(dimension_semantics=sem, vmem_limit_bytes=VMEM_LIMIT_BYTES)


def _layer_block(a, layer):
    return pl.BlockSpec((1,) + a.shape[1:], lambda *_: (layer,) + (0,) * (a.ndim - 1), pipeline_mode=pl.Buffered(1))


def _nt_dot(a, b):
    return lax.dot_general(a, b, (((1,), (1,)), ((), ())), preferred_element_type=F32)


def _inproj_kernel(x_ref, g_ref, w_ref, seg_ref, dqg_ref, dkg_ref, nqg_ref, nkg_ref,
                   dq_ref, dk_ref, dv_ref, nq_ref, nk_ref, nv_ref, gate_ref, *, layer, gate_width):
    x = x_ref[...]
    ms = jnp.mean(x * x, axis=-1, keepdims=True)
    h = (x * lax.rsqrt(ms + EPS) * g_ref[layer]).astype(BF16)
    width = DIFF_WIDTH

    def proj(c0):
        return jnp.dot(h, w_ref[0, :, c0:c0 + width], preferred_element_type=F32)

    def head_norm(a, gain):
        sq = (a * a).astype(BF16)
        halves = [jnp.dot(sq[:, c:c + MXU_DIM], seg_ref[...], preferred_element_type=F32)
                  for c in range(0, width, MXU_DIM)]
        ms_seg = jnp.concatenate(halves, axis=1)
        return a * lax.rsqrt(ms_seg + EPS) * gain

    dq_ref[...] = head_norm(proj(0), dqg_ref[layer]).astype(BF16)
    dk_ref[...] = head_norm(proj(width), dkg_ref[layer]).astype(BF16)
    dv_ref[...] = proj(2 * width).astype(BF16)
    nq_ref[...] = head_norm(proj(3 * width), nqg_ref[layer]).astype(BF16)
    nk_ref[...] = head_norm(proj(4 * width), nkg_ref[layer]).astype(BF16)
    nv_ref[...] = proj(5 * width).astype(BF16)
    for c in range(0, gate_width, width):
        a = proj(6 * width + c)
        gate_ref[:, c:c + width] = (1.0 / (1.0 + jnp.exp(-a))).astype(BF16)


def _in_proj(x2d, layer, norm_g, w_in, seg, dqg, dkg, nqg, nkg, tm):
    n, d = x2d.shape
    in_width = w_in.shape[-1]
    gate_width = in_width - 6 * DIFF_WIDTH
    whole = lambda a: pl.BlockSpec(a.shape, lambda i: (0,) * a.ndim)
    row = lambda w: pl.BlockSpec((tm, w), lambda i: (i, 0))
    outs = [jax.ShapeDtypeStruct((n, DIFF_WIDTH), BF16)] * 6 + [jax.ShapeDtypeStruct((n, gate_width), BF16)]
    return pl.pallas_call(
        functools.partial(_inproj_kernel, layer=layer, gate_width=gate_width),
        grid=(n // tm,),
        in_specs=[row(d), whole(norm_g), _layer_block(w_in, layer), whole(seg), whole(dqg), whole(dkg),
                  whole(nqg), whole(nkg)],
        out_specs=[row(DIFF_WIDTH)] * 6 + [row(gate_width)],
        out_shape=outs,
        compiler_params=_cparams("parallel"),
    )(x2d, norm_g, w_in, seg, dqg, dkg, nqg, nkg)


def _diff_attn_kernel(q_ref, k_ref, v_ref, bias_ref, lam_ref, g_ref, o_ref,
                      m1_ref, a1_ref, m2_ref, a2_ref, *, layer, lambda_init):
    qi = pl.program_id(2)
    hw = 2 * DIFF_HEAD_DIM
    tq = q_ref.shape[1]
    seq = k_ref.shape[1]
    sub = bias_ref.shape[-1]

    q = q_ref[0]
    lane = lax.broadcasted_iota(jnp.int32, q.shape, 1)
    zero = jnp.zeros_like(q)
    q1 = jnp.where(lane < DIFF_HEAD_DIM, q, zero)
    q2 = jnp.where(lane >= DIFF_HEAD_DIM, q, zero)

    def update(qm, k, v_ones, bias, m_ref, a_ref, first):
        s = _nt_dot(qm, k) + bias
        m_cur = jnp.max(s, axis=-1, keepdims=True)
        if first:
            m_new = jnp.broadcast_to(m_cur, (tq, LANES))
        else:
            m_prev = m_ref[...]
            m_new = jnp.maximum(m_prev, m_cur)
        p = jnp.exp2(s - jnp.concatenate([m_new] * (sub // LANES), axis=1))
        pv = jnp.dot(p.astype(BF16), v_ones, preferred_element_type=F32)
        if first:
            a_ref[...] = pv
        else:
            alpha = jnp.exp2(m_prev - m_new)
            a_ref[...] = jnp.concatenate([alpha, alpha], axis=1) * a_ref[...] + pv
        m_ref[...] = m_new

    d_lo, _ = _t5_tile_range(tq, sub)
    n_tiles = bias_ref.shape[1]
    for c in range(seq // sub):
        k = k_ref[0, c * sub:(c + 1) * sub]
        v = v_ref[0, c * sub:(c + 1) * sub]
        v_ones = jnp.concatenate([v, jnp.ones_like(v)], axis=1)
        bias = bias_ref[0, jnp.clip(c - qi * (tq // sub) - d_lo, 0, n_tiles - 1)]
        update(q1, k, v_ones, bias, m1_ref, a1_ref, c == 0)
        update(q2, k, v_ones, bias, m2_ref, a2_ref, c == 0)

    lp = lam_ref[layer]
    lam = (jnp.exp(jnp.sum(lp[0:1] * lp[1:2], axis=-1, keepdims=True))
           - jnp.exp(jnp.sum(lp[2:3] * lp[3:4], axis=-1, keepdims=True)) + lambda_init)
    o = a1_ref[:, :hw] / a1_ref[:, hw:] - lam * (a2_ref[:, :hw] / a2_ref[:, hw:])
    ms = jnp.mean(o * o, axis=-1, keepdims=True)
    o = o * lax.rsqrt(ms + EPS) * g_ref[layer] * (1.0 - lambda_init)
    o_ref[0] = o.astype(BF16)


def _diff_attention(dq, dk, dv, bias_tiles, lam_params, subln_g, layer, lambda_init, t):
    b, s, _ = dq.shape
    hw = 2 * DIFF_HEAD_DIM
    qspec = pl.BlockSpec((1, t, hw), lambda h, bi, qi: (bi, qi, h))
    kspec = pl.BlockSpec((1, s, hw), lambda h, bi, qi: (bi, 0, h))
    bspec = pl.BlockSpec((1,) + bias_tiles.shape[1:], lambda h, bi, qi: (h, 0, 0, 0))
    whole = lambda a: pl.BlockSpec(a.shape, lambda h, bi, qi: (0,) * a.ndim)
    return pl.pallas_call(
        functools.partial(_diff_attn_kernel, layer=layer, lambda_init=lambda_init),
        grid=(DIFF_HEADS, b, s // t),
        in_specs=[qspec, kspec, kspec, bspec, whole(lam_params), whole(subln_g)],
        out_specs=qspec,
        out_shape=jax.ShapeDtypeStruct((b, s, DIFF_WIDTH), BF16),
        scratch_shapes=[pltpu.VMEM((t, LANES), F32), pltpu.VMEM((t, 2 * hw), F32),
                        pltpu.VMEM((t, LANES), F32), pltpu.VMEM((t, 2 * hw), F32)],
        compiler_params=_cparams("parallel", "parallel", "arbitrary"),
    )(dq, dk, dv, bias_tiles, lam_params, subln_g)


def _t5_tile_range(tq, sub):
    d_lo = (-(T5_MAX_DIST - 1) - sub) // sub
    d_hi = -(-(tq + T5_MAX_DIST - 1) // sub)
    return d_lo, d_hi


def _t5_bucket(rel):
    nb = T5_BUCKETS // 2
    max_exact = nb // 2
    side = jnp.where(rel > 0, nb, 0)
    n = jnp.abs(rel)
    nf = jnp.maximum(n, 1).astype(F32)
    large = max_exact + (jnp.log(nf / max_exact) / math.log(T5_MAX_DIST / max_exact) * (nb - max_exact)).astype(jnp.int32)
    large = jnp.minimum(large, nb - 1)
    return side + jnp.where(n < max_exact, n, large)


def _t5_bias_tiles(rel_bias, tq, sub):
    d_lo, d_hi = _t5_tile_range(tq, sub)
    i = jnp.arange(tq, dtype=jnp.int32)[:, None]
    j = jnp.arange(sub, dtype=jnp.int32)[None, :]
    d = jnp.arange(d_lo, d_hi + 1, dtype=jnp.int32)[:, None, None]
    rel = d * sub + (j - i)[None]
    tiles = jnp.einsum("dqkb,bh->dqkh", jax.nn.one_hot(_t5_bucket(rel), T5_BUCKETS, dtype=F32),
                       rel_bias.astype(F32), precision=lax.Precision.HIGHEST)
    return jnp.transpose(tiles, (3, 0, 1, 2)) * LOG2E


def _na_kernel(q_ref, k_ref, v_ref, bias_ref, o_ref, *, rows, rows_per_step):
    nkeys = NA_WIN_ROWS * GRID_W
    lane = lax.broadcasted_iota(jnp.int32, (GRID_W, LANES), 1)
    first_half = lane < NA_HEAD_DIM
    for j in range(rows_per_step):
        r = pl.program_id(1) * rows_per_step + j
        rs = jnp.clip(r - NA_WIN_ROWS // 2, 0, rows - NA_WIN_ROWS)
        case = r - rs
        q = q_ref[0, j]
        kwin = k_ref[0, pl.ds(rs, NA_WIN_ROWS)].reshape(nkeys, NA_WIDTH)
        vwin = v_ref[0, pl.ds(rs, NA_WIN_ROWS)].reshape(nkeys, NA_WIDTH)
        for hp in range(NA_HEADS // 2):
            c0 = hp * LANES
            qp = q[:, c0:c0 + LANES]
            zero = jnp.zeros_like(qp)
            q_stack = jnp.concatenate([jnp.where(first_half, qp, zero), jnp.where(first_half, zero, qp)], axis=0)
            vp = vwin[:, c0:c0 + LANES]
            s = _nt_dot(q_stack, kwin[:, c0:c0 + LANES]) + bias_ref[case, hp]
            m = jnp.max(s, axis=-1, keepdims=True)
            p = jnp.exp2(s - m)
            res = jnp.dot(p.astype(BF16), jnp.concatenate([vp, jnp.ones_like(vp)], axis=1),
                          preferred_element_type=F32)
            o = res[:, :LANES] / res[:, LANES:]
            o_ref[0, j, :, c0:c0 + LANES] = jnp.where(first_half, o[:GRID_W], o[GRID_W:]).astype(BF16)


def _na_attention(nq, nk, nv, bias_full, rows_per_step):
    b, s, _ = nq.shape
    rows = s // GRID_W
    assert rows >= NA_WIN_ROWS and rows % rows_per_step == 0
    shp = (b, rows, GRID_W, NA_WIDTH)
    qspec = pl.BlockSpec((1, rows_per_step, GRID_W, NA_WIDTH), lambda bi, r: (bi, r, 0, 0))
    kspec = pl.BlockSpec((1, rows, GRID_W, NA_WIDTH), lambda bi, r: (bi, 0, 0, 0))
    bspec = pl.BlockSpec(bias_full.shape, lambda bi, r: (0, 0, 0, 0))
    out = pl.pallas_call(
        functools.partial(_na_kernel, rows=rows, rows_per_step=rows_per_step),
        grid=(b, rows // rows_per_step),
        in_specs=[qspec, kspec, kspec, bspec],
        out_specs=qspec,
        out_shape=jax.ShapeDtypeStruct(shp, BF16),
        compiler_params=_cparams("parallel", "arbitrary"),
    )(nq.reshape(shp), nk.reshape(shp), nv.reshape(shp), bias_full)
    return out.reshape(b, s, NA_WIDTH)


def _na_bias(rpb):
    case = jnp.arange(NA_WIN_ROWS)[:, None]
    w = jnp.arange(NA_WIN_ROWS)[None, :]
    dr_idx = w - case + (NA_WIN_ROWS - 1)
    c = jnp.arange(GRID_W)[:, None]
    kc = jnp.arange(GRID_W)[None, :]
    cs = jnp.clip(c - NA_WIN_COLS // 2, 0, GRID_W - NA_WIN_COLS)
    valid = (kc >= cs) & (kc < cs + NA_WIN_COLS)
    dc_idx = jnp.clip(kc - c + NA_WIN_COLS - 1, 0, 2 * NA_WIN_COLS - 2)
    tab = rpb.astype(F32)[:, dr_idx]
    tab = jnp.einsum("hawd,ckd->hawck", tab, jax.nn.one_hot(dc_idx, 2 * NA_WIN_COLS - 1, dtype=F32),
                     precision=lax.Precision.HIGHEST)
    tab = jnp.where(valid[None, None, None], tab * LOG2E, NEG_BIG)
    tab = jnp.transpose(tab, (1, 0, 3, 2, 4))
    return tab.reshape(NA_WIN_ROWS, NA_HEADS // 2, 2 * GRID_W, NA_WIN_ROWS * GRID_W)


def _merge_kernel(x_ref, yd_ref, yn_ref, gate_ref, wbd_ref, wbn_ref, wo_ref, g2_ref, wrh_ref, wrl_ref,
                  x1_ref, h2_ref, aff_ref, *, layer):
    d = x_ref.shape[-1]
    bd = jnp.dot(yd_ref[...], wbd_ref[0], preferred_element_type=F32)
    bn = jnp.dot(yn_ref[...], wbn_ref[0], preferred_element_type=F32)
    merged = gate_ref[:, :d].astype(F32) * bd + gate_ref[:, d:].astype(F32) * bn
    x1 = x_ref[...] + jnp.dot(merged.astype(BF16), wo_ref[0], preferred_element_type=F32)
    x1_ref[...] = x1
    ms = jnp.mean(x1 * x1, axis=-1, keepdims=True)
    t = x1 * lax.rsqrt(ms + EPS) * g2_ref[layer]
    t_hi = t.astype(BF16)
    h2_ref[...] = t_hi
    t_lo = (t - t_hi.astype(F32)).astype(BF16)
    wrh = wrh_ref[layer]
    ne = wrh.shape[0]
    both = _nt_dot(jnp.concatenate([wrh, wrl_ref[layer]], axis=0), t_hi)
    logits = both[:ne] + _nt_dot(wrh, t_lo) + both[ne:]
    mx = jnp.max(logits, axis=0, keepdims=True)
    e = jnp.exp(logits - mx)
    aff_ref[...] = e / jnp.sum(e, axis=0, keepdims=True)


def _merge_out(x2d, yd, yn, gates, wbd, wbn, wo, g2, wrh, wrl, layer, tm):
    n, d = x2d.shape
    ne = wrh.shape[1]
    whole = lambda a: pl.BlockSpec(a.shape, lambda i: (0,) * a.ndim)
    row = lambda w: pl.BlockSpec((tm, w), lambda i: (i, 0))
    return pl.pallas_call(
        functools.partial(_merge_kernel, layer=layer),
        grid=(n // tm,),
        in_specs=[row(d), row(DIFF_WIDTH), row(NA_WIDTH), row(2 * d), _layer_block(wbd, layer),
                  _layer_block(wbn, layer), _layer_block(wo, layer), whole(g2), whole(wrh), whole(wrl)],
        out_specs=[row(d), row(d), pl.BlockSpec((ne, tm), lambda i: (0, i))],
        out_shape=[jax.ShapeDtypeStruct((n, d), F32), jax.ShapeDtypeStruct((n, d), BF16),
                   jax.ShapeDtypeStruct((ne, n), F32)],
        compiler_params=_cparams("parallel"),
    )(x2d, yd, yn, gates, wbd, wbn, wo, g2, wrh, wrl)


def _select_kernel(aff_ref, pos_ref, tb_ref, *, cap):
    ne, nt, _ = aff_ref.shape
    bits = lambda e: pltpu.bitcast(aff_ref[e], jnp.int32)

    def bit_step(i, thr):
        cand = thr | lax.shift_left(jnp.int32(1), 30 - i)
        rows = []
        for e in range(ne):
            ce = cand[e:e + 1]
            ge = (bits(e) >= ce).astype(jnp.int32)
            cnt = jnp.sum(jnp.sum(ge, axis=0, keepdims=True), axis=1, keepdims=True)
            rows.append(jnp.where(cnt >= cap, ce, thr[e:e + 1]))
        return jnp.concatenate(rows, axis=0)

    thr = lax.fori_loop(0, 31, bit_step, jnp.zeros((ne, LANES), jnp.int32))

    li = lax.broadcasted_iota(jnp.int32, (LANES, LANES), 0)
    lj = lax.broadcasted_iota(jnp.int32, (LANES, LANES), 1)
    upper = (li < lj).astype(BF16)
    ones = jnp.ones((LANES, LANES), BF16)
    ti = lax.broadcasted_iota(jnp.int32, (nt, nt), 0)
    tj = lax.broadcasted_iota(jnp.int32, (nt, nt), 1)
    lower = (tj < ti).astype(BF16)

    def prefix(flags):
        fb = flags.astype(BF16)
        within = jnp.dot(fb, upper, preferred_element_type=F32)
        per_tile = jnp.dot(fb, ones, preferred_element_type=F32)
        before = jnp.dot(lower, per_tile.astype(BF16), preferred_element_type=F32)
        return before + within, before

    for e in range(ne):
        be = bits(e)
        te = thr[e:e + 1]
        gt = be > te
        eq = be == te
        n_gt = jnp.sum(jnp.sum(gt.astype(F32), axis=0, keepdims=True), axis=1, keepdims=True)
        eq_rank, _ = prefix(eq)
        sel = gt | (eq & (eq_rank < cap - n_gt))
        pos, before = prefix(sel)
        pos_ref[e] = jnp.where(sel, pos, -1.0).astype(jnp.int32)
        tb_ref[e] = before


def _select(aff_t, cap):
    ne, n = aff_t.shape
    nt = n // LANES
    shp = (ne, nt, LANES)
    spec = pl.BlockSpec(shp, lambda i: (0, 0, 0))
    pos, tb = pl.pallas_call(
        functools.partial(_select_kernel, cap=cap),
        grid=(1,),
        in_specs=[spec],
        out_specs=[spec, spec],
        out_shape=[jax.ShapeDtypeStruct(shp, jnp.int32), jax.ShapeDtypeStruct(shp, F32)],
        compiler_params=_cparams("arbitrary"),
    )(aff_t.reshape(shp))
    return pos.reshape(ne, n), tb[:, :, 0].astype(jnp.int32)


def _dispatch_kernel(tb_ref, nr_ref, h_ref, pos_ref, aff_ref, xe_hbm, gc_hbm, xs_ref, gs_ref, last_ref, pend_ref,
                     sem, *, cap):
    j = pl.program_id(0)
    ne = pos_ref.shape[0]
    tt = h_ref.shape[0]

    @pl.when(j == 0)
    def _():
        xs_ref[...] = jnp.zeros(xs_ref.shape, xs_ref.dtype)
        gs_ref[...] = jnp.zeros(gs_ref.shape, F32)
        for e in range(ne):
            last_ref[e] = 0
            pend_ref[e] = 0
        pad = [(pltpu.make_async_copy(xs_ref.at[e], xe_hbm.at[e, pl.ds(cap + q * DISPATCH_WIN, DISPATCH_WIN)], sem.at[0, e]),
                pltpu.make_async_copy(gs_ref.at[e], gc_hbm.at[e, pl.ds(cap + q * DISPATCH_WIN, DISPATCH_WIN)], sem.at[1, e]))
               for e in range(ne) for q in range((xe_hbm.shape[1] - cap) // DISPATCH_WIN)]
        for cx, cg in pad:
            cx.start()
            cg.start()
        for cx, cg in pad:
            cx.wait()
            cg.wait()

    h = h_ref[...]
    slot = lax.broadcasted_iota(jnp.int32, (DISPATCH_WIN, tt), 0)
    row = lax.broadcasted_iota(jnp.int32, (BF16_ROWS, 1), 0)

    def copies(e, start):
        dst = pl.ds(pl.multiple_of(start, BF16_ROWS), DISPATCH_WIN)
        return (pltpu.make_async_copy(xs_ref.at[e], xe_hbm.at[e, dst], sem.at[0, e]),
                pltpu.make_async_copy(gs_ref.at[e], gc_hbm.at[e, dst], sem.at[1, e]))

    def drain():
        for e in range(ne):
            @pl.when(pend_ref[e] == 1)
            def _():
                for c in copies(e, 0):
                    c.wait()
                pend_ref[e] = 0

    def emit(e, start, xs_e, gate_col, n_carry, off):
        keep = row < n_carry
        gs = jnp.broadcast_to(gate_col, (DISPATCH_WIN, LANES))
        hd = BF16_ROWS
        x_head = xs_e[:hd] + jnp.where(keep, xs_ref[e, pl.ds(off, hd)].astype(F32), 0.0)
        g_head = gs[:hd] + jnp.where(keep, gs_ref[e, pl.ds(off, hd)], 0.0)
        xs_ref[e, hd:] = xs_e[hd:].astype(xs_ref.dtype)
        gs_ref[e, hd:] = gs[hd:]
        xs_ref[e, :hd] = x_head.astype(xs_ref.dtype)
        gs_ref[e, :hd] = g_head
        last_ref[e] = start
        for c in copies(e, start):
            c.start()
        pend_ref[e] = 1

    def gate_of(hit, e):
        return jnp.sum(jnp.where(hit, aff_ref[e:e + 1, :], 0.0), axis=1, keepdims=True)

    first = [tb_ref[e, j] for e in range(ne)]
    starts = [lax.shift_left(lax.shift_right_logical(first[e], 4), 4) for e in range(ne)]
    hits = [pos_ref[e:e + 1, :] - starts[e] == slot for e in range(ne)]
    xs = jnp.dot(jnp.concatenate([hit.astype(BF16) for hit in hits], axis=0), h, preferred_element_type=F32)
    gates = [gate_of(hits[e], e) for e in range(ne)]
    drain()
    for e in range(ne):
        emit(e, starts[e], xs[e * DISPATCH_WIN:(e + 1) * DISPATCH_WIN], gates[e], first[e] - starts[e],
             pl.multiple_of(starts[e] - last_ref[e], BF16_ROWS))

    def later_round(r, carry):
        for e in range(ne):
            @pl.when(tb_ref[e, j + 1] > first[e] + r * ROUTE_CHUNK)
            def _():
                start = starts[e] + r * ROUTE_CHUNK
                hit = pos_ref[e:e + 1, :] - start == slot
                xs_e = jnp.dot(hit.astype(BF16), h, preferred_element_type=F32)
                gate_col = gate_of(hit, e)

                @pl.when(pend_ref[e] == 1)
                def _():
                    for c in copies(e, 0):
                        c.wait()
                    pend_ref[e] = 0
                emit(e, start, xs_e, gate_col, 0, 0)
        return carry

    lax.fori_loop(1, nr_ref[j], later_round, 0)

    @pl.when(j == pl.num_programs(0) - 1)
    def _():
        drain()


def _dispatch(h2, pos, aff_t, tb_tiles, n_rounds, cap):
    n, d = h2.shape
    ne = pos.shape[0]
    tt = ROUTE_TILE
    assert cap % BF16_ROWS == 0
    cap_pad = cap + -(-tt // ROUTE_CHUNK) * DISPATCH_WIN
    grid_spec = pltpu.PrefetchScalarGridSpec(
        num_scalar_prefetch=2,
        grid=(n // tt,),
        in_specs=[pl.BlockSpec((tt, d), lambda j, *_: (j, 0)),
                  pl.BlockSpec((ne, tt), lambda j, *_: (0, j)),
                  pl.BlockSpec((ne, tt), lambda j, *_: (0, j))],
        out_specs=[pl.BlockSpec(memory_space=pl.ANY), pl.BlockSpec(memory_space=pl.ANY)],
        scratch_shapes=[pltpu.VMEM((ne, DISPATCH_WIN, d), BF16), pltpu.VMEM((ne, DISPATCH_WIN, LANES), F32),
                        pltpu.SMEM((ne,), jnp.int32), pltpu.SMEM((ne,), jnp.int32),
                        pltpu.SemaphoreType.DMA((2, ne))],
    )
    return pl.pallas_call(
        functools.partial(_dispatch_kernel, cap=cap),
        grid_spec=grid_spec,
        out_shape=[jax.ShapeDtypeStruct((ne, cap_pad, d), BF16), jax.ShapeDtypeStruct((ne, cap_pad, LANES), F32)],
        compiler_params=_cparams("arbitrary"),
    )(tb_tiles, n_rounds, h2, pos, aff_t)


def _combine_kernel(tb_ref, nr_ref, x_ref, post_ref, tbt_ref, spread_ref, kcol_ref, ye_hbm, o_ref, ch_ref, one_ref,
                    sem, *, cap):
    j = pl.program_id(0)
    ne = ye_hbm.shape[0]
    spread = spread_ref[...]
    kcol = kcol_ref[...]
    last = cap - COMBINE_CHUNK

    buf = j % 2

    def chunk_copies(tile, r, b):
        def start_of(e):
            s = tb_ref[e, tile] + r * ROUTE_CHUNK
            return jnp.minimum(lax.shift_left(lax.shift_right_logical(s, 4), 4), last)
        return [pltpu.make_async_copy(ye_hbm.at[e, pl.ds(pl.multiple_of(start_of(e), BF16_ROWS), COMBINE_CHUNK)],
                                      ch_ref.at[b, e], sem.at[b]) for e in range(ne)]

    @pl.when(j == 0)
    def _():
        for c in chunk_copies(0, 0, 0):
            c.start()

    @pl.when(j + 1 < pl.num_programs(0))
    def _():
        for c in chunk_copies(j + 1, 0, 1 - buf):
            c.start()

    def late_copy(e, r):
        s = tb_ref[e, j] + r * ROUTE_CHUNK
        start = jnp.minimum(lax.shift_left(lax.shift_right_logical(s, 4), 4), last)
        return pltpu.make_async_copy(ye_hbm.at[e, pl.ds(pl.multiple_of(start, BF16_ROWS), COMBINE_CHUNK)],
                                     one_ref.at[e, pl.ds(0, COMBINE_CHUNK)], sem.at[2 + e])

    def in_round(e, r):
        return tb_ref[e, j + 1] > tb_ref[e, j] + r * ROUTE_CHUNK

    @pl.when(j == 0)
    def _():
        one_ref[:, COMBINE_CHUNK:, :] = jnp.zeros((ne, LANES - COMBINE_CHUNK, one_ref.shape[-1]), one_ref.dtype)

    for e in range(ne):
        @pl.when(in_round(e, 1))
        def _():
            late_copy(e, 1).start()

    post = post_ref[...]

    def rel_slots(r):
        first = tbt_ref[pl.ds(j, 1), :] + (r * ROUTE_CHUNK).astype(F32)
        sv = jnp.minimum(jnp.floor(first * (1.0 / BF16_ROWS)) * BF16_ROWS, float(last))
        mine = (post >= first) & (post < first + ROUTE_CHUNK)
        return jnp.where(mine, post - sv, -1.0).astype(BF16)

    rel_wide = jnp.dot(rel_slots(jnp.int32(0)), spread, preferred_element_type=F32)
    onehot = (rel_wide == kcol).astype(BF16)
    for c in chunk_copies(j, 0, buf):
        c.wait()
    rows = ch_ref[buf].reshape(ne * COMBINE_CHUNK, ch_ref.shape[-1])
    o_ref[...] = x_ref[...] + jnp.dot(onehot, rows, preferred_element_type=F32)


    lane_row = lax.broadcasted_iota(jnp.int32, (LANES, LANES), 0)
    k_lane = lax.broadcasted_iota(jnp.int32, (1, LANES), 1).astype(F32)

    def later_round(r, carry):
        rel = rel_slots(r)
        for e in range(ne):
            @pl.when(tb_ref[e, j + 1] > tb_ref[e, j] + r * ROUTE_CHUNK)
            def _():
                cp = late_copy(e, r)

                @pl.when(r > 1)
                def _():
                    cp.start()
                pick = (lane_row == e).astype(BF16)
                rel_e = jnp.dot(rel, pick, preferred_element_type=F32)
                onehot_e = (rel_e == k_lane).astype(BF16)
                cp.wait()
                o_ref[...] += jnp.dot(onehot_e, one_ref[e], preferred_element_type=F32)
        return carry

    lax.fori_loop(1, nr_ref[j], later_round, 0)


def _combine(x1, pos_t, tb_tiles, tb_t, n_rounds, ye, cap):
    n, d = x1.shape
    ne = ye.shape[0]
    tt = ROUTE_TILE
    assert cap >= COMBINE_CHUNK and cap % BF16_ROWS == 0
    col = np.arange(ne * COMBINE_CHUNK)
    spread = jnp.asarray(np.arange(LANES)[:, None] == (col // COMBINE_CHUNK)[None, :], BF16)
    kcol = jnp.asarray((col % COMBINE_CHUNK)[None, :], F32)
    grid_spec = pltpu.PrefetchScalarGridSpec(
        num_scalar_prefetch=2,
        grid=(n // tt,),
        in_specs=[pl.BlockSpec((tt, d), lambda j, *_: (j, 0)),
                  pl.BlockSpec((tt, LANES), lambda j, *_: (j, 0)),
                  pl.BlockSpec(tb_t.shape, lambda j, *_: (0, 0)),
                  pl.BlockSpec(spread.shape, lambda j, *_: (0, 0)),
                  pl.BlockSpec(kcol.shape, lambda j, *_: (0, 0)),
                  pl.BlockSpec(memory_space=pl.ANY)],
        out_specs=pl.BlockSpec((tt, d), lambda j, *_: (j, 0)),
        scratch_shapes=[pltpu.VMEM((2, ne, COMBINE_CHUNK, d), BF16), pltpu.VMEM((ne, LANES, d), BF16),
                        pltpu.SemaphoreType.DMA((2 + ne,))],
    )
    return pl.pallas_call(
        functools.partial(_combine_kernel, cap=cap),
        grid_spec=grid_spec,
        out_shape=jax.ShapeDtypeStruct((n, d), F32),
        compiler_params=_cparams("arbitrary"),
    )(tb_tiles, n_rounds, x1, pos_t, tb_t, spread, kcol, ye)


def _expert_kernel(xe_ref, gate_ref, wg_ref, wu_ref, wd_ref, o_ref, *, ff_chunk):
    x = xe_ref[0]
    ff = wg_ref.shape[-1]
    acc = None
    for c in range(0, ff, ff_chunk):
        w = min(ff_chunk, ff - c)
        g = jnp.dot(x, wg_ref[0, 0, :, c:c + w], preferred_element_type=F32)
        u = jnp.dot(x, wu_ref[0, 0, :, c:c + w], preferred_element_type=F32)
        hmid = (g / (1.0 + jnp.exp(-g)) * u).astype(BF16)
        part = jnp.dot(hmid, wd_ref[0, 0, c:c + w, :], preferred_element_type=F32)
        acc = part if acc is None else acc + part
    gate = gate_ref[0]
    gate = jnp.concatenate([gate] * (acc.shape[1] // LANES), axis=1)
    o_ref[0] = (acc * gate).astype(o_ref.dtype)


def _expert_ffn(xe, gate, wg, wu, wd, layer, cap, tm, ff_chunk):
    ne, _, d = xe.shape
    ff = wg.shape[-1]
    return pl.pallas_call(
        functools.partial(_expert_kernel, ff_chunk=ff_chunk),
        grid=(ne, cap // tm),
        in_specs=[pl.BlockSpec((1, tm, d), lambda e, m: (e, m, 0)),
                  pl.BlockSpec((1, tm, LANES), lambda e, m: (e, m, 0)),
                  pl.BlockSpec((1, 1, d, ff), lambda e, m: (layer, e, 0, 0)),
                  pl.BlockSpec((1, 1, d, ff), lambda e, m: (layer, e, 0, 0)),
                  pl.BlockSpec((1, 1, ff, d), lambda e, m: (layer, e, 0, 0))],
        out_specs=pl.BlockSpec((1, tm, d), lambda e, m: (e, m, 0)),
        out_shape=jax.ShapeDtypeStruct((ne, cap, d), BF16),
        compiler_params=_cparams("parallel", "arbitrary"),
    )(xe, gate, wg, wu, wd)


def _pick_tile(n, pref):
    t = min(n, pref)
    assert n % t == 0
    return t


EXPERT_FF_CHUNK = 2 * MXU_DIM


def kernel(x_prompt, x_sample, norm1_g, w_in, diff_q_norm, diff_k_norm, lambda_q1, lambda_k1, lambda_q2,
           lambda_k2, diff_subln_g, rel_bias, na_q_norm, na_k_norm, na_rpb, w_branch_diff, w_branch_na,
           w_out, norm2_g, w_router, w_expert_gate, w_expert_up, w_expert_down):
    depth, d_model, _ = w_in.shape
    n_experts = w_router.shape[-1]
    ff = w_expert_gate.shape[-1]

    w_in_b = w_in.astype(BF16)
    wbd_b = w_branch_diff.astype(BF16)
    wbn_b = w_branch_na.astype(BF16)
    wo_b = w_out.astype(BF16)
    wg_b = w_expert_gate.astype(BF16)
    wu_b = w_expert_up.astype(BF16)
    wd_b = w_expert_down.astype(BF16)
    wr_t = jnp.swapaxes(w_router.astype(F32), 1, 2)
    wr_hi = wr_t.astype(BF16)
    wr_lo = (wr_t - wr_hi.astype(F32)).astype(BF16)
    g1 = norm1_g.astype(F32)[:, None, :]
    g2 = norm2_g.astype(F32)[:, None, :]
    tile_gain = lambda g, reps, scale: (jnp.tile(g.astype(F32), (1, reps)) * scale)[:, None, :]
    dqg = tile_gain(diff_q_norm, 2 * DIFF_HEADS, DIFF_HEAD_DIM ** -0.5 * LOG2E)
    dkg = tile_gain(diff_k_norm, 2 * DIFF_HEADS, 1.0)
    nqg = tile_gain(na_q_norm, NA_HEADS, NA_HEAD_DIM ** -0.5 * LOG2E)
    nkg = tile_gain(na_k_norm, NA_HEADS, 1.0)
    subln = diff_subln_g.astype(F32)[:, None, :]
    lam_params = jnp.stack([lambda_q1, lambda_k1, lambda_q2, lambda_k2], axis=1).astype(F32)
    seg_id = jnp.arange(MXU_DIM) // DIFF_HEAD_DIM
    seg = jnp.where(seg_id[:, None] == seg_id[None, :], 1.0 / DIFF_HEAD_DIM, 0.0).astype(BF16)
    na_bias = [_na_bias(na_rpb[l]) for l in range(depth)]

    def run(x):
        b, s, _ = x.shape
        n = b * s
        cap = CAPACITY_FACTOR * n // n_experts
        tm = _pick_tile(n, 512)
        t_attn = _pick_tile(s, 1024)
        t5_tiles = _t5_bias_tiles(rel_bias, t_attn, min(s, MXU_DIM))
        x2d = x.reshape(n, d_model)
        for l in range(depth):
            lambda_init = 0.8 - 0.6 * math.exp(-0.3 * l)
            dq, dk, dv, nq, nk, nv, gates = _in_proj(x2d, l, g1, w_in_b, seg, dqg, dkg, nqg, nkg,
                                                     _pick_tile(n, 1024))
            r3 = lambda a: a.reshape(b, s, a.shape[-1])
            yd = _diff_attention(r3(dq), r3(dk), r3(dv), t5_tiles, lam_params, subln, l, lambda_init, t_attn)
            yn = _na_attention(r3(nq), r3(nk), r3(nv), na_bias[l], NA_ROWS_PER_STEP)
            x1, h2, aff_t = _merge_out(x2d, yd.reshape(n, DIFF_WIDTH), yn.reshape(n, NA_WIDTH), gates,
                                       wbd_b, wbn_b, wo_b, g2, wr_hi, wr_lo, l, tm)
            pos, tb128 = _select(aff_t, cap)
            per = ROUTE_TILE // LANES
            tb_tiles = jnp.concatenate([tb128[:, ::per], jnp.full((n_experts, 1), cap, jnp.int32)], axis=1)
            counts = tb_tiles[:, 1:] - tb_tiles[:, :-1]
            n_rounds = jnp.maximum(jnp.max(-(-counts // ROUTE_CHUNK), axis=0), 1).astype(jnp.int32)
            tb_t = jnp.pad(tb_tiles.T.astype(F32), ((0, 7), (0, LANES - n_experts)))
            pos_t = jnp.pad(pos.T.astype(F32), ((0, 0), (0, LANES - n_experts)), constant_values=-1.0)
            xe, gate_rows = _dispatch(h2, pos, aff_t, tb_tiles, n_rounds, cap)
            ye = _expert_ffn(xe, gate_rows, wg_b, wu_b, wd_b, l, cap, _pick_tile(cap, 512), EXPERT_FF_CHUNK)
            x2d = _combine(x1, pos_t, tb_tiles, tb_t, n_rounds, ye, cap)
        return x2d.reshape(b, s, d_model)

    return (run(x_prompt), run(x_sample))
```
